```python
import math
import jax, jax.numpy as jnp
from jax import lax
import numpy as np

D_MODEL = 2048
BATCH = 4
SEQ = 2048
DEPTH = 1
DEC_BATCH = 128
DEC_SEQ = 1
PAST_LEN = 16384
PAGE_SIZE = 128

RET_WIDTH = 1024
RET_HEADS = 8
RET_DK = RET_WIDTH // RET_HEADS
RET_DV = RET_WIDTH // RET_HEADS
RET_CHUNK = 128
ROPE_BASE = 10000.0
S5_WIDTH = 1024
S5_GROUP = 16
S5_GROUPS = S5_WIDTH // S5_GROUP
S5_P = 64
S5_CHUNK = 128
DT_MIN = 1e-3
DT_MAX = 1e-1
EPS = 1e-6
IN_COLS = 4 * RET_WIDTH + 2 * S5_WIDTH + 2 * D_MODEL

kernel_name = 'hybrid_retention_s5_gated_step'

F32 = jnp.float32


def rmsnorm(x, g):
    xf = x.astype(F32)
    y = xf * lax.rsqrt(jnp.mean(xf * xf, axis=-1, keepdims=True) + EPS) * g.astype(F32)
    return y.astype(x.dtype)


def rope(x, pos):
    half = x.shape[-1] // 2
    inv = ROPE_BASE ** (-jnp.arange(half, dtype=F32) / half)
    ang = pos.astype(F32)[:, None] * inv[None, :]
    cos = jnp.cos(ang)[None, :, None, :]
    sin = jnp.sin(ang)[None, :, None, :]
    x1, x2 = x[..., :half], x[..., half:]
    return jnp.concatenate([x1 * cos - x2 * sin, x1 * sin + x2 * cos], axis=-1)


def ret_log_gamma():
    return jnp.log(1.0 - 2.0 ** (-5.0 - jnp.arange(RET_HEADS, dtype=F32)))


def retention_chunk(q, k, v, s0, lg):
    c = q.shape[2]
    idx = jnp.arange(c, dtype=F32)
    diff = idx[:, None] - idx[None, :]
    mask = jnp.where(diff[None] >= 0, jnp.exp(jnp.maximum(diff, 0.0)[None] * lg[:, None, None]), 0.0)
    scores = jnp.einsum('bhqd,bhkd->bhqk', q, k) * mask[None]
    inner = jnp.einsum('bhqk,bhkv->bhqv', scores, v)
    q_dec = q * jnp.exp((idx[None, :] + 1.0) * lg[:, None])[None, :, :, None]
    cross = jnp.einsum('bhqd,bhdv->bhqv', q_dec, s0)
    k_dec = k * jnp.exp((c - 1.0 - idx[None, :]) * lg[:, None])[None, :, :, None]
    s_new = jnp.exp(c * lg)[None, :, None, None] * s0 + jnp.einsum('bhkd,bhkv->bhdv', k_dec, v)
    return inner + cross, s_new


def retention(q, k, v, pos, s0):
    bn, L, _ = q.shape
    q = rope(q.reshape(bn, L, RET_HEADS, RET_DK).astype(F32), pos)
    k = rope(k.reshape(bn, L, RET_HEADS, RET_DK).astype(F32), pos) * (RET_DK ** -0.5)
    v = v.reshape(bn, L, RET_HEADS, RET_DV).astype(F32)
    lg = ret_log_gamma()
    c = RET_CHUNK if L % RET_CHUNK == 0 else L
    nc = L // c

    def to_chunks(t):
        return t.reshape(bn, nc, c, RET_HEADS, t.shape[-1]).transpose(1, 0, 3, 2, 4)

    def step(s, qkv):
        qc, kc, vc = qkv
        o, s = retention_chunk(qc, kc, vc, s, lg)
        return s, o

    s_fin, o = lax.scan(step, s0.astype(F32), (to_chunks(q), to_chunks(k), to_chunks(v)))
    o = o.transpose(1, 0, 3, 2, 4).reshape(bn, L, RET_HEADS, RET_DV)
    mu = jnp.mean(o, axis=-1, keepdims=True)
    var = jnp.mean(jnp.square(o - mu), axis=-1, keepdims=True)
    o = (o - mu) * lax.rsqrt(var + EPS)
    return o.reshape(bn, L, RET_WIDTH), s_fin


def s5_combine(e1, e2):
    a1r, a1i, b1r, b1i = e1
    a2r, a2i, b2r, b2i = e2
    ar = a1r * a2r - a1i * a2i
    ai = a1r * a2i + a1i * a2r
    br = a2r * b1r - a2i * b1i + b2r
    bi = a2r * b1i + a2i * b1r + b2i
    return ar, ai, br, bi


def s5(u, x0r, x0i, lam_re, lam_im, log_dt, b_re, b_im, c_re, c_im, d_skip, w_glu, b_glu):
    bn, L, _ = u.shape
    u = u.reshape(bn, L, S5_GROUPS, S5_GROUP).astype(F32)
    lr, li = lam_re.astype(F32), lam_im.astype(F32)
    dt = jnp.exp(log_dt.astype(F32))[:, None]
    mag = jnp.exp(lr * dt)
    abar_r = mag * jnp.cos(li * dt)
    abar_i = mag * jnp.sin(li * dt)
    nr, ni = abar_r - 1.0, abar_i
    den = lr * lr + li * li
    coef_r = (nr * lr + ni * li) / den
    coef_i = (ni * lr - nr * li) / den
    br, bi = b_re.astype(F32), b_im.astype(F32)
    bbar_r = coef_r[:, :, None] * br - coef_i[:, :, None] * bi
    bbar_i = coef_r[:, :, None] * bi + coef_i[:, :, None] * br
    cr, ci = c_re.astype(F32), c_im.astype(F32)
    dsk = d_skip.astype(F32)
    c = S5_CHUNK if L % S5_CHUNK == 0 else L
    nc = L // c
    uc = u.reshape(bn, nc, c, S5_GROUPS, S5_GROUP).transpose(1, 0, 2, 3, 4)

    def step(carry, u_c):
        xr0, xi0 = carry
        bur = jnp.einsum('gpn,bcgn->bcgp', bbar_r, u_c)
        bui = jnp.einsum('gpn,bcgn->bcgp', bbar_i, u_c)
        ar = jnp.broadcast_to(abar_r, bur.shape)
        ai = jnp.broadcast_to(abar_i, bur.shape)
        A_r, A_i, X_r, X_i = lax.associative_scan(s5_combine, (ar, ai, bur, bui), axis=1)
        xr = X_r + A_r * xr0[:, None] - A_i * xi0[:, None]
        xi = X_i + A_r * xi0[:, None] + A_i * xr0[:, None]
        y = (jnp.einsum('gnp,bcgp->bcgn', cr, xr) - jnp.einsum('gnp,bcgp->bcgn', ci, xi)
             + dsk[None, None] * u_c)
        return (xr[:, -1], xi[:, -1]), y

    (sr, si), y = lax.scan(step, (x0r.astype(F32), x0i.astype(F32)), uc)
    y = y.transpose(1, 0, 2, 3, 4).reshape(bn, L, S5_WIDTH)
    y = jax.nn.gelu(y)
    y = y * jax.nn.sigmoid(y @ w_glu.astype(F32) + b_glu.astype(F32))
    return y, sr, si


def layer(x, pos, s_ret, s_re, s_im, g_pre, w_in, w_pa, w_pb, w_out, g_post,
          lam_re, lam_im, log_dt, b_re, b_im, c_re, c_im, d_skip, w_glu, b_glu):
    h = rmsnorm(x, g_pre)
    proj = h @ w_in
    splits = np.cumsum([RET_WIDTH, RET_WIDTH, RET_WIDTH, RET_WIDTH, S5_WIDTH, S5_WIDTH, D_MODEL]).tolist()
    q, k, v, za, ub, zb, ga, gb = jnp.split(proj, splits, axis=-1)
    ya, s_ret_new = retention(q, k, v, pos, s_ret)
    ya = (ya * jax.nn.silu(za.astype(F32))).astype(x.dtype) @ w_pa
    yb, s_re_new, s_im_new = s5(ub, s_re, s_im, lam_re, lam_im, log_dt, b_re, b_im, c_re, c_im, d_skip, w_glu, b_glu)
    yb = (yb * jax.nn.silu(zb.astype(F32))).astype(x.dtype) @ w_pb
    merged = jax.nn.sigmoid(ga) * ya + jax.nn.sigmoid(gb) * yb
    out = merged @ w_out
    return x + rmsnorm(out, g_post), s_ret_new, s_re_new, s_im_new


def setup_inputs(seed: int = 0) -> dict:
    key = jax.random.key(seed)
    ks = jax.random.split(key, 24)
    nrm = jax.random.normal
    n = jnp.arange(S5_P, dtype=F32)
    return {
        'x_prompt': nrm(ks[0], (BATCH, SEQ, D_MODEL), F32),
        'x_sample': nrm(ks[1], (DEC_BATCH, DEC_SEQ, D_MODEL), F32),
        'state_ret': 0.5 * nrm(ks[2], (DEPTH, DEC_BATCH, RET_HEADS, RET_DK, RET_DV), F32),
        'state_s5_re': 0.1 * nrm(ks[3], (DEPTH, DEC_BATCH, S5_GROUPS, S5_P), F32),
        'state_s5_im': 0.1 * nrm(ks[4], (DEPTH, DEC_BATCH, S5_GROUPS, S5_P), F32),
        'g_pre': 1.0 + 0.05 * nrm(ks[5], (DEPTH, D_MODEL), F32),
        'w_in': nrm(ks[6], (DEPTH, D_MODEL, IN_COLS), F32) * D_MODEL ** -0.5,
        'w_pa': nrm(ks[7], (DEPTH, RET_WIDTH, D_MODEL), F32) * RET_WIDTH ** -0.5,
        'w_pb': nrm(ks[8], (DEPTH, S5_WIDTH, D_MODEL), F32) * S5_WIDTH ** -0.5,
        'w_out': nrm(ks[9], (DEPTH, D_MODEL, D_MODEL), F32) * D_MODEL ** -0.5,
        'g_post': 1.0 + 0.05 * nrm(ks[10], (DEPTH, D_MODEL), F32),
        's5_lam_re': -0.5 + 0.01 * nrm(ks[11], (DEPTH, S5_GROUPS, S5_P), F32),
        's5_lam_im': math.pi * n + 0.01 * nrm(ks[12], (DEPTH, S5_GROUPS, S5_P), F32),
        's5_log_dt': jax.random.uniform(ks[13], (DEPTH, S5_GROUPS), F32, math.log(DT_MIN), math.log(DT_MAX)),
        's5_b_re': nrm(ks[14], (DEPTH, S5_GROUPS, S5_P, S5_GROUP), F32) * (2 * S5_GROUP) ** -0.5,
        's5_b_im': nrm(ks[15], (DEPTH, S5_GROUPS, S5_P, S5_GROUP), F32) * (2 * S5_GROUP) ** -0.5,
        's5_c_re': nrm(ks[16], (DEPTH, S5_GROUPS, S5_GROUP, S5_P), F32) * S5_P ** -0.5,
        's5_c_im': nrm(ks[17], (DEPTH, S5_GROUPS, S5_GROUP, S5_P), F32) * S5_P ** -0.5,
        's5_d': nrm(ks[18], (DEPTH, S5_GROUPS, S5_GROUP), F32),
        's5_w_glu': nrm(ks[19], (DEPTH, S5_WIDTH, S5_WIDTH), F32) * S5_WIDTH ** -0.5,
        's5_b_glu': 0.01 * nrm(ks[20], (DEPTH, S5_WIDTH), F32),
    }


def reference(x_prompt, x_sample, state_ret, state_s5_re, state_s5_im, g_pre, w_in, w_pa, w_pb, w_out, g_post,
              s5_lam_re, s5_lam_im, s5_log_dt, s5_b_re, s5_b_im, s5_c_re, s5_c_im, s5_d, s5_w_glu, s5_b_glu):
    bp, lp, _ = x_prompt.shape
    ds = x_sample.shape[1]
    pos_p = jnp.arange(lp, dtype=jnp.int32)
    pos_s = PAST_LEN + jnp.arange(ds, dtype=jnp.int32)
    hp, hs = x_prompt, x_sample
    rp, rep, imp, rs, res, ims = [], [], [], [], [], []
    for l in range(DEPTH):
        w = (g_pre[l], w_in[l], w_pa[l], w_pb[l], w_out[l], g_post[l], s5_lam_re[l], s5_lam_im[l], s5_log_dt[l],
             s5_b_re[l], s5_b_im[l], s5_c_re[l], s5_c_im[l], s5_d[l], s5_w_glu[l], s5_b_glu[l])
        zr = jnp.zeros((bp, RET_HEADS, RET_DK, RET_DV), F32)
        zs = jnp.zeros((bp, S5_GROUPS, S5_P), F32)
        hp, a, b, c = layer(hp, pos_p, zr, zs, zs, *w)
        rp.append(a); rep.append(b); imp.append(c)
        hs, a, b, c = layer(hs, pos_s, state_ret[l], state_s5_re[l], state_s5_im[l], *w)
        rs.append(a); res.append(b); ims.append(c)
    return (hp, hs, jnp.stack(rp), jnp.stack(rep), jnp.stack(imp), jnp.stack(rs), jnp.stack(res), jnp.stack(ims))
```

```python
import functools
import math

import jax
import jax.numpy as jnp
import numpy as np
from jax import lax
from jax.experimental import pallas as pl
from jax.experimental.pallas import tpu as pltpu

F32 = jnp.float32
BF16 = jnp.bfloat16

D_MODEL = 2048
RET_WIDTH = 1024
RET_HEADS = 8
HEAD_DIM = RET_WIDTH // RET_HEADS
RET_CHUNK = 128
ROPE_BASE = 10000.0
S5_WIDTH = 1024
S5_GROUP = 16
S5_GROUPS = S5_WIDTH // S5_GROUP
S5_P = 64
PAST_LEN = 16384
EPS = 1e-6
IN_COLS = 4 * RET_WIDTH + 2 * S5_WIDTH + 2 * D_MODEL

LANES = 128
S5_PAIRS = S5_GROUPS // 2
PAIRS_PER_BLOCK = LANES // (2 * S5_GROUP)
S5_BLOCKS = S5_WIDTH // LANES
S5_PITCH = 40
VMEM_LIMIT = 56 * 1024 * 1024

_LOG_GAMMA = [float(np.log(np.float32(1.0) - np.float32(2.0) ** np.float32(-5.0 - h)))
              for h in range(RET_HEADS)]


def _params(*sem):
    return pltpu.CompilerParams(dimension_semantics=sem, vmem_limit_bytes=VMEM_LIMIT)


def _resident(shape):
    nd = len(shape)
    return pl.BlockSpec(shape, lambda *_: (0,) * nd, pipeline_mode=pl.Buffered(1))


def _inproj_kernel(x_ref, g_ref, w_ref, o_ref, h_ref, *, slab):
    @pl.when(pl.program_id(1) == 0)
    def _():
        g = g_ref[...]

        def body(r, _):
            rows = pl.ds(pl.multiple_of(r * slab, slab), slab)
            x = x_ref[rows, :]
            ms = jnp.mean(x * x, axis=-1, keepdims=True)
            h_ref[rows, :] = (x * lax.rsqrt(ms + EPS) * g).astype(BF16)
            return 0

        lax.fori_loop(0, x_ref.shape[0] // slab, body, 0)

    o_ref[...] = jnp.dot(h_ref[...], w_ref[...], preferred_element_type=F32)


def _inproj(x2d, g_pre, w_in_bf16, *, tm, tn):
    m, d = x2d.shape
    n = w_in_bf16.shape[1]
    return pl.pallas_call(
        functools.partial(_inproj_kernel, slab=min(tm, 128)),
        grid=(m // tm, n // tn),
        in_specs=[
            pl.BlockSpec((tm, d), lambda i, j: (i, 0)),
            pl.BlockSpec((1, d), lambda i, j: (0, 0)),
            pl.BlockSpec((d, tn), lambda i, j: (0, j)),
        ],
        out_specs=pl.BlockSpec((tm, tn), lambda i, j: (i, j)),
        out_shape=jax.ShapeDtypeStruct((m, n), F32),
        scratch_shapes=[pltpu.VMEM((tm, d), BF16)],
        compiler_params=_params("arbitrary", "arbitrary"),
        name="inproj",
    )(x2d, g_pre, w_in_bf16)


def _rope(x, cos, sin_signed):
    return x * cos + pltpu.roll(x, HEAD_DIM // 2, 1) * sin_signed


def _group_norm(o):
    mu = jnp.mean(o, axis=-1, keepdims=True)
    d = o - mu
    var = jnp.mean(d * d, axis=-1, keepdims=True)
    return d * lax.rsqrt(var + EPS)


def _retention_kernel(q_ref, k_ref, v_ref, z_ref, cos_ref, sin_ref, o_ref, s_ref, *, n_chunks):
    c = RET_CHUNK

    @pl.when(pl.program_id(1) == 0)
    def _():
        s_ref[...] = jnp.zeros_like(s_ref)

    row = lax.broadcasted_iota(jnp.int32, (c, c), 0).astype(F32)
    col = lax.broadcasted_iota(jnp.int32, (c, c), 1).astype(F32)
    diff = row - col
    for h in range(RET_HEADS):
        lg = _LOG_GAMMA[h]
        mask = jnp.where(diff >= 0, jnp.exp(jnp.maximum(diff, 0.0) * lg), 0.0)
        q_decay = jnp.exp((row + 1.0) * lg)
        k_decay = jnp.exp((c - 1.0 - row) * lg)
        chunk_decay = math.exp(c * lg)
        cols = slice(h * HEAD_DIM, (h + 1) * HEAD_DIM)
        for ci in range(n_chunks):
            rows = slice(ci * c, (ci + 1) * c)
            cos = cos_ref[rows, :]
            sin = sin_ref[rows, :]
            q = _rope(q_ref[rows, cols], cos, sin)
            k = _rope(k_ref[rows, cols], cos, sin) * (HEAD_DIM ** -0.5)
            v = v_ref[rows, cols].astype(BF16)
            s0 = s_ref[0, h]
            qb = q.astype(BF16)
            scores = lax.dot_general(qb, k.astype(BF16), (((1,), (1,)), ((), ())),
                                     preferred_element_type=F32) * mask
            inner = jnp.dot(scores.astype(BF16), v, preferred_element_type=F32)
            cross = jnp.dot(qb, s0.astype(BF16), preferred_element_type=F32) * q_decay
            kd = (k * k_decay).astype(BF16)
            s_ref[0, h] = chunk_decay * s0 + lax.dot_general(
                kd, v, (((0,), (0,)), ((), ())), preferred_element_type=F32)
            o = _group_norm(inner + cross)
            o_ref[rows, cols] = (o * jax.nn.silu(z_ref[rows, cols])).astype(BF16)


def _retention_prompt(proj, cos, sin, *, batch, seq, tm):
    nt = seq // tm
    row_block = lambda b, i: b * nt + i
    col_spec = lambda cb: pl.BlockSpec((tm, RET_WIDTH), lambda b, i: (row_block(b, i), cb))
    return pl.pallas_call(
        functools.partial(_retention_kernel, n_chunks=tm // RET_CHUNK),
        grid=(batch, nt),
        in_specs=[
            col_spec(0), col_spec(1), col_spec(2), col_spec(3),
            pl.BlockSpec((tm, HEAD_DIM), lambda b, i: (i, 0)),
            pl.BlockSpec((tm, HEAD_DIM), lambda b, i: (i, 0)),
        ],
        out_specs=[
            pl.BlockSpec((tm, RET_WIDTH), lambda b, i: (row_block(b, i), 0)),
            pl.BlockSpec((1, RET_HEADS, HEAD_DIM, HEAD_DIM), lambda b, i: (b, 0, 0, 0)),
        ],
        out_shape=[
            jax.ShapeDtypeStruct((batch * seq, RET_WIDTH), BF16),
            jax.ShapeDtypeStruct((batch, RET_HEADS, HEAD_DIM, HEAD_DIM), F32),
        ],
        compiler_params=_params("arbitrary", "arbitrary"),
        name="retention_prompt",
    )(proj, proj, proj, proj, cos, sin)


def _retention_step_kernel(q_ref, k_ref, v_ref, z_ref, cos_ref, sin_ref, s_ref, o_ref, sn_ref, acc_ref,
                           *, bb):
    cos = cos_ref[...]
    sin = sin_ref[...]
    pad = jnp.zeros((HEAD_DIM - bb, HEAD_DIM), F32)
    for h in range(RET_HEADS):
        gamma = math.exp(_LOG_GAMMA[h])
        cols = slice(h * HEAD_DIM, (h + 1) * HEAD_DIM)
        q = _rope(q_ref[:, cols], cos, sin)
        k = _rope(k_ref[:, cols], cos, sin) * (HEAD_DIM ** -0.5)
        v = v_ref[:, cols]
        qt = jnp.concatenate([q, pad], axis=0).T
        kt = jnp.concatenate([k, pad], axis=0).T
        for b in range(bb):
            s_new = gamma * s_ref[b, h] + kt[:, b:b + 1] * v[b:b + 1, :]
            sn_ref[b, h] = s_new
            acc_ref[b:b + 1, cols] = jnp.sum(qt[:, b:b + 1] * s_new, axis=0, keepdims=True)
    for h in range(RET_HEADS):
        cols = slice(h * HEAD_DIM, (h + 1) * HEAD_DIM)
        o_ref[:, cols] = _group_norm(acc_ref[:, cols]) * jax.nn.silu(z_ref[:, cols])


def _retention_sample(proj, cos, sin, state, *, bb):
    nb = state.shape[0]
    col_spec = lambda cb: pl.BlockSpec((bb, RET_WIDTH), lambda i: (i, cb))
    state_spec = pl.BlockSpec((bb, RET_HEADS, HEAD_DIM, HEAD_DIM), lambda i: (i, 0, 0, 0))
    return pl.pallas_call(
        functools.partial(_retention_step_kernel, bb=bb),
        grid=(nb // bb,),
        in_specs=[
            col_spec(0), col_spec(1), col_spec(2), col_spec(3),
            pl.BlockSpec((1, HEAD_DIM), lambda i: (0, 0)),
            pl.BlockSpec((1, HEAD_DIM), lambda i: (0, 0)),
            state_spec,
        ],
        out_specs=[pl.BlockSpec((bb, RET_WIDTH), lambda i: (i, 0)), state_spec],
        out_shape=[
            jax.ShapeDtypeStruct((nb, RET_WIDTH), F32),
            jax.ShapeDtypeStruct(state.shape, F32),
        ],
        scratch_shapes=[pltpu.VMEM((bb, RET_WIDTH), F32)],
        compiler_params=_params("arbitrary"),
        name="retention_sample",
    )(proj, proj, proj, proj, cos, sin, state)


def _s5_output_gate(y, z, wglu_ref, bglu_ref):
    y = jax.nn.gelu(y)
    g = jnp.dot(y.astype(BF16), wglu_ref[...], preferred_element_type=F32) + bglu_ref[...]
    return y * jax.nn.sigmoid(g) * jax.nn.silu(z)


def _s5_prompt_kernel(u_ref, z_ref, wb_ref, cw_ref, ar_ref, ai_ref, d_ref, wglu_ref, bglu_ref,
                      o_ref, sr_ref, si_ref, xr_scr, xi_scr, y_scr, *, tm):
    @pl.when(pl.program_id(1) == 0)
    def _():
        sr_ref[...] = jnp.zeros_like(sr_ref)
        si_ref[...] = jnp.zeros_like(si_ref)

    for blk in range(S5_BLOCKS):
        ub = u_ref[:, blk * LANES:(blk + 1) * LANES].astype(BF16)
        for j in range(blk * PAIRS_PER_BLOCK, (blk + 1) * PAIRS_PER_BLOCK):
            bu = jnp.dot(ub, wb_ref[j], preferred_element_type=F32)
            xr_scr[pl.ds(j, tm, stride=S5_PITCH), :] = bu[:, :LANES]
            xi_scr[pl.ds(j, tm, stride=S5_PITCH), :] = bu[:, LANES:]

    ar = ar_ref[...]
    ai = ai_ref[...]

    def step(t, carry):
        xr, xi = carry
        rows = pl.ds(pl.multiple_of(t * S5_PITCH, 8), S5_PAIRS)
        nr = ar * xr - ai * xi + xr_scr[rows, :]
        ni = ar * xi + ai * xr + xi_scr[rows, :]
        xr_scr[rows, :] = nr
        xi_scr[rows, :] = ni
        return nr, ni

    xr, xi = lax.fori_loop(0, tm, step, (sr_ref[0], si_ref[0]), unroll=4)
    sr_ref[0] = xr
    si_ref[0] = xi

    for blk in range(S5_BLOCKS):
        cols = slice(blk * LANES, (blk + 1) * LANES)
        acc = d_ref[:, cols] * u_ref[:, cols]
        for j in range(blk * PAIRS_PER_BLOCK, (blk + 1) * PAIRS_PER_BLOCK):
            x = jnp.concatenate([xr_scr[pl.ds(j, tm, stride=S5_PITCH), :],
                                 xi_scr[pl.ds(j, tm, stride=S5_PITCH), :]], axis=1)
            acc = acc + jnp.dot(x.astype(BF16), cw_ref[j], preferred_element_type=F32)
        y_scr[:, cols] = acc
    o_ref[...] = _s5_output_gate(y_scr[...], z_ref[...], wglu_ref, bglu_ref).astype(BF16)


def _s5_prompt(proj, wb, cw, ar, ai, d, wglu, bglu, *, batch, seq, tm):
    nt = seq // tm
    row_block = lambda b, i: b * nt + i
    ub_col = (4 * RET_WIDTH) // S5_WIDTH
    state_spec = pl.BlockSpec((1, S5_PAIRS, LANES), lambda b, i: (b, 0, 0))
    return pl.pallas_call(
        functools.partial(_s5_prompt_kernel, tm=tm),
        grid=(batch, nt),
        in_specs=[
            pl.BlockSpec((tm, S5_WIDTH), lambda b, i: (row_block(b, i), ub_col)),
            pl.BlockSpec((tm, S5_WIDTH), lambda b, i: (row_block(b, i), ub_col + 1)),
            _resident(wb.shape), _resident(cw.shape), _resident(ar.shape), _resident(ai.shape),
            _resident(d.shape), _resident(wglu.shape), _resident(bglu.shape),
        ],
        out_specs=[
            pl.BlockSpec((tm, S5_WIDTH), lambda b, i: (row_block(b, i), 0)),
            state_spec, state_spec,
        ],
        out_shape=[
            jax.ShapeDtypeStruct((batch * seq, S5_WIDTH), BF16),
            jax.ShapeDtypeStruct((batch, S5_PAIRS, LANES), F32),
            jax.ShapeDtypeStruct((batch, S5_PAIRS, LANES), F32),
        ],
        scratch_shapes=[
            pltpu.VMEM((tm * S5_PITCH, LANES), F32),
            pltpu.VMEM((tm * S5_PITCH, LANES), F32),
            pltpu.VMEM((tm, S5_WIDTH), F32),
        ],
        compiler_params=_params("arbitrary", "arbitrary"),
        name="s5_prompt",
    )(proj, proj, wb, cw, ar, ai, d, wglu, bglu)


def _s5_step_kernel(u_ref, z_ref, x0r_ref, x0i_ref, wb_ref, cw_ref, ar_ref, ai_ref, d_ref,
                    wglu_ref, bglu_ref, o_ref, sr_ref, si_ref, y_scr):
    for blk in range(S5_BLOCKS):
        cols = slice(blk * LANES, (blk + 1) * LANES)
        ub = u_ref[:, cols].astype(BF16)
        acc = d_ref[:, cols] * u_ref[:, cols]
        for j in range(blk * PAIRS_PER_BLOCK, (blk + 1) * PAIRS_PER_BLOCK):
            pc = slice(j * LANES, (j + 1) * LANES)
            bu = jnp.dot(ub, wb_ref[j], preferred_element_type=F32)
            ar = ar_ref[j:j + 1, :]
            ai = ai_ref[j:j + 1, :]
            x0r = x0r_ref[:, pc]
            x0i = x0i_ref[:, pc]
            nr = ar * x0r - ai * x0i + bu[:, :LANES]
            ni = ar * x0i + ai * x0r + bu[:, LANES:]
            sr_ref[:, pc] = nr
            si_ref[:, pc] = ni
            x = jnp.concatenate([nr, ni], axis=1).astype(BF16)
            acc = acc + jnp.dot(x, cw_ref[j], preferred_element_type=F32)
        y_scr[:, cols] = acc
    o_ref[...] = _s5_output_gate(y_scr[...], z_ref[...], wglu_ref, bglu_ref).astype(BF16)


def _s5_sample(proj, x0r, x0i, wb, cw, ar, ai, d, wglu, bglu):
    nb = proj.shape[0]
    ub_col = (4 * RET_WIDTH) // S5_WIDTH
    full = lambda a: pl.BlockSpec(a.shape, lambda i: (0,) * a.ndim)
    return pl.pallas_call(
        _s5_step_kernel,
        grid=(1,),
        in_specs=[
            pl.BlockSpec((nb, S5_WIDTH), lambda i: (0, ub_col)),
            pl.BlockSpec((nb, S5_WIDTH), lambda i: (0, ub_col + 1)),
            full(x0r), full(x0i), full(wb), full(cw), full(ar), full(ai), full(d), full(wglu), full(bglu),
        ],
        out_specs=[
            pl.BlockSpec((nb, S5_WIDTH), lambda i: (0, 0)),
            full(x0r), full(x0i),
        ],
        out_shape=[
            jax.ShapeDtypeStruct((nb, S5_WIDTH), BF16),
            jax.ShapeDtypeStruct(x0r.shape, F32),
            jax.ShapeDtypeStruct(x0i.shape, F32),
        ],
        scratch_shapes=[pltpu.VMEM((nb, S5_WIDTH), F32)],
        compiler_params=_params("arbitrary"),
        name="s5_sample",
    )(proj, proj, x0r, x0i, wb, cw, ar, ai, d, wglu, bglu)


def _merge_kernel(ya_ref, yb_ref, ga_ref, gb_ref, x_ref, wpa_ref, wpb_ref, wout_ref, gpost_ref, o_ref):
    ya = jnp.dot(ya_ref[...].astype(BF16), wpa_ref[...], preferred_element_type=F32)
    yb = jnp.dot(yb_ref[...].astype(BF16), wpb_ref[...], preferred_element_type=F32)
    merged = jax.nn.sigmoid(ga_ref[...]) * ya + jax.nn.sigmoid(gb_ref[...]) * yb
    out = jnp.dot(merged.astype(BF16), wout_ref[...], preferred_element_type=F32)
    ms = jnp.mean(out * out, axis=-1, keepdims=True)
    o_ref[...] = x_ref[...] + out * lax.rsqrt(ms + EPS) * gpost_ref[...]


def _merge(ya, yb, proj, x2d, wpa, wpb, wout, gpost, *, tm):
    m = x2d.shape[0]
    ga_col = (4 * RET_WIDTH + 2 * S5_WIDTH) // D_MODEL
    return pl.pallas_call(
        _merge_kernel,
        grid=(m // tm,),
        in_specs=[
            pl.BlockSpec((tm, RET_WIDTH), lambda i: (i, 0)),
            pl.BlockSpec((tm, S5_WIDTH), lambda i: (i, 0)),
            pl.BlockSpec((tm, D_MODEL), lambda i: (i, ga_col)),
            pl.BlockSpec((tm, D_MODEL), lambda i: (i, ga_col + 1)),
            pl.BlockSpec((tm, D_MODEL), lambda i: (i, 0)),
            _resident(wpa.shape), _resident(wpb.shape), _resident(wout.shape), _resident(gpost.shape),
        ],
        out_specs=pl.BlockSpec((tm, D_MODEL), lambda i: (i, 0)),
        out_shape=jax.ShapeDtypeStruct((m, D_MODEL), F32),
        compiler_params=_params("arbitrary"),
        name="merge_out",
    )(ya, yb, proj, proj, x2d, wpa, wpb, wout, gpost)


def _rope_tables(pos):
    half = HEAD_DIM // 2
    inv = ROPE_BASE ** (-jnp.arange(half, dtype=F32) / half)
    ang = pos.astype(F32)[:, None] * inv[None, :]
    cos, sin = jnp.cos(ang), jnp.sin(ang)
    return jnp.concatenate([cos, cos], axis=-1), jnp.concatenate([-sin, sin], axis=-1)


def _s5_discretize(lam_re, lam_im, log_dt, b_re, b_im):
    dt = jnp.exp(log_dt)[:, None]
    mag = jnp.exp(lam_re * dt)
    abar_r = mag * jnp.cos(lam_im * dt)
    abar_i = mag * jnp.sin(lam_im * dt)
    nr, ni = abar_r - 1.0, abar_i
    den = lam_re * lam_re + lam_im * lam_im
    coef_r = (nr * lam_re + ni * lam_im) / den
    coef_i = (ni * lam_re - nr * lam_im) / den
    bbar_r = coef_r[:, :, None] * b_re - coef_i[:, :, None] * b_im
    bbar_i = coef_r[:, :, None] * b_im + coef_i[:, :, None] * b_re
    return abar_r, abar_i, bbar_r, bbar_i


def _s5_pair_weights(bbar_r, bbar_i, c_re, c_im):
    slot = jax.nn.one_hot(jnp.arange(S5_PAIRS) % PAIRS_PER_BLOCK, PAIRS_PER_BLOCK, dtype=F32)
    eye = jnp.eye(2, dtype=F32)

    def in_w(b):
        b4 = b.reshape(S5_PAIRS, 2, S5_P, S5_GROUP)
        return jnp.einsum("jgpm,jq,gh->jqgmhp", b4, slot, eye).reshape(S5_PAIRS, LANES, LANES)

    def out_w(c):
        c4 = c.reshape(S5_PAIRS, 2, S5_GROUP, S5_P)
        return jnp.einsum("jgnp,jq,gh->jgpqhn", c4, slot, eye).reshape(S5_PAIRS, LANES, LANES)

    wb = jnp.concatenate([in_w(bbar_r), in_w(bbar_i)], axis=2).astype(BF16)
    cw = jnp.concatenate([out_w(c_re), out_w(-c_im)], axis=1).astype(BF16)
    return wb, cw


def kernel(x_prompt, x_sample, state_ret, state_s5_re, state_s5_im, g_pre, w_in, w_pa, w_pb, w_out, g_post,
           s5_lam_re, s5_lam_im, s5_log_dt, s5_b_re, s5_b_im, s5_c_re, s5_c_im, s5_d, s5_w_glu, s5_b_glu):
    assert w_in.shape[0] == 1, "single trunk layer"
    bp, lp, _ = x_prompt.shape
    bs, ls, _ = x_sample.shape
    assert ls == 1 and lp % RET_CHUNK == 0

    w_in_b = w_in[0].astype(BF16)
    w_pa_b = w_pa[0].astype(BF16)
    w_pb_b = w_pb[0].astype(BF16)
    w_out_b = w_out[0].astype(BF16)
    w_glu_b = s5_w_glu[0].astype(BF16)
    b_glu = s5_b_glu[0].reshape(1, S5_WIDTH)
    gpre = g_pre[0].reshape(1, D_MODEL)
    gpost = g_post[0].reshape(1, D_MODEL)

    abar_r, abar_i, bbar_r, bbar_i = _s5_discretize(
        s5_lam_re[0], s5_lam_im[0], s5_log_dt[0], s5_b_re[0], s5_b_im[0])
    wb, cw = _s5_pair_weights(bbar_r, bbar_i, s5_c_re[0], s5_c_im[0])
    ar = abar_r.reshape(S5_PAIRS, LANES)
    ai = abar_i.reshape(S5_PAIRS, LANES)
    d_skip = s5_d[0].reshape(1, S5_WIDTH)

    xp = x_prompt.reshape(bp * lp, D_MODEL)
    proj_p = _inproj(xp, gpre, w_in_b, tm=1024, tn=512)
    cos_p, sin_p = _rope_tables(jnp.arange(lp, dtype=jnp.int32))
    ya_p, ret_p = _retention_prompt(proj_p, cos_p, sin_p, batch=bp, seq=lp, tm=256)
    yb_p, s5r_p, s5i_p = _s5_prompt(proj_p, wb, cw, ar, ai, d_skip, w_glu_b, b_glu, batch=bp, seq=lp, tm=256)
    y_p = _merge(ya_p, yb_p, proj_p, xp, w_pa_b, w_pb_b, w_out_b, gpost, tm=256)

    xs = x_sample.reshape(bs, D_MODEL)
    proj_s = _inproj(xs, gpre, w_in_b, tm=bs, tn=512)
    cos_s, sin_s = _rope_tables(PAST_LEN + jnp.arange(1, dtype=jnp.int32))
    ya_s, ret_s = _retention_sample(proj_s, cos_s, sin_s, state_ret[0], bb=8)
    x0r = state_s5_re[0].reshape(bs, S5_GROUPS * S5_P)
    x0i = state_s5_im[0].reshape(bs, S5_GROUPS * S5_P)
    yb_s, s5r_s, s5i_s = _s5_sample(proj_s, x0r, x0i, wb, cw, ar, ai, d_skip, w_glu_b, b_glu)
    y_s = _merge(ya_s, yb_s, proj_s, xs, w_pa_b, w_pb_b, w_out_b, gpost, tm=bs)

    state_shape = (1, -1, S5_GROUPS, S5_P)
    return (y_p.reshape(bp, lp, D_MODEL), y_s.reshape(bs, 1, D_MODEL),
            ret_p[None], s5r_p.reshape(state_shape), s5i_p.reshape(state_shape),
            ret_s[None], s5r_s.reshape(state_shape), s5i_s.reshape(state_shape))
```

```python
import functools
import math

import jax
import jax.numpy as jnp
import numpy as np
from jax import lax
from jax.experimental import pallas as pl
from jax.experimental.pallas import tpu as pltpu

F32 = jnp.float32
BF16 = jnp.bfloat16

D_MODEL = 2048
RET_WIDTH = 1024
RET_HEADS = 8
HEAD_DIM = RET_WIDTH // RET_HEADS
RET_CHUNK = 128
ROPE_BASE = 10000.0
S5_WIDTH = 1024
S5_GROUP = 16
S5_GROUPS = S5_WIDTH // S5_GROUP
S5_P = 64
PAST_LEN = 16384
EPS = 1e-6
IN_COLS = 4 * RET_WIDTH + 2 * S5_WIDTH + 2 * D_MODEL

LANES = 128
S5_PAIRS = S5_GROUPS // 2
PAIRS_PER_BLOCK = LANES // (2 * S5_GROUP)
S5_BLOCKS = S5_WIDTH // LANES
S5_PITCH = 40
VMEM_LIMIT = 56 * 1024 * 1024

_LOG_GAMMA = [float(np.log(np.float32(1.0) - np.float32(2.0) ** np.float32(-5.0 - h)))
              for h in range(RET_HEADS)]


def _params(*sem):
    return pltpu.CompilerParams(dimension_semantics=sem, vmem_limit_bytes=VMEM_LIMIT)


def _resident(shape):
    nd = len(shape)
    return pl.BlockSpec(shape, lambda *_: (0,) * nd, pipeline_mode=pl.Buffered(1))


def _rope(x, cos, sin_signed):
    return x * cos + pltpu.roll(x, HEAD_DIM // 2, 1) * sin_signed


def _inproj_kernel(x_ref, g_ref, w_ref, cos_ref, sin_ref, o_ref, h_ref, *, slab, tn):
    j = pl.program_id(1)

    @pl.when(j == 0)
    def _():
        g = g_ref[...]

        def body(r, _):
            rows = pl.ds(pl.multiple_of(r * slab, slab), slab)
            x = x_ref[rows, :]
            ms = jnp.mean(x * x, axis=-1, keepdims=True)
            h_ref[rows, :] = (x * lax.rsqrt(ms + EPS) * g).astype(BF16)
            return 0

        lax.fori_loop(0, x_ref.shape[0] // slab, body, 0)

    acc = jnp.dot(h_ref[...], w_ref[...], preferred_element_type=F32)
    k_tile0 = RET_WIDTH // tn
    v_tile0 = 2 * RET_WIDTH // tn

    @pl.when(j < v_tile0)
    def _():
        scale = jnp.where(j >= k_tile0, HEAD_DIM ** -0.5, 1.0).astype(F32)
        cos = cos_ref[...]
        sin = sin_ref[...]
        for hh in range(tn // HEAD_DIM):
            cols = slice(hh * HEAD_DIM, (hh + 1) * HEAD_DIM)
            o_ref[:, cols] = (_rope(acc[:, cols], cos, sin) * scale).astype(BF16)

    @pl.when(j >= v_tile0)
    def _():
        o_ref[...] = acc.astype(BF16)


def _inproj(x2d, g_pre, w_in_bf16, cos, sin, *, tm, tn):
    m, d = x2d.shape
    n = w_in_bf16.shape[1]
    pos_tiles = cos.shape[0] // tm
    return pl.pallas_call(
        functools.partial(_inproj_kernel, slab=min(tm, 128), tn=tn),
        grid=(m // tm, n // tn),
        in_specs=[
            pl.BlockSpec((tm, d), lambda i, j: (i, 0)),
            pl.BlockSpec((1, d), lambda i, j: (0, 0)),
            pl.BlockSpec((d, tn), lambda i, j: (0, j)),
            pl.BlockSpec((tm, HEAD_DIM), lambda i, j: (i % pos_tiles, 0)),
            pl.BlockSpec((tm, HEAD_DIM), lambda i, j: (i % pos_tiles, 0)),
        ],
        out_specs=pl.BlockSpec((tm, tn), lambda i, j: (i, j)),
        out_shape=jax.ShapeDtypeStruct((m, n), BF16),
        scratch_shapes=[pltpu.VMEM((tm, d), BF16)],
        compiler_params=_params("arbitrary", "arbitrary"),
        name="inproj",
    )(x2d, g_pre, w_in_bf16, cos, sin)


def _group_norm(o):
    mu = jnp.mean(o, axis=-1, keepdims=True)
    d = o - mu
    var = jnp.mean(d * d, axis=-1, keepdims=True)
    return d * lax.rsqrt(var + EPS)


def _retention_kernel(q_ref, k_ref, v_ref, z_ref, o_ref, s_ref, *, n_chunks):
    c = RET_CHUNK

    @pl.when(pl.program_id(1) == 0)
    def _():
        s_ref[...] = jnp.zeros_like(s_ref)

    row = lax.broadcasted_iota(jnp.int32, (c, c), 0).astype(F32)
    col = lax.broadcasted_iota(jnp.int32, (c, c), 1).astype(F32)
    diff = row - col
    for h in range(RET_HEADS):
        lg = _LOG_GAMMA[h]
        mask = jnp.where(diff >= 0, jnp.exp(jnp.maximum(diff, 0.0) * lg), 0.0)
        q_decay = jnp.exp((row + 1.0) * lg)
        k_decay = jnp.exp((c - 1.0 - row) * lg)
        chunk_decay = math.exp(c * lg)
        cols = slice(h * HEAD_DIM, (h + 1) * HEAD_DIM)
        for ci in range(n_chunks):
            rows = slice(ci * c, (ci + 1) * c)
            q = q_ref[rows, cols]
            k = k_ref[rows, cols]
            v = v_ref[rows, cols]
            s0 = s_ref[0, h]
            scores = lax.dot_general(q, k, (((1,), (1,)), ((), ())),
                                     preferred_element_type=F32) * mask
            inner = jnp.dot(scores.astype(BF16), v, preferred_element_type=F32)
            cross = jnp.dot(q, s0.astype(BF16), preferred_element_type=F32) * q_decay
            kd = (k.astype(F32) * k_decay).astype(BF16)
            s_ref[0, h] = chunk_decay * s0 + lax.dot_general(
                kd, v, (((0,), (0,)), ((), ())), preferred_element_type=F32)
            o = _group_norm(inner + cross)
            o_ref[rows, cols] = (o * jax.nn.silu(z_ref[rows, cols].astype(F32))).astype(BF16)


def _retention_prompt(proj, *, batch, seq, tm):
    nt = seq // tm
    row_block = lambda b, i: b * nt + i
    col_spec = lambda cb: pl.BlockSpec((tm, RET_WIDTH), lambda b, i: (row_block(b, i), cb))
    return pl.pallas_call(
        functools.partial(_retention_kernel, n_chunks=tm // RET_CHUNK),
        grid=(batch, nt),
        in_specs=[col_spec(0), col_spec(1), col_spec(2), col_spec(3)],
        out_specs=[
            pl.BlockSpec((tm, RET_WIDTH), lambda b, i: (row_block(b, i), 0)),
            pl.BlockSpec((1, RET_HEADS, HEAD_DIM, HEAD_DIM), lambda b, i: (b, 0, 0, 0)),
        ],
        out_shape=[
            jax.ShapeDtypeStruct((batch * seq, RET_WIDTH), BF16),
            jax.ShapeDtypeStruct((batch, RET_HEADS, HEAD_DIM, HEAD_DIM), F32),
        ],
        compiler_params=_params("arbitrary", "arbitrary"),
        name="retention_prompt",
    )(proj, proj, proj, proj)


def _retention_step_kernel(q_ref, k_ref, v_ref, z_ref, s_ref, o_ref, sn_ref, acc_ref, *, bb):
    pad = jnp.zeros((HEAD_DIM - bb, HEAD_DIM), F32)
    for h in range(RET_HEADS):
        gamma = math.exp(_LOG_GAMMA[h])
        cols = slice(h * HEAD_DIM, (h + 1) * HEAD_DIM)
        q = q_ref[:, cols].astype(F32)
        k = k_ref[:, cols].astype(F32)
        v = v_ref[:, cols].astype(F32)
        qt = jnp.concatenate([q, pad], axis=0).T
        kt = jnp.concatenate([k, pad], axis=0).T
        for b in range(bb):
            s_new = gamma * s_ref[b, h] + kt[:, b:b + 1] * v[b:b + 1, :]
            sn_ref[b, h] = s_new
            acc_ref[b:b + 1, cols] = jnp.sum(qt[:, b:b + 1] * s_new, axis=0, keepdims=True)
    for h in range(RET_HEADS):
        cols = slice(h * HEAD_DIM, (h + 1) * HEAD_DIM)
        o_ref[:, cols] = _group_norm(acc_ref[:, cols]) * jax.nn.silu(z_ref[:, cols].astype(F32))


def _retention_sample(proj, state, *, bb):
    nb = state.shape[0]
    col_spec = lambda cb: pl.BlockSpec((bb, RET_WIDTH), lambda i: (i, cb))
    state_spec = pl.BlockSpec((bb, RET_HEADS, HEAD_DIM, HEAD_DIM), lambda i: (i, 0, 0, 0))
    return pl.pallas_call(
        functools.partial(_retention_step_kernel, bb=bb),
        grid=(nb // bb,),
        in_specs=[col_spec(0), col_spec(1), col_spec(2), col_spec(3), state_spec],
        out_specs=[pl.BlockSpec((bb, RET_WIDTH), lambda i: (i, 0)), state_spec],
        out_shape=[
            jax.ShapeDtypeStruct((nb, RET_WIDTH), F32),
            jax.ShapeDtypeStruct(state.shape, F32),
        ],
        scratch_shapes=[pltpu.VMEM((bb, RET_WIDTH), F32)],
        compiler_params=_params("arbitrary"),
        name="retention_sample",
    )(proj, proj, proj, proj, state)


def _s5_output_gate(y, z, wglu_ref, bglu_ref):
    y = jax.nn.gelu(y)
    g = jnp.dot(y.astype(BF16), wglu_ref[...], preferred_element_type=F32) + bglu_ref[...]
    return y * jax.nn.sigmoid(g) * jax.nn.silu(z.astype(F32))


def _s5_prompt_kernel(u_ref, z_ref, wb_ref, cw_ref, ar_ref, ai_ref, d_ref, wglu_ref, bglu_ref,
                      o_ref, sr_ref, si_ref, xr_scr, xi_scr, y_scr, *, tm):
    @pl.when(pl.program_id(1) == 0)
    def _():
        sr_ref[...] = jnp.zeros_like(sr_ref)
        si_ref[...] = jnp.zeros_like(si_ref)

    for blk in range(S5_BLOCKS):
        ub = u_ref[:, blk * LANES:(blk + 1) * LANES]
        for j in range(blk * PAIRS_PER_BLOCK, (blk + 1) * PAIRS_PER_BLOCK):
            bu = jnp.dot(ub, wb_ref[j], preferred_element_type=F32)
            xr_scr[pl.ds(j, tm, stride=S5_PITCH), :] = bu[:, :LANES]
            xi_scr[pl.ds(j, tm, stride=S5_PITCH), :] = bu[:, LANES:]

    ar = ar_ref[...]
    ai = ai_ref[...]

    def step(t, carry):
        xr, xi = carry
        rows = pl.ds(pl.multiple_of(t * S5_PITCH, 8), S5_PAIRS)
        nr = ar * xr - ai * xi + xr_scr[rows, :]
        ni = ar * xi + ai * xr + xi_scr[rows, :]
        xr_scr[rows, :] = nr
        xi_scr[rows, :] = ni
        return nr, ni

    xr, xi = lax.fori_loop(0, tm, step, (sr_ref[0], si_ref[0]), unroll=4)
    sr_ref[0] = xr
    si_ref[0] = xi

    for blk in range(S5_BLOCKS):
        cols = slice(blk * LANES, (blk + 1) * LANES)
        acc = d_ref[:, cols] * u_ref[:, cols].astype(F32)
        for j in range(blk * PAIRS_PER_BLOCK, (blk + 1) * PAIRS_PER_BLOCK):
            x = jnp.concatenate([xr_scr[pl.ds(j, tm, stride=S5_PITCH), :],
                                 xi_scr[pl.ds(j, tm, stride=S5_PITCH), :]], axis=1)
            acc = acc + jnp.dot(x.astype(BF16), cw_ref[j], preferred_element_type=F32)
        y_scr[:, cols] = acc
    o_ref[...] = _s5_output_gate(y_scr[...], z_ref[...], wglu_ref, bglu_ref).astype(BF16)


def _s5_prompt(proj, wb, cw, ar, ai, d, wglu, bglu, *, batch, seq, tm):
    nt = seq // tm
    row_block = lambda b, i: b * nt + i
    ub_col = (4 * RET_WIDTH) // S5_WIDTH
    state_spec = pl.BlockSpec((1, S5_PAIRS, LANES), lambda b, i: (b, 0, 0))
    return pl.pallas_call(
        functools.partial(_s5_prompt_kernel, tm=tm),
        grid=(batch, nt),
        in_specs=[
            pl.BlockSpec((tm, S5_WIDTH), lambda b, i: (row_block(b, i), ub_col)),
            pl.BlockSpec((tm, S5_WIDTH), lambda b, i: (row_block(b, i), ub_col + 1)),
            _resident(wb.shape), _resident(cw.shape), _resident(ar.shape), _resident(ai.shape),
            _resident(d.shape), _resident(wglu.shape), _resident(bglu.shape),
        ],
        out_specs=[
            pl.BlockSpec((tm, S5_WIDTH), lambda b, i: (row_block(b, i), 0)),
            state_spec, state_spec,
        ],
        out_shape=[
            jax.ShapeDtypeStruct((batch * seq, S5_WIDTH), BF16),
            jax.ShapeDtypeStruct((batch, S5_PAIRS, LANES), F32),
            jax.ShapeDtypeStruct((batch, S5_PAIRS, LANES), F32),
        ],
        scratch_shapes=[
            pltpu.VMEM((tm * S5_PITCH, LANES), F32),
            pltpu.VMEM((tm * S5_PITCH, LANES), F32),
            pltpu.VMEM((tm, S5_WIDTH), F32),
        ],
        compiler_params=_params("arbitrary", "arbitrary"),
        name="s5_prompt",
    )(proj, proj, wb, cw, ar, ai, d, wglu, bglu)


def _s5_step_kernel(u_ref, z_ref, x0r_ref, x0i_ref, wb_ref, cw_ref, ar_ref, ai_ref, d_ref,
                    wglu_ref, bglu_ref, o_ref, sr_ref, si_ref, y_scr):
    for blk in range(S5_BLOCKS):
        cols = slice(blk * LANES, (blk + 1) * LANES)
        ub = u_ref[:, cols]
        acc = d_ref[:, cols] * ub.astype(F32)
        for j in range(blk * PAIRS_PER_BLOCK, (blk + 1) * PAIRS_PER_BLOCK):
            pc = slice(j * LANES, (j + 1) * LANES)
            bu = jnp.dot(ub, wb_ref[j], preferred_element_type=F32)
            ar = ar_ref[j:j + 1, :]
            ai = ai_ref[j:j + 1, :]
            x0r = x0r_ref[:, pc]
            x0i = x0i_ref[:, pc]
            nr = ar * x0r - ai * x0i + bu[:, :LANES]
            ni = ar * x0i + ai * x0r + bu[:, LANES:]
            sr_ref[:, pc] = nr
            si_ref[:, pc] = ni
            x = jnp.concatenate([nr, ni], axis=1).astype(BF16)
            acc = acc + jnp.dot(x, cw_ref[j], preferred_element_type=F32)
        y_scr[:, cols] = acc
    o_ref[...] = _s5_output_gate(y_scr[...], z_ref[...], wglu_ref, bglu_ref).astype(BF16)


def _s5_sample(proj, x0r, x0i, wb, cw, ar, ai, d, wglu, bglu):
    nb = proj.shape[0]
    ub_col = (4 * RET_WIDTH) // S5_WIDTH
    full = lambda a: pl.BlockSpec(a.shape, lambda i: (0,) * a.ndim)
    return pl.pallas_call(
        _s5_step_kernel,
        grid=(1,),
        in_specs=[
            pl.BlockSpec((nb, S5_WIDTH), lambda i: (0, ub_col)),
            pl.BlockSpec((nb, S5_WIDTH), lambda i: (0, ub_col + 1)),
            full(x0r), full(x0i), full(wb), full(cw), full(ar), full(ai), full(d), full(wglu), full(bglu),
        ],
        out_specs=[
            pl.BlockSpec((nb, S5_WIDTH), lambda i: (0, 0)),
            full(x0r), full(x0i),
        ],
        out_shape=[
            jax.ShapeDtypeStruct((nb, S5_WIDTH), BF16),
            jax.ShapeDtypeStruct(x0r.shape, F32),
            jax.ShapeDtypeStruct(x0i.shape, F32),
        ],
        scratch_shapes=[pltpu.VMEM((nb, S5_WIDTH), F32)],
        compiler_params=_params("arbitrary"),
        name="s5_sample",
    )(proj, proj, x0r, x0i, wb, cw, ar, ai, d, wglu, bglu)


def _merge_kernel(ya_ref, yb_ref, ga_ref, gb_ref, x_ref, wpa_ref, wpb_ref, wout_ref, gpost_ref, o_ref):
    ya = jnp.dot(ya_ref[...].astype(BF16), wpa_ref[...], preferred_element_type=F32)
    yb = jnp.dot(yb_ref[...].astype(BF16), wpb_ref[...], preferred_element_type=F32)
    merged = (jax.nn.sigmoid(ga_ref[...].astype(F32)) * ya
              + jax.nn.sigmoid(gb_ref[...].astype(F32)) * yb)
    out = jnp.dot(merged.astype(BF16), wout_ref[...], preferred_element_type=F32)
    ms = jnp.mean(out * out, axis=-1, keepdims=True)
    o_ref[...] = x_ref[...] + out * lax.rsqrt(ms + EPS) * gpost_ref[...]


def _merge(ya, yb, proj, x2d, wpa, wpb, wout, gpost, *, tm):
    m = x2d.shape[0]
    ga_col = (4 * RET_WIDTH + 2 * S5_WIDTH) // D_MODEL
    return pl.pallas_call(
        _merge_kernel,
        grid=(m // tm,),
        in_specs=[
            pl.BlockSpec((tm, RET_WIDTH), lambda i: (i, 0)),
            pl.BlockSpec((tm, S5_WIDTH), lambda i: (i, 0)),
            pl.BlockSpec((tm, D_MODEL), lambda i: (i, ga_col)),
            pl.BlockSpec((tm, D_MODEL), lambda i: (i, ga_col + 1)),
            pl.BlockSpec((tm, D_MODEL), lambda i: (i, 0)),
            _resident(wpa.shape), _resident(wpb.shape), _resident(wout.shape), _resident(gpost.shape),
        ],
        out_specs=pl.BlockSpec((tm, D_MODEL), lambda i: (i, 0)),
        out_shape=jax.ShapeDtypeStruct((m, D_MODEL), F32),
        compiler_params=_params("arbitrary"),
        name="merge_out",
    )(ya, yb, proj, proj, x2d, wpa, wpb, wout, gpost)


def _rope_tables(pos):
    half = HEAD_DIM // 2
    inv = ROPE_BASE ** (-jnp.arange(half, dtype=F32) / half)
    ang = pos.astype(F32)[:, None] * inv[None, :]
    cos, sin = jnp.cos(ang), jnp.sin(ang)
    return jnp.concatenate([cos, cos], axis=-1), jnp.concatenate([-sin, sin], axis=-1)


def _s5_discretize(lam_re, lam_im, log_dt, b_re, b_im):
    dt = jnp.exp(log_dt)[:, None]
    mag = jnp.exp(lam_re * dt)
    abar_r = mag * jnp.cos(lam_im * dt)
    abar_i = mag * jnp.sin(lam_im * dt)
    nr, ni = abar_r - 1.0, abar_i
    den = lam_re * lam_re + lam_im * lam_im
    coef_r = (nr * lam_re + ni * lam_im) / den
    coef_i = (ni * lam_re - nr * lam_im) / den
    bbar_r = coef_r[:, :, None] * b_re - coef_i[:, :, None] * b_im
    bbar_i = coef_r[:, :, None] * b_im + coef_i[:, :, None] * b_re
    return abar_r, abar_i, bbar_r, bbar_i


def _s5_pair_weights(bbar_r, bbar_i, c_re, c_im):
    slot = jax.nn.one_hot(jnp.arange(S5_PAIRS) % PAIRS_PER_BLOCK, PAIRS_PER_BLOCK, dtype=F32)
    eye = jnp.eye(2, dtype=F32)

    def in_w(b):
        b4 = b.reshape(S5_PAIRS, 2, S5_P, S5_GROUP)
        return jnp.einsum("jgpm,jq,gh->jqgmhp", b4, slot, eye).reshape(S5_PAIRS, LANES, LANES)

    def out_w(c):
        c4 = c.reshape(S5_PAIRS, 2, S5_GROUP, S5_P)
        return jnp.einsum("jgnp,jq,gh->jgpqhn", c4, slot, eye).reshape(S5_PAIRS, LANES, LANES)

    wb = jnp.concatenate([in_w(bbar_r), in_w(bbar_i)], axis=2).astype(BF16)
    cw = jnp.concatenate([out_w(c_re), out_w(-c_im)], axis=1).astype(BF16)
    return wb, cw


def kernel(x_prompt, x_sample, state_ret, state_s5_re, state_s5_im, g_pre, w_in, w_pa, w_pb, w_out, g_post,
           s5_lam_re, s5_lam_im, s5_log_dt, s5_b_re, s5_b_im, s5_c_re, s5_c_im, s5_d, s5_w_glu, s5_b_glu):
    assert w_in.shape[0] == 1, "single trunk layer"
    bp, lp, _ = x_prompt.shape
    bs, ls, _ = x_sample.shape
    assert ls == 1 and lp % RET_CHUNK == 0

    w_in_b = w_in[0].astype(BF16)
    w_pa_b = w_pa[0].astype(BF16)
    w_pb_b = w_pb[0].astype(BF16)
    w_out_b = w_out[0].astype(BF16)
    w_glu_b = s5_w_glu[0].astype(BF16)
    b_glu = s5_b_glu[0].reshape(1, S5_WIDTH)
    gpre = g_pre[0].reshape(1, D_MODEL)
    gpost = g_post[0].reshape(1, D_MODEL)

    abar_r, abar_i, bbar_r, bbar_i = _s5_discretize(
        s5_lam_re[0], s5_lam_im[0], s5_log_dt[0], s5_b_re[0], s5_b_im[0])
    wb, cw = _s5_pair_weights(bbar_r, bbar_i, s5_c_re[0], s5_c_im[0])
    ar = abar_r.reshape(S5_PAIRS, LANES)
    ai = abar_i.reshape(S5_PAIRS, LANES)
    d_skip = s5_d[0].reshape(1, S5_WIDTH)

    xp = x_prompt.reshape(bp * lp, D_MODEL)
    cos_p, sin_p = _rope_tables(jnp.arange(lp, dtype=jnp.int32))
    proj_p = _inproj(xp, gpre, w_in_b, cos_p, sin_p, tm=1024, tn=512)
    ya_p, ret_p = _retention_prompt(proj_p, batch=bp, seq=lp, tm=256)
    yb_p, s5r_p, s5i_p = _s5_prompt(proj_p, wb, cw, ar, ai, d_skip, w_glu_b, b_glu, batch=bp, seq=lp, tm=256)
    y_p = _merge(ya_p, yb_p, proj_p, xp, w_pa_b, w_pb_b, w_out_b, gpost, tm=256)

    xs = x_sample.reshape(bs, D_MODEL)
    cos_s, sin_s = _rope_tables(jnp.full((bs,), PAST_LEN, jnp.int32))
    proj_s = _inproj(xs, gpre, w_in_b, cos_s, sin_s, tm=bs, tn=512)
    ya_s, ret_s = _retention_sample(proj_s, state_ret[0], bb=16)
    x0r = state_s5_re[0].reshape(bs, S5_GROUPS * S5_P)
    x0i = state_s5_im[0].reshape(bs, S5_GROUPS * S5_P)
    yb_s, s5r_s, s5i_s = _s5_sample(proj_s, x0r, x0i, wb, cw, ar, ai, d_skip, w_glu_b, b_glu)
    y_s = _merge(ya_s, yb_s, proj_s, xs, w_pa_b, w_pb_b, w_out_b, gpost, tm=bs)

    state_shape = (1, -1, S5_GROUPS, S5_P)
    return (y_p.reshape(bp, lp, D_MODEL), y_s.reshape(bs, 1, D_MODEL),
            ret_p[None], s5r_p.reshape(state_shape), s5i_p.reshape(state_shape),
            ret_s[None], s5r_s.reshape(state_shape), s5i_s.reshape(state_shape))
```

```python
import functools
import math

import jax
import jax.numpy as jnp
import numpy as np
from jax import lax
from jax.experimental import pallas as pl
from jax.experimental.pallas import tpu as pltpu

F32 = jnp.float32
BF16 = jnp.bfloat16

D_MODEL = 2048
RET_WIDTH = 1024
RET_HEADS = 8
HEAD_DIM = RET_WIDTH // RET_HEADS
RET_CHUNK = 128
ROPE_BASE = 10000.0
S5_WIDTH = 1024
S5_GROUP = 16
S5_GROUPS = S5_WIDTH // S5_GROUP
S5_P = 64
PAST_LEN = 16384
EPS = 1e-6
IN_COLS = 4 * RET_WIDTH + 2 * S5_WIDTH + 2 * D_MODEL

LANES = 128
MXU_DIM = 256
S5_PAIRS = S5_GROUPS // 2
PAIRS_PER_BLOCK = LANES // (2 * S5_GROUP)
S5_BLOCKS = S5_WIDTH // LANES
S5_PITCH = 40
VMEM_LIMIT = 56 * 1024 * 1024

_LOG_GAMMA = [float(np.log(np.float32(1.0) - np.float32(2.0) ** np.float32(-5.0 - h)))
              for h in range(RET_HEADS)]


def _params(*sem):
    return pltpu.CompilerParams(dimension_semantics=sem, vmem_limit_bytes=VMEM_LIMIT)


def _resident(shape):
    nd = len(shape)
    return pl.BlockSpec(shape, lambda *_: (0,) * nd, pipeline_mode=pl.Buffered(1))


def _rope(x, cos, sin_signed):
    return x * cos + pltpu.roll(x, HEAD_DIM // 2, 1) * sin_signed


def _inproj_kernel(x_ref, g_ref, w_ref, cos_ref, sin_ref, o_ref, h_ref, *, slab, tn):
    j = pl.program_id(1)

    @pl.when(j == 0)
    def _():
        g = g_ref[...]

        def body(r, _):
            rows = pl.ds(pl.multiple_of(r * slab, slab), slab)
            x = x_ref[rows, :]
            ms = jnp.mean(x * x, axis=-1, keepdims=True)
            h_ref[rows, :] = (x * lax.rsqrt(ms + EPS) * g).astype(BF16)
            return 0

        lax.fori_loop(0, x_ref.shape[0] // slab, body, 0)

    k_tile0 = RET_WIDTH // tn
    v_tile0 = 2 * RET_WIDTH // tn

    def slabs(epilogue):
        for s in range(tn // MXU_DIM):
            cols = slice(s * MXU_DIM, (s + 1) * MXU_DIM)
            acc = jnp.dot(h_ref[...], w_ref[:, cols], preferred_element_type=F32)
            o_ref[:, cols] = epilogue(acc).astype(BF16)

    @pl.when(j < v_tile0)
    def _():
        scale = jnp.where(j >= k_tile0, HEAD_DIM ** -0.5, 1.0).astype(F32)
        cos = cos_ref[...] * scale
        sin = sin_ref[...] * scale
        slabs(lambda acc: jnp.concatenate(
            [_rope(acc[:, hh * HEAD_DIM:(hh + 1) * HEAD_DIM], cos, sin)
             for hh in range(MXU_DIM // HEAD_DIM)], axis=1))

    @pl.when(j >= v_tile0)
    def _():
        slabs(lambda acc: acc)


def _inproj(x2d, g_pre, w_in_bf16, cos, sin, *, tm, tn):
    m, d = x2d.shape
    n = w_in_bf16.shape[1]
    pos_tiles = cos.shape[0] // tm
    return pl.pallas_call(
        functools.partial(_inproj_kernel, slab=min(tm, 128), tn=tn),
        grid=(m // tm, n // tn),
        in_specs=[
            pl.BlockSpec((tm, d), lambda i, j: (i, 0)),
            pl.BlockSpec((1, d), lambda i, j: (0, 0)),
            pl.BlockSpec((d, tn), lambda i, j: (0, j)),
            pl.BlockSpec((tm, HEAD_DIM), lambda i, j: (i % pos_tiles, 0)),
            pl.BlockSpec((tm, HEAD_DIM), lambda i, j: (i % pos_tiles, 0)),
        ],
        out_specs=pl.BlockSpec((tm, tn), lambda i, j: (i, j)),
        out_shape=jax.ShapeDtypeStruct((m, n), BF16),
        scratch_shapes=[pltpu.VMEM((tm, d), BF16)],
        compiler_params=_params("arbitrary", "arbitrary"),
        name="inproj",
    )(x2d, g_pre, w_in_bf16, cos, sin)


def _group_norm(o):
    mu = jnp.mean(o, axis=-1, keepdims=True)
    d = o - mu
    var = jnp.mean(d * d, axis=-1, keepdims=True)
    return d * lax.rsqrt(var + EPS)


def _retention_kernel(q_ref, k_ref, v_ref, z_ref, o_ref, s_ref, *, n_chunks):
    c = RET_CHUNK

    @pl.when(pl.program_id(1) == 0)
    def _():
        s_ref[...] = jnp.zeros_like(s_ref)

    row = lax.broadcasted_iota(jnp.int32, (c, c), 0).astype(F32)
    col = lax.broadcasted_iota(jnp.int32, (c, c), 1).astype(F32)
    diff = row - col
    for h in range(RET_HEADS):
        lg = _LOG_GAMMA[h]
        mask = jnp.where(diff >= 0, jnp.exp(jnp.maximum(diff, 0.0) * lg), 0.0)
        q_decay = jnp.exp((row + 1.0) * lg)
        k_decay = jnp.exp((c - 1.0 - row) * lg)
        chunk_decay = math.exp(c * lg)
        cols = slice(h * HEAD_DIM, (h + 1) * HEAD_DIM)
        for ci in range(n_chunks):
            rows = slice(ci * c, (ci + 1) * c)
            q = q_ref[rows, cols]
            k = k_ref[rows, cols]
            v = v_ref[rows, cols]
            s0 = s_ref[0, h]
            scores = lax.dot_general(q, k, (((1,), (1,)), ((), ())),
                                     preferred_element_type=F32) * mask
            inner = jnp.dot(scores.astype(BF16), v, preferred_element_type=F32)
            cross = jnp.dot(q, s0.astype(BF16), preferred_element_type=F32) * q_decay
            kd = (k.astype(F32) * k_decay).astype(BF16)
            s_ref[0, h] = chunk_decay * s0 + lax.dot_general(
                kd, v, (((0,), (0,)), ((), ())), preferred_element_type=F32)
            o = _group_norm(inner + cross)
            o_ref[rows, cols] = (o * jax.nn.silu(z_ref[rows, cols].astype(F32))).astype(BF16)


def _retention_prompt(proj, *, batch, seq, tm):
    nt = seq // tm
    row_block = lambda b, i: b * nt + i
    col_spec = lambda cb: pl.BlockSpec((tm, RET_WIDTH), lambda b, i: (row_block(b, i), cb))
    return pl.pallas_call(
        functools.partial(_retention_kernel, n_chunks=tm // RET_CHUNK),
        grid=(batch, nt),
        in_specs=[col_spec(0), col_spec(1), col_spec(2), col_spec(3)],
        out_specs=[
            pl.BlockSpec((tm, RET_WIDTH), lambda b, i: (row_block(b, i), 0)),
            pl.BlockSpec((1, RET_HEADS, HEAD_DIM, HEAD_DIM), lambda b, i: (b, 0, 0, 0)),
        ],
        out_shape=[
            jax.ShapeDtypeStruct((batch * seq, RET_WIDTH), BF16),
            jax.ShapeDtypeStruct((batch, RET_HEADS, HEAD_DIM, HEAD_DIM), F32),
        ],
        compiler_params=_params("arbitrary", "arbitrary"),
        name="retention_prompt",
    )(proj, proj, proj, proj)


def _retention_step_kernel(q_ref, k_ref, v_ref, z_ref, s_ref, o_ref, sn_ref, acc_ref, *, bb):
    pad = jnp.zeros((HEAD_DIM - bb, HEAD_DIM), F32)
    for h in range(RET_HEADS):
        gamma = math.exp(_LOG_GAMMA[h])
        cols = slice(h * HEAD_DIM, (h + 1) * HEAD_DIM)
        q = q_ref[:, cols].astype(F32)
        k = k_ref[:, cols].astype(F32)
        v = v_ref[:, cols].astype(F32)
        qt = jnp.concatenate([q, pad], axis=0).T
        kt = jnp.concatenate([k, pad], axis=0).T
        for b in range(bb):
            s_new = gamma * s_ref[b, h] + kt[:, b:b + 1] * v[b:b + 1, :]
            sn_ref[b, h] = s_new
            acc_ref[b:b + 1, cols] = jnp.sum(qt[:, b:b + 1] * s_new, axis=0, keepdims=True)
    for h in range(RET_HEADS):
        cols = slice(h * HEAD_DIM, (h + 1) * HEAD_DIM)
        o_ref[:, cols] = _group_norm(acc_ref[:, cols]) * jax.nn.silu(z_ref[:, cols].astype(F32))


def _retention_sample(proj, state, *, bb):
    nb = state.shape[0]
    col_spec = lambda cb: pl.BlockSpec((bb, RET_WIDTH), lambda i: (i, cb))
    state_spec = pl.BlockSpec((bb, RET_HEADS, HEAD_DIM, HEAD_DIM), lambda i: (i, 0, 0, 0))
    return pl.pallas_call(
        functools.partial(_retention_step_kernel, bb=bb),
        grid=(nb // bb,),
        in_specs=[col_spec(0), col_spec(1), col_spec(2), col_spec(3), state_spec],
        out_specs=[pl.BlockSpec((bb, RET_WIDTH), lambda i: (i, 0)), state_spec],
        out_shape=[
            jax.ShapeDtypeStruct((nb, RET_WIDTH), F32),
            jax.ShapeDtypeStruct(state.shape, F32),
        ],
        scratch_shapes=[pltpu.VMEM((bb, RET_WIDTH), F32)],
        compiler_params=_params("arbitrary"),
        name="retention_sample",
    )(proj, proj, proj, proj, state)


def _s5_output_gate(y, z, wglu_ref, bglu_ref):
    y = jax.nn.gelu(y)
    g = jnp.dot(y.astype(BF16), wglu_ref[...], preferred_element_type=F32) + bglu_ref[...]
    return y * jax.nn.sigmoid(g) * jax.nn.silu(z.astype(F32))


def _s5_prompt_kernel(u_ref, z_ref, wb_ref, cw_ref, ar_ref, ai_ref, d_ref, wglu_ref, bglu_ref,
                      o_ref, sr_ref, si_ref, xr_scr, xi_scr, y_scr, *, tm):
    @pl.when(pl.program_id(1) == 0)
    def _():
        sr_ref[...] = jnp.zeros_like(sr_ref)
        si_ref[...] = jnp.zeros_like(si_ref)

    for blk in range(S5_BLOCKS):
        ub = u_ref[:, blk * LANES:(blk + 1) * LANES]
        for j in range(blk * PAIRS_PER_BLOCK, (blk + 1) * PAIRS_PER_BLOCK):
            bu = jnp.dot(ub, wb_ref[j], preferred_element_type=F32)
            xr_scr[pl.ds(j, tm, stride=S5_PITCH), :] = bu[:, :LANES]
            xi_scr[pl.ds(j, tm, stride=S5_PITCH), :] = bu[:, LANES:]

    ar = ar_ref[...]
    ai = ai_ref[...]

    def step(t, carry):
        xr, xi = carry
        rows = pl.ds(pl.multiple_of(t * S5_PITCH, 8), S5_PAIRS)
        nr = ar * xr - ai * xi + xr_scr[rows, :]
        ni = ar * xi + ai * xr + xi_scr[rows, :]
        xr_scr[rows, :] = nr
        xi_scr[rows, :] = ni
        return nr, ni

    xr, xi = lax.fori_loop(0, tm, step, (sr_ref[0], si_ref[0]), unroll=4)
    sr_ref[0] = xr
    si_ref[0] = xi

    for blk in range(S5_BLOCKS):
        cols = slice(blk * LANES, (blk + 1) * LANES)
        acc = d_ref[:, cols] * u_ref[:, cols].astype(F32)
        for j in range(blk * PAIRS_PER_BLOCK, (blk + 1) * PAIRS_PER_BLOCK):
            x = jnp.concatenate([xr_scr[pl.ds(j, tm, stride=S5_PITCH), :],
                                 xi_scr[pl.ds(j, tm, stride=S5_PITCH), :]], axis=1)
            acc = acc + jnp.dot(x.astype(BF16), cw_ref[j], preferred_element_type=F32)
        y_scr[:, cols] = acc
    o_ref[...] = _s5_output_gate(y_scr[...], z_ref[...], wglu_ref, bglu_ref).astype(BF16)


def _s5_prompt(proj, wb, cw, ar, ai, d, wglu, bglu, *, batch, seq, tm):
    nt = seq // tm
    row_block = lambda b, i: b * nt + i
    ub_col = (4 * RET_WIDTH) // S5_WIDTH
    state_spec = pl.BlockSpec((1, S5_PAIRS, LANES), lambda b, i: (b, 0, 0))
    return pl.pallas_call(
        functools.partial(_s5_prompt_kernel, tm=tm),
        grid=(batch, nt),
        in_specs=[
            pl.BlockSpec((tm, S5_WIDTH), lambda b, i: (row_block(b, i), ub_col)),
            pl.BlockSpec((tm, S5_WIDTH), lambda b, i: (row_block(b, i), ub_col + 1)),
            _resident(wb.shape), _resident(cw.shape), _resident(ar.shape), _resident(ai.shape),
            _resident(d.shape), _resident(wglu.shape), _resident(bglu.shape),
        ],
        out_specs=[
            pl.BlockSpec((tm, S5_WIDTH), lambda b, i: (row_block(b, i), 0)),
            state_spec, state_spec,
        ],
        out_shape=[
            jax.ShapeDtypeStruct((batch * seq, S5_WIDTH), BF16),
            jax.ShapeDtypeStruct((batch, S5_PAIRS, LANES), F32),
            jax.ShapeDtypeStruct((batch, S5_PAIRS, LANES), F32),
        ],
        scratch_shapes=[
            pltpu.VMEM((tm * S5_PITCH, LANES), F32),
            pltpu.VMEM((tm * S5_PITCH, LANES), F32),
            pltpu.VMEM((tm, S5_WIDTH), F32),
        ],
        compiler_params=_params("arbitrary", "arbitrary"),
        name="s5_prompt",
    )(proj, proj, wb, cw, ar, ai, d, wglu, bglu)


def _s5_step_kernel(u_ref, z_ref, x0r_ref, x0i_ref, wb_ref, cw_ref, ar_ref, ai_ref, d_ref,
                    wglu_ref, bglu_ref, o_ref, sr_ref, si_ref, y_scr):
    for blk in range(S5_BLOCKS):
        cols = slice(blk * LANES, (blk + 1) * LANES)
        ub = u_ref[:, cols]
        acc = d_ref[:, cols] * ub.astype(F32)
        for j in range(blk * PAIRS_PER_BLOCK, (blk + 1) * PAIRS_PER_BLOCK):
            pc = slice(j * LANES, (j + 1) * LANES)
            bu = jnp.dot(ub, wb_ref[j], preferred_element_type=F32)
            ar = ar_ref[j:j + 1, :]
            ai = ai_ref[j:j + 1, :]
            x0r = x0r_ref[:, pc]
            x0i = x0i_ref[:, pc]
            nr = ar * x0r - ai * x0i + bu[:, :LANES]
            ni = ar * x0i + ai * x0r + bu[:, LANES:]
            sr_ref[:, pc] = nr
            si_ref[:, pc] = ni
            x = jnp.concatenate([nr, ni], axis=1).astype(BF16)
            acc = acc + jnp.dot(x, cw_ref[j], preferred_element_type=F32)
        y_scr[:, cols] = acc
    o_ref[...] = _s5_output_gate(y_scr[...], z_ref[...], wglu_ref, bglu_ref).astype(BF16)


def _s5_sample(proj, x0r, x0i, wb, cw, ar, ai, d, wglu, bglu):
    nb = proj.shape[0]
    ub_col = (4 * RET_WIDTH) // S5_WIDTH
    full = lambda a: pl.BlockSpec(a.shape, lambda i: (0,) * a.ndim)
    return pl.pallas_call(
        _s5_step_kernel,
        grid=(1,),
        in_specs=[
            pl.BlockSpec((nb, S5_WIDTH), lambda i: (0, ub_col)),
            pl.BlockSpec((nb, S5_WIDTH), lambda i: (0, ub_col + 1)),
            full(x0r), full(x0i), full(wb), full(cw), full(ar), full(ai), full(d), full(wglu), full(bglu),
        ],
        out_specs=[
            pl.BlockSpec((nb, S5_WIDTH), lambda i: (0, 0)),
            full(x0r), full(x0i),
        ],
        out_shape=[
            jax.ShapeDtypeStruct((nb, S5_WIDTH), BF16),
            jax.ShapeDtypeStruct(x0r.shape, F32),
            jax.ShapeDtypeStruct(x0i.shape, F32),
        ],
        scratch_shapes=[pltpu.VMEM((nb, S5_WIDTH), F32)],
        compiler_params=_params("arbitrary"),
        name="s5_sample",
    )(proj, proj, x0r, x0i, wb, cw, ar, ai, d, wglu, bglu)


def _merge_kernel(ya_ref, yb_ref, ga_ref, gb_ref, x_ref, wpa_ref, wpb_ref, wout_ref, gpost_ref, o_ref):
    ya = jnp.dot(ya_ref[...].astype(BF16), wpa_ref[...], preferred_element_type=F32)
    yb = jnp.dot(yb_ref[...].astype(BF16), wpb_ref[...], preferred_element_type=F32)
    merged = (jax.nn.sigmoid(ga_ref[...].astype(F32)) * ya
              + jax.nn.sigmoid(gb_ref[...].astype(F32)) * yb)
    out = jnp.dot(merged.astype(BF16), wout_ref[...], preferred_element_type=F32)
    ms = jnp.mean(out * out, axis=-1, keepdims=True)
    o_ref[...] = x_ref[...] + out * lax.rsqrt(ms + EPS) * gpost_ref[...]


def _merge(ya, yb, proj, x2d, wpa, wpb, wout, gpost, *, tm):
    m = x2d.shape[0]
    ga_col = (4 * RET_WIDTH + 2 * S5_WIDTH) // D_MODEL
    return pl.pallas_call(
        _merge_kernel,
        grid=(m // tm,),
        in_specs=[
            pl.BlockSpec((tm, RET_WIDTH), lambda i: (i, 0)),
            pl.BlockSpec((tm, S5_WIDTH), lambda i: (i, 0)),
            pl.BlockSpec((tm, D_MODEL), lambda i: (i, ga_col)),
            pl.BlockSpec((tm, D_MODEL), lambda i: (i, ga_col + 1)),
            pl.BlockSpec((tm, D_MODEL), lambda i: (i, 0)),
            _resident(wpa.shape), _resident(wpb.shape), _resident(wout.shape), _resident(gpost.shape),
        ],
        out_specs=pl.BlockSpec((tm, D_MODEL), lambda i: (i, 0)),
        out_shape=jax.ShapeDtypeStruct((m, D_MODEL), F32),
        compiler_params=_params("arbitrary"),
        name="merge_out",
    )(ya, yb, proj, proj, x2d, wpa, wpb, wout, gpost)


def _rope_tables(pos):
    half = HEAD_DIM // 2
    inv = ROPE_BASE ** (-jnp.arange(half, dtype=F32) / half)
    ang = pos.astype(F32)[:, None] * inv[None, :]
    cos, sin = jnp.cos(ang), jnp.sin(ang)
    return jnp.concatenate([cos, cos], axis=-1), jnp.concatenate([-sin, sin], axis=-1)


def _s5_discretize(lam_re, lam_im, log_dt, b_re, b_im):
    dt = jnp.exp(log_dt)[:, None]
    mag = jnp.exp(lam_re * dt)
    abar_r = mag * jnp.cos(lam_im * dt)
    abar_i = mag * jnp.sin(lam_im * dt)
    nr, ni = abar_r - 1.0, abar_i
    den = lam_re * lam_re + lam_im * lam_im
    coef_r = (nr * lam_re + ni * lam_im) / den
    coef_i = (ni * lam_re - nr * lam_im) / den
    bbar_r = coef_r[:, :, None] * b_re - coef_i[:, :, None] * b_im
    bbar_i = coef_r[:, :, None] * b_im + coef_i[:, :, None] * b_re
    return abar_r, abar_i, bbar_r, bbar_i


def _s5_pair_weights(bbar_r, bbar_i, c_re, c_im):
    slot = jax.nn.one_hot(jnp.arange(S5_PAIRS) % PAIRS_PER_BLOCK, PAIRS_PER_BLOCK, dtype=F32)
    eye = jnp.eye(2, dtype=F32)

    def in_w(b):
        b4 = b.reshape(S5_PAIRS, 2, S5_P, S5_GROUP)
        return jnp.einsum("jgpm,jq,gh->jqgmhp", b4, slot, eye).reshape(S5_PAIRS, LANES, LANES)

    def out_w(c):
        c4 = c.reshape(S5_PAIRS, 2, S5_GROUP, S5_P)
        return jnp.einsum("jgnp,jq,gh->jgpqhn", c4, slot, eye).reshape(S5_PAIRS, LANES, LANES)

    wb = jnp.concatenate([in_w(bbar_r), in_w(bbar_i)], axis=2).astype(BF16)
    cw = jnp.concatenate([out_w(c_re), out_w(-c_im)], axis=1).astype(BF16)
    return wb, cw


def kernel(x_prompt, x_sample, state_ret, state_s5_re, state_s5_im, g_pre, w_in, w_pa, w_pb, w_out, g_post,
           s5_lam_re, s5_lam_im, s5_log_dt, s5_b_re, s5_b_im, s5_c_re, s5_c_im, s5_d, s5_w_glu, s5_b_glu):
    assert w_in.shape[0] == 1, "single trunk layer"
    bp, lp, _ = x_prompt.shape
    bs, ls, _ = x_sample.shape
    assert ls == 1 and lp % RET_CHUNK == 0

    w_in_b = w_in[0].astype(BF16)
    w_pa_b = w_pa[0].astype(BF16)
    w_pb_b = w_pb[0].astype(BF16)
    w_out_b = w_out[0].astype(BF16)
    w_glu_b = s5_w_glu[0].astype(BF16)
    b_glu = s5_b_glu[0].reshape(1, S5_WIDTH)
    gpre = g_pre[0].reshape(1, D_MODEL)
    gpost = g_post[0].reshape(1, D_MODEL)

    abar_r, abar_i, bbar_r, bbar_i = _s5_discretize(
        s5_lam_re[0], s5_lam_im[0], s5_log_dt[0], s5_b_re[0], s5_b_im[0])
    wb, cw = _s5_pair_weights(bbar_r, bbar_i, s5_c_re[0], s5_c_im[0])
    ar = abar_r.reshape(S5_PAIRS, LANES)
    ai = abar_i.reshape(S5_PAIRS, LANES)
    d_skip = s5_d[0].reshape(1, S5_WIDTH)

    xp = x_prompt.reshape(bp * lp, D_MODEL)
    cos_p, sin_p = _rope_tables(jnp.arange(lp, dtype=jnp.int32))
    proj_p = _inproj(xp, gpre, w_in_b, cos_p, sin_p, tm=1024, tn=1024)
    ya_p, ret_p = _retention_prompt(proj_p, batch=bp, seq=lp, tm=256)
    yb_p, s5r_p, s5i_p = _s5_prompt(proj_p, wb, cw, ar, ai, d_skip, w_glu_b, b_glu, batch=bp, seq=lp, tm=256)
    y_p = _merge(ya_p, yb_p, proj_p, xp, w_pa_b, w_pb_b, w_out_b, gpost, tm=256)

    xs = x_sample.reshape(bs, D_MODEL)
    cos_s, sin_s = _rope_tables(jnp.full((bs,), PAST_LEN, jnp.int32))
    proj_s = _inproj(xs, gpre, w_in_b, cos_s, sin_s, tm=bs, tn=1024)
    ya_s, ret_s = _retention_sample(proj_s, state_ret[0], bb=16)
    x0r = state_s5_re[0].reshape(bs, S5_GROUPS * S5_P)
    x0i = state_s5_im[0].reshape(bs, S5_GROUPS * S5_P)
    yb_s, s5r_s, s5i_s = _s5_sample(proj_s, x0r, x0i, wb, cw, ar, ai, d_skip, w_glu_b, b_glu)
    y_s = _merge(ya_s, yb_s, proj_s, xs, w_pa_b, w_pb_b, w_out_b, gpost, tm=bs)

    state_shape = (1, -1, S5_GROUPS, S5_P)
    return (y_p.reshape(bp, lp, D_MODEL), y_s.reshape(bs, 1, D_MODEL),
            ret_p[None], s5r_p.reshape(state_shape), s5i_p.reshape(state_shape),
            ret_s[None], s5r_s.reshape(state_shape), s5i_s.reshape(state_shape))
```

```python
import functools
import math

import jax
import jax.numpy as jnp
import numpy as np
from jax import lax
from jax.experimental import pallas as pl
from jax.experimental.pallas import tpu as pltpu

F32 = jnp.float32
BF16 = jnp.bfloat16

D_MODEL = 2048
RET_WIDTH = 1024
RET_HEADS = 8
HEAD_DIM = RET_WIDTH // RET_HEADS
RET_CHUNK = 128
ROPE_BASE = 10000.0
S5_WIDTH = 1024
S5_GROUP = 16
S5_GROUPS = S5_WIDTH // S5_GROUP
S5_P = 64
PAST_LEN = 16384
EPS = 1e-6
IN_COLS = 4 * RET_WIDTH + 2 * S5_WIDTH + 2 * D_MODEL

LANES = 128
MXU_DIM = 256
S5_PAIRS = S5_GROUPS // 2
PAIRS_PER_BLOCK = LANES // (2 * S5_GROUP)
S5_BLOCKS = S5_WIDTH // LANES
S5_PITCH = 40
VMEM_LIMIT = 56 * 1024 * 1024

_LOG_GAMMA = [float(np.log(np.float32(1.0) - np.float32(2.0) ** np.float32(-5.0 - h)))
              for h in range(RET_HEADS)]


def _params(*sem):
    return pltpu.CompilerParams(dimension_semantics=sem, vmem_limit_bytes=VMEM_LIMIT)


def _resident(shape):
    nd = len(shape)
    return pl.BlockSpec(shape, lambda *_: (0,) * nd, pipeline_mode=pl.Buffered(1))


def _rope(x, cos, sin_signed):
    return x * cos + pltpu.roll(x, HEAD_DIM // 2, 1) * sin_signed


def _inproj_kernel(x_ref, g_ref, w_ref, cos_ref, sin_ref, o_ref, *rest, slab, tn):
    h_ref = rest[-1]
    wcopy_ref = rest[0] if len(rest) == 2 else None
    j = pl.program_id(1)

    @pl.when(j == 0)
    def _():
        g = g_ref[...]

        def body(r, _):
            rows = pl.ds(pl.multiple_of(r * slab, slab), slab)
            x = x_ref[rows, :]
            ms = jnp.mean(x * x, axis=-1, keepdims=True)
            h_ref[rows, :] = (x * lax.rsqrt(ms + EPS) * g).astype(BF16)
            return 0

        lax.fori_loop(0, x_ref.shape[0] // slab, body, 0)

    k_tile0 = RET_WIDTH // tn
    v_tile0 = 2 * RET_WIDTH // tn

    def slabs(epilogue):
        for s in range(tn // MXU_DIM):
            cols = slice(s * MXU_DIM, (s + 1) * MXU_DIM)
            w = w_ref[:, cols]
            if wcopy_ref is not None:
                w = w.astype(BF16)
                wcopy_ref[:, cols] = w
            acc = jnp.dot(h_ref[...], w, preferred_element_type=F32)
            o_ref[:, cols] = epilogue(acc).astype(BF16)

    @pl.when(j < v_tile0)
    def _():
        scale = jnp.where(j >= k_tile0, HEAD_DIM ** -0.5, 1.0).astype(F32)
        cos = cos_ref[...] * scale
        sin = sin_ref[...] * scale
        slabs(lambda acc: jnp.concatenate(
            [_rope(acc[:, hh * HEAD_DIM:(hh + 1) * HEAD_DIM], cos, sin)
             for hh in range(MXU_DIM // HEAD_DIM)], axis=1))

    @pl.when(j >= v_tile0)
    def _():
        slabs(lambda acc: acc)


def _inproj(x2d, g_pre, w_in, cos, sin, *, tm, tn):
    m, d = x2d.shape
    n = w_in.shape[1]
    pos_tiles = cos.shape[0] // tm
    emit_weights = w_in.dtype != BF16
    assert not emit_weights or m == tm, "each weight tile must be visited once to be copied out"
    out_specs = [pl.BlockSpec((tm, tn), lambda i, j: (i, j))]
    out_shape = [jax.ShapeDtypeStruct((m, n), BF16)]
    if emit_weights:
        out_specs.append(pl.BlockSpec((d, tn), lambda i, j: (0, j)))
        out_shape.append(jax.ShapeDtypeStruct((d, n), BF16))
    return pl.pallas_call(
        functools.partial(_inproj_kernel, slab=min(tm, 128), tn=tn),
        grid=(m // tm, n // tn),
        in_specs=[
            pl.BlockSpec((tm, d), lambda i, j: (i, 0)),
            pl.BlockSpec((1, d), lambda i, j: (0, 0)),
            pl.BlockSpec((d, tn), lambda i, j: (0, j)),
            pl.BlockSpec((tm, HEAD_DIM), lambda i, j: (i % pos_tiles, 0)),
            pl.BlockSpec((tm, HEAD_DIM), lambda i, j: (i % pos_tiles, 0)),
        ],
        out_specs=out_specs,
        out_shape=out_shape,
        scratch_shapes=[pltpu.VMEM((tm, d), BF16)],
        compiler_params=_params("arbitrary", "arbitrary"),
        name="inproj",
    )(x2d, g_pre, w_in, cos, sin)


def _group_norm(o):
    mu = jnp.mean(o, axis=-1, keepdims=True)
    d = o - mu
    var = jnp.mean(d * d, axis=-1, keepdims=True)
    return d * lax.rsqrt(var + EPS)


def _retention_kernel(q_ref, k_ref, v_ref, z_ref, o_ref, s_ref, *, n_chunks):
    c = RET_CHUNK

    @pl.when(pl.program_id(1) == 0)
    def _():
        s_ref[...] = jnp.zeros_like(s_ref)

    row = lax.broadcasted_iota(jnp.int32, (c, c), 0).astype(F32)
    col = lax.broadcasted_iota(jnp.int32, (c, c), 1).astype(F32)
    diff = row - col
    for h in range(RET_HEADS):
        lg = _LOG_GAMMA[h]
        mask = jnp.where(diff >= 0, jnp.exp(jnp.maximum(diff, 0.0) * lg), 0.0)
        q_decay = jnp.exp((row + 1.0) * lg)
        k_decay = jnp.exp((c - 1.0 - row) * lg)
        chunk_decay = math.exp(c * lg)
        cols = slice(h * HEAD_DIM, (h + 1) * HEAD_DIM)
        for ci in range(n_chunks):
            rows = slice(ci * c, (ci + 1) * c)
            q = q_ref[rows, cols]
            k = k_ref[rows, cols]
            v = v_ref[rows, cols]
            s0 = s_ref[0, h]
            scores = lax.dot_general(q, k, (((1,), (1,)), ((), ())),
                                     preferred_element_type=F32) * mask
            inner = jnp.dot(scores.astype(BF16), v, preferred_element_type=F32)
            cross = jnp.dot(q, s0.astype(BF16), preferred_element_type=F32) * q_decay
            kd = (k.astype(F32) * k_decay).astype(BF16)
            s_ref[0, h] = chunk_decay * s0 + lax.dot_general(
                kd, v, (((0,), (0,)), ((), ())), preferred_element_type=F32)
            o = _group_norm(inner + cross)
            o_ref[rows, cols] = (o * jax.nn.silu(z_ref[rows, cols].astype(F32))).astype(BF16)


def _retention_prompt(proj, *, batch, seq, tm):
    nt = seq // tm
    row_block = lambda b, i: b * nt + i
    col_spec = lambda cb: pl.BlockSpec((tm, RET_WIDTH), lambda b, i: (row_block(b, i), cb))
    return pl.pallas_call(
        functools.partial(_retention_kernel, n_chunks=tm // RET_CHUNK),
        grid=(batch, nt),
        in_specs=[col_spec(0), col_spec(1), col_spec(2), col_spec(3)],
        out_specs=[
            pl.BlockSpec((tm, RET_WIDTH), lambda b, i: (row_block(b, i), 0)),
            pl.BlockSpec((1, RET_HEADS, HEAD_DIM, HEAD_DIM), lambda b, i: (b, 0, 0, 0)),
        ],
        out_shape=[
            jax.ShapeDtypeStruct((batch * seq, RET_WIDTH), BF16),
            jax.ShapeDtypeStruct((batch, RET_HEADS, HEAD_DIM, HEAD_DIM), F32),
        ],
        compiler_params=_params("arbitrary", "arbitrary"),
        name="retention_prompt",
    )(proj, proj, proj, proj)


def _retention_step_kernel(q_ref, k_ref, v_ref, z_ref, s_ref, o_ref, sn_ref, acc_ref, *, bb):
    pad = jnp.zeros((HEAD_DIM - bb, HEAD_DIM), F32)
    for h in range(RET_HEADS):
        gamma = math.exp(_LOG_GAMMA[h])
        cols = slice(h * HEAD_DIM, (h + 1) * HEAD_DIM)
        q = q_ref[:, cols].astype(F32)
        k = k_ref[:, cols].astype(F32)
        v = v_ref[:, cols].astype(F32)
        qt = jnp.concatenate([q, pad], axis=0).T
        kt = jnp.concatenate([k, pad], axis=0).T
        for b in range(bb):
            s_new = gamma * s_ref[b, h] + kt[:, b:b + 1] * v[b:b + 1, :]
            sn_ref[b, h] = s_new
            acc_ref[b:b + 1, cols] = jnp.sum(qt[:, b:b + 1] * s_new, axis=0, keepdims=True)
    for h in range(RET_HEADS):
        cols = slice(h * HEAD_DIM, (h + 1) * HEAD_DIM)
        o_ref[:, cols] = _group_norm(acc_ref[:, cols]) * jax.nn.silu(z_ref[:, cols].astype(F32))


def _retention_sample(proj, state, *, bb):
    nb = state.shape[0]
    col_spec = lambda cb: pl.BlockSpec((bb, RET_WIDTH), lambda i: (i, cb))
    state_spec = pl.BlockSpec((bb, RET_HEADS, HEAD_DIM, HEAD_DIM), lambda i: (i, 0, 0, 0))
    return pl.pallas_call(
        functools.partial(_retention_step_kernel, bb=bb),
        grid=(nb // bb,),
        in_specs=[col_spec(0), col_spec(1), col_spec(2), col_spec(3), state_spec],
        out_specs=[pl.BlockSpec((bb, RET_WIDTH), lambda i: (i, 0)), state_spec],
        out_shape=[
            jax.ShapeDtypeStruct((nb, RET_WIDTH), F32),
            jax.ShapeDtypeStruct(state.shape, F32),
        ],
        scratch_shapes=[pltpu.VMEM((bb, RET_WIDTH), F32)],
        compiler_params=_params("arbitrary"),
        name="retention_sample",
    )(proj, proj, proj, proj, state)


def _s5_output_gate(y, z, wglu_ref, bglu_ref):
    y = jax.nn.gelu(y)
    g = jnp.dot(y.astype(BF16), wglu_ref[...], preferred_element_type=F32) + bglu_ref[...]
    return y * jax.nn.sigmoid(g) * jax.nn.silu(z.astype(F32))


def _s5_prompt_kernel(u_ref, z_ref, wb_ref, cw_ref, ar_ref, ai_ref, d_ref, wglu_ref, bglu_ref,
                      o_ref, sr_ref, si_ref, xr_scr, xi_scr, y_scr, *, tm):
    @pl.when(pl.program_id(1) == 0)
    def _():
        sr_ref[...] = jnp.zeros_like(sr_ref)
        si_ref[...] = jnp.zeros_like(si_ref)

    for blk in range(S5_BLOCKS):
        ub = u_ref[:, blk * LANES:(blk + 1) * LANES]
        for j in range(blk * PAIRS_PER_BLOCK, (blk + 1) * PAIRS_PER_BLOCK):
            bu = jnp.dot(ub, wb_ref[j], preferred_element_type=F32)
            xr_scr[pl.ds(j, tm, stride=S5_PITCH), :] = bu[:, :LANES]
            xi_scr[pl.ds(j, tm, stride=S5_PITCH), :] = bu[:, LANES:]

    ar = ar_ref[...]
    ai = ai_ref[...]

    def step(t, carry):
        xr, xi = carry
        rows = pl.ds(pl.multiple_of(t * S5_PITCH, 8), S5_PAIRS)
        nr = ar * xr - ai * xi + xr_scr[rows, :]
        ni = ar * xi + ai * xr + xi_scr[rows, :]
        xr_scr[rows, :] = nr
        xi_scr[rows, :] = ni
        return nr, ni

    xr, xi = lax.fori_loop(0, tm, step, (sr_ref[0], si_ref[0]), unroll=4)
    sr_ref[0] = xr
    si_ref[0] = xi

    for blk in range(S5_BLOCKS):
        cols = slice(blk * LANES, (blk + 1) * LANES)
        acc = d_ref[:, cols] * u_ref[:, cols].astype(F32)
        for j in range(blk * PAIRS_PER_BLOCK, (blk + 1) * PAIRS_PER_BLOCK):
            x = jnp.concatenate([xr_scr[pl.ds(j, tm, stride=S5_PITCH), :],
                                 xi_scr[pl.ds(j, tm, stride=S5_PITCH), :]], axis=1)
            acc = acc + jnp.dot(x.astype(BF16), cw_ref[j], preferred_element_type=F32)
        y_scr[:, cols] = acc
    o_ref[...] = _s5_output_gate(y_scr[...], z_ref[...], wglu_ref, bglu_ref).astype(BF16)


def _s5_prompt(proj, wb, cw, ar, ai, d, wglu, bglu, *, batch, seq, tm):
    nt = seq // tm
    row_block = lambda b, i: b * nt + i
    ub_col = (4 * RET_WIDTH) // S5_WIDTH
    state_spec = pl.BlockSpec((1, S5_PAIRS, LANES), lambda b, i: (b, 0, 0))
    return pl.pallas_call(
        functools.partial(_s5_prompt_kernel, tm=tm),
        grid=(batch, nt),
        in_specs=[
            pl.BlockSpec((tm, S5_WIDTH), lambda b, i: (row_block(b, i), ub_col)),
            pl.BlockSpec((tm, S5_WIDTH), lambda b, i: (row_block(b, i), ub_col + 1)),
            _resident(wb.shape), _resident(cw.shape), _resident(ar.shape), _resident(ai.shape),
            _resident(d.shape), _resident(wglu.shape), _resident(bglu.shape),
        ],
        out_specs=[
            pl.BlockSpec((tm, S5_WIDTH), lambda b, i: (row_block(b, i), 0)),
            state_spec, state_spec,
        ],
        out_shape=[
            jax.ShapeDtypeStruct((batch * seq, S5_WIDTH), BF16),
            jax.ShapeDtypeStruct((batch, S5_PAIRS, LANES), F32),
            jax.ShapeDtypeStruct((batch, S5_PAIRS, LANES), F32),
        ],
        scratch_shapes=[
            pltpu.VMEM((tm * S5_PITCH, LANES), F32),
            pltpu.VMEM((tm * S5_PITCH, LANES), F32),
            pltpu.VMEM((tm, S5_WIDTH), F32),
        ],
        compiler_params=_params("arbitrary", "arbitrary"),
        name="s5_prompt",
    )(proj, proj, wb, cw, ar, ai, d, wglu, bglu)


def _s5_step_kernel(u_ref, z_ref, x0r_ref, x0i_ref, wb_ref, cw_ref, ar_ref, ai_ref, d_ref,
                    wglu_ref, bglu_ref, o_ref, sr_ref, si_ref, y_scr):
    for blk in range(S5_BLOCKS):
        cols = slice(blk * LANES, (blk + 1) * LANES)
        ub = u_ref[:, cols]
        acc = d_ref[:, cols] * ub.astype(F32)
        for j in range(blk * PAIRS_PER_BLOCK, (blk + 1) * PAIRS_PER_BLOCK):
            pc = slice(j * LANES, (j + 1) * LANES)
            bu = jnp.dot(ub, wb_ref[j], preferred_element_type=F32)
            ar = ar_ref[j:j + 1, :]
            ai = ai_ref[j:j + 1, :]
            x0r = x0r_ref[:, pc]
            x0i = x0i_ref[:, pc]
            nr = ar * x0r - ai * x0i + bu[:, :LANES]
            ni = ar * x0i + ai * x0r + bu[:, LANES:]
            sr_ref[:, pc] = nr
            si_ref[:, pc] = ni
            x = jnp.concatenate([nr, ni], axis=1).astype(BF16)
            acc = acc + jnp.dot(x, cw_ref[j], preferred_element_type=F32)
        y_scr[:, cols] = acc
    o_ref[...] = _s5_output_gate(y_scr[...], z_ref[...], wglu_ref, bglu_ref).astype(BF16)


def _s5_sample(proj, x0r, x0i, wb, cw, ar, ai, d, wglu, bglu):
    nb = proj.shape[0]
    ub_col = (4 * RET_WIDTH) // S5_WIDTH
    full = lambda a: pl.BlockSpec(a.shape, lambda i: (0,) * a.ndim)
    return pl.pallas_call(
        _s5_step_kernel,
        grid=(1,),
        in_specs=[
            pl.BlockSpec((nb, S5_WIDTH), lambda i: (0, ub_col)),
            pl.BlockSpec((nb, S5_WIDTH), lambda i: (0, ub_col + 1)),
            full(x0r), full(x0i), full(wb), full(cw), full(ar), full(ai), full(d), full(wglu), full(bglu),
        ],
        out_specs=[
            pl.BlockSpec((nb, S5_WIDTH), lambda i: (0, 0)),
            full(x0r), full(x0i),
        ],
        out_shape=[
            jax.ShapeDtypeStruct((nb, S5_WIDTH), BF16),
            jax.ShapeDtypeStruct(x0r.shape, F32),
            jax.ShapeDtypeStruct(x0i.shape, F32),
        ],
        scratch_shapes=[pltpu.VMEM((nb, S5_WIDTH), F32)],
        compiler_params=_params("arbitrary"),
        name="s5_sample",
    )(proj, proj, x0r, x0i, wb, cw, ar, ai, d, wglu, bglu)


def _merge_kernel(ya_ref, yb_ref, ga_ref, gb_ref, x_ref, wpa_ref, wpb_ref, wout_ref, gpost_ref, o_ref):
    ya = jnp.dot(ya_ref[...].astype(BF16), wpa_ref[...], preferred_element_type=F32)
    yb = jnp.dot(yb_ref[...].astype(BF16), wpb_ref[...], preferred_element_type=F32)
    merged = (jax.nn.sigmoid(ga_ref[...].astype(F32)) * ya
              + jax.nn.sigmoid(gb_ref[...].astype(F32)) * yb)
    out = jnp.dot(merged.astype(BF16), wout_ref[...], preferred_element_type=F32)
    ms = jnp.mean(out * out, axis=-1, keepdims=True)
    o_ref[...] = x_ref[...] + out * lax.rsqrt(ms + EPS) * gpost_ref[...]


def _merge(ya, yb, proj, x2d, wpa, wpb, wout, gpost, *, tm):
    m = x2d.shape[0]
    ga_col = (4 * RET_WIDTH + 2 * S5_WIDTH) // D_MODEL
    return pl.pallas_call(
        _merge_kernel,
        grid=(m // tm,),
        in_specs=[
            pl.BlockSpec((tm, RET_WIDTH), lambda i: (i, 0)),
            pl.BlockSpec((tm, S5_WIDTH), lambda i: (i, 0)),
            pl.BlockSpec((tm, D_MODEL), lambda i: (i, ga_col)),
            pl.BlockSpec((tm, D_MODEL), lambda i: (i, ga_col + 1)),
            pl.BlockSpec((tm, D_MODEL), lambda i: (i, 0)),
            _resident(wpa.shape), _resident(wpb.shape), _resident(wout.shape), _resident(gpost.shape),
        ],
        out_specs=pl.BlockSpec((tm, D_MODEL), lambda i: (i, 0)),
        out_shape=jax.ShapeDtypeStruct((m, D_MODEL), F32),
        compiler_params=_params("arbitrary"),
        name="merge_out",
    )(ya, yb, proj, proj, x2d, wpa, wpb, wout, gpost)


def _rope_tables(pos):
    half = HEAD_DIM // 2
    inv = ROPE_BASE ** (-jnp.arange(half, dtype=F32) / half)
    ang = pos.astype(F32)[:, None] * inv[None, :]
    cos, sin = jnp.cos(ang), jnp.sin(ang)
    return jnp.concatenate([cos, cos], axis=-1), jnp.concatenate([-sin, sin], axis=-1)


def _s5_discretize(lam_re, lam_im, log_dt, b_re, b_im):
    dt = jnp.exp(log_dt)[:, None]
    mag = jnp.exp(lam_re * dt)
    abar_r = mag * jnp.cos(lam_im * dt)
    abar_i = mag * jnp.sin(lam_im * dt)
    nr, ni = abar_r - 1.0, abar_i
    den = lam_re * lam_re + lam_im * lam_im
    coef_r = (nr * lam_re + ni * lam_im) / den
    coef_i = (ni * lam_re - nr * lam_im) / den
    bbar_r = coef_r[:, :, None] * b_re - coef_i[:, :, None] * b_im
    bbar_i = coef_r[:, :, None] * b_im + coef_i[:, :, None] * b_re
    return abar_r, abar_i, bbar_r, bbar_i


def _s5_pair_weights(bbar_r, bbar_i, c_re, c_im):
    slot = jnp.arange(S5_PAIRS)[:, None] % PAIRS_PER_BLOCK == jnp.arange(PAIRS_PER_BLOCK)[None, :]
    keep = (slot[:, :, None, None] & jnp.eye(2, dtype=bool)[None, None]).astype(F32)

    b = jnp.stack([bbar_r, bbar_i]).reshape(2, S5_PAIRS, 2, S5_P, S5_GROUP)
    b = b.transpose(1, 4, 0, 2, 3)[:, None, None]
    wb = b * keep[:, :, :, None, None, :, None]
    c = jnp.stack([c_re, -c_im]).reshape(2, S5_PAIRS, 2, S5_GROUP, S5_P)
    c = c.transpose(1, 0, 2, 4, 3)[:, :, :, :, None, None]
    cw = c * keep.transpose(0, 2, 1, 3)[:, None, :, None, :, :, None]
    return (wb.reshape(S5_PAIRS, LANES, 2 * LANES).astype(BF16),
            cw.reshape(S5_PAIRS, 2 * LANES, LANES).astype(BF16))


def kernel(x_prompt, x_sample, state_ret, state_s5_re, state_s5_im, g_pre, w_in, w_pa, w_pb, w_out, g_post,
           s5_lam_re, s5_lam_im, s5_log_dt, s5_b_re, s5_b_im, s5_c_re, s5_c_im, s5_d, s5_w_glu, s5_b_glu):
    assert w_in.shape[0] == 1, "single trunk layer"
    bp, lp, _ = x_prompt.shape
    bs, ls, _ = x_sample.shape
    assert ls == 1 and lp % RET_CHUNK == 0

    w_pa_b = w_pa[0].astype(BF16)
    w_pb_b = w_pb[0].astype(BF16)
    w_out_b = w_out[0].astype(BF16)
    w_glu_b = s5_w_glu[0].astype(BF16)
    b_glu = s5_b_glu[0].reshape(1, S5_WIDTH)
    gpre = g_pre[0].reshape(1, D_MODEL)
    gpost = g_post[0].reshape(1, D_MODEL)

    abar_r, abar_i, bbar_r, bbar_i = _s5_discretize(
        s5_lam_re[0], s5_lam_im[0], s5_log_dt[0], s5_b_re[0], s5_b_im[0])
    wb, cw = _s5_pair_weights(bbar_r, bbar_i, s5_c_re[0], s5_c_im[0])
    ar = abar_r.reshape(S5_PAIRS, LANES)
    ai = abar_i.reshape(S5_PAIRS, LANES)
    d_skip = s5_d[0].reshape(1, S5_WIDTH)

    xs = x_sample.reshape(bs, D_MODEL)
    cos_s, sin_s = _rope_tables(jnp.full((bs,), PAST_LEN, jnp.int32))
    proj_s, w_in_b = _inproj(xs, gpre, w_in[0], cos_s, sin_s, tm=bs, tn=1024)

    xp = x_prompt.reshape(bp * lp, D_MODEL)
    cos_p, sin_p = _rope_tables(jnp.arange(lp, dtype=jnp.int32))
    (proj_p,) = _inproj(xp, gpre, w_in_b, cos_p, sin_p, tm=1024, tn=1024)
    ya_p, ret_p = _retention_prompt(proj_p, batch=bp, seq=lp, tm=256)
    yb_p, s5r_p, s5i_p = _s5_prompt(proj_p, wb, cw, ar, ai, d_skip, w_glu_b, b_glu, batch=bp, seq=lp, tm=256)
    y_p = _merge(ya_p, yb_p, proj_p, xp, w_pa_b, w_pb_b, w_out_b, gpost, tm=256)

    ya_s, ret_s = _retention_sample(proj_s, state_ret[0], bb=16)
    x0r = state_s5_re[0].reshape(bs, S5_GROUPS * S5_P)
    x0i = state_s5_im[0].reshape(bs, S5_GROUPS * S5_P)
    yb_s, s5r_s, s5i_s = _s5_sample(proj_s, x0r, x0i, wb, cw, ar, ai, d_skip, w_glu_b, b_glu)
    y_s = _merge(ya_s, yb_s, proj_s, xs, w_pa_b, w_pb_b, w_out_b, gpost, tm=bs)

    state_shape = (1, -1, S5_GROUPS, S5_P)
    return (y_p.reshape(bp, lp, D_MODEL), y_s.reshape(bs, 1, D_MODEL),
            ret_p[None], s5r_p.reshape(state_shape), s5i_p.reshape(state_shape),
            ret_s[None], s5r_s.reshape(state_shape), s5i_s.reshape(state_shape))
```

```python
import functools
import math

import jax
import jax.numpy as jnp
import numpy as np
from jax import lax
from jax.experimental import pallas as pl
from jax.experimental.pallas import tpu as pltpu

F32 = jnp.float32
BF16 = jnp.bfloat16

D_MODEL = 2048
RET_WIDTH = 1024
RET_HEADS = 8
HEAD_DIM = RET_WIDTH // RET_HEADS
RET_CHUNK = 128
ROPE_BASE = 10000.0
S5_WIDTH = 1024
S5_GROUP = 16
S5_GROUPS = S5_WIDTH // S5_GROUP
S5_P = 64
PAST_LEN = 16384
EPS = 1e-6
IN_COLS = 4 * RET_WIDTH + 2 * S5_WIDTH + 2 * D_MODEL

LANES = 128
MXU_DIM = 256
S5_PAIRS = S5_GROUPS // 2
PAIRS_PER_BLOCK = LANES // (2 * S5_GROUP)
S5_BLOCKS = S5_WIDTH // LANES
S5_PITCH = 36
VMEM_LIMIT = 56 * 1024 * 1024

_LOG_GAMMA = [float(np.log(np.float32(1.0) - np.float32(2.0) ** np.float32(-5.0 - h)))
              for h in range(RET_HEADS)]


def _params(*sem):
    return pltpu.CompilerParams(dimension_semantics=sem, vmem_limit_bytes=VMEM_LIMIT)


def _resident(shape):
    nd = len(shape)
    return pl.BlockSpec(shape, lambda *_: (0,) * nd, pipeline_mode=pl.Buffered(1))


def _rope(x, cos, sin_signed):
    return x * cos + pltpu.roll(x, HEAD_DIM // 2, 1) * sin_signed


def _inproj_kernel(x_ref, g_ref, w_ref, cos_ref, sin_ref, o_ref, *rest, slab, tn):
    h_ref = rest[-1]
    wcopy_ref = rest[0] if len(rest) == 2 else None
    j = pl.program_id(1)

    @pl.when(j == 0)
    def _():
        g = g_ref[...]

        def body(r, _):
            rows = pl.ds(pl.multiple_of(r * slab, slab), slab)
            x = x_ref[rows, :]
            ms = jnp.mean(x * x, axis=-1, keepdims=True)
            h_ref[rows, :] = (x * lax.rsqrt(ms + EPS) * g).astype(BF16)
            return 0

        lax.fori_loop(0, x_ref.shape[0] // slab, body, 0)

    k_tile0 = RET_WIDTH // tn
    v_tile0 = 2 * RET_WIDTH // tn

    def slabs(epilogue):
        for s in range(tn // MXU_DIM):
            cols = slice(s * MXU_DIM, (s + 1) * MXU_DIM)
            w = w_ref[:, cols]
            if wcopy_ref is not None:
                w = w.astype(BF16)
                wcopy_ref[:, cols] = w
            acc = jnp.dot(h_ref[...], w, preferred_element_type=F32)
            o_ref[:, cols] = epilogue(acc).astype(BF16)

    @pl.when(j < v_tile0)
    def _():
        scale = jnp.where(j >= k_tile0, HEAD_DIM ** -0.5, 1.0).astype(F32)
        cos = cos_ref[...] * scale
        sin = sin_ref[...] * scale
        slabs(lambda acc: jnp.concatenate(
            [_rope(acc[:, hh * HEAD_DIM:(hh + 1) * HEAD_DIM], cos, sin)
             for hh in range(MXU_DIM // HEAD_DIM)], axis=1))

    @pl.when(j >= v_tile0)
    def _():
        slabs(lambda acc: acc)


def _inproj(x2d, g_pre, w_in, cos, sin, *, tm, tn):
    m, d = x2d.shape
    n = w_in.shape[1]
    pos_tiles = cos.shape[0] // tm
    emit_weights = w_in.dtype != BF16
    assert not emit_weights or m == tm, "each weight tile must be visited once to be copied out"
    out_specs = [pl.BlockSpec((tm, tn), lambda i, j: (i, j))]
    out_shape = [jax.ShapeDtypeStruct((m, n), BF16)]
    if emit_weights:
        out_specs.append(pl.BlockSpec((d, tn), lambda i, j: (0, j)))
        out_shape.append(jax.ShapeDtypeStruct((d, n), BF16))
    return pl.pallas_call(
        functools.partial(_inproj_kernel, slab=min(tm, 128), tn=tn),
        grid=(m // tm, n // tn),
        in_specs=[
            pl.BlockSpec((tm, d), lambda i, j: (i, 0)),
            pl.BlockSpec((1, d), lambda i, j: (0, 0)),
            pl.BlockSpec((d, tn), lambda i, j: (0, j)),
            pl.BlockSpec((tm, HEAD_DIM), lambda i, j: (i % pos_tiles, 0)),
            pl.BlockSpec((tm, HEAD_DIM), lambda i, j: (i % pos_tiles, 0)),
        ],
        out_specs=out_specs,
        out_shape=out_shape,
        scratch_shapes=[pltpu.VMEM((tm, d), BF16)],
        compiler_params=_params("arbitrary", "arbitrary"),
        name="inproj",
    )(x2d, g_pre, w_in, cos, sin)


def _group_norm(o):
    mu = jnp.mean(o, axis=-1, keepdims=True)
    d = o - mu
    var = jnp.mean(d * d, axis=-1, keepdims=True)
    return d * lax.rsqrt(var + EPS)


def _retention_kernel(q_ref, k_ref, v_ref, z_ref, o_ref, s_ref, *, n_chunks):
    c = RET_CHUNK

    @pl.when(pl.program_id(1) == 0)
    def _():
        s_ref[...] = jnp.zeros_like(s_ref)

    row = lax.broadcasted_iota(jnp.int32, (c, c), 0).astype(F32)
    col = lax.broadcasted_iota(jnp.int32, (c, c), 1).astype(F32)
    diff = row - col
    for h in range(RET_HEADS):
        lg = _LOG_GAMMA[h]
        mask = jnp.where(diff >= 0, jnp.exp(jnp.maximum(diff, 0.0) * lg), 0.0)
        q_decay = jnp.exp((row + 1.0) * lg)
        k_decay = jnp.exp((c - 1.0 - row) * lg)
        chunk_decay = math.exp(c * lg)
        cols = slice(h * HEAD_DIM, (h + 1) * HEAD_DIM)
        for ci in range(n_chunks):
            rows = slice(ci * c, (ci + 1) * c)
            q = q_ref[rows, cols]
            k = k_ref[rows, cols]
            v = v_ref[rows, cols]
            s0 = s_ref[0, h]
            scores = lax.dot_general(q, k, (((1,), (1,)), ((), ())),
                                     preferred_element_type=F32) * mask
            inner = jnp.dot(scores.astype(BF16), v, preferred_element_type=F32)
            cross = jnp.dot(q, s0.astype(BF16), preferred_element_type=F32) * q_decay
            kd = (k.astype(F32) * k_decay).astype(BF16)
            s_ref[0, h] = chunk_decay * s0 + lax.dot_general(
                kd, v, (((0,), (0,)), ((), ())), preferred_element_type=F32)
            o = _group_norm(inner + cross)
            o_ref[rows, cols] = (o * jax.nn.silu(z_ref[rows, cols].astype(F32))).astype(BF16)


def _retention_prompt(proj, *, batch, seq, tm):
    nt = seq // tm
    row_block = lambda b, i: b * nt + i
    col_spec = lambda cb: pl.BlockSpec((tm, RET_WIDTH), lambda b, i: (row_block(b, i), cb))
    return pl.pallas_call(
        functools.partial(_retention_kernel, n_chunks=tm // RET_CHUNK),
        grid=(batch, nt),
        in_specs=[col_spec(0), col_spec(1), col_spec(2), col_spec(3)],
        out_specs=[
            pl.BlockSpec((tm, RET_WIDTH), lambda b, i: (row_block(b, i), 0)),
            pl.BlockSpec((1, RET_HEADS, HEAD_DIM, HEAD_DIM), lambda b, i: (b, 0, 0, 0)),
        ],
        out_shape=[
            jax.ShapeDtypeStruct((batch * seq, RET_WIDTH), BF16),
            jax.ShapeDtypeStruct((batch, RET_HEADS, HEAD_DIM, HEAD_DIM), F32),
        ],
        compiler_params=_params("arbitrary", "arbitrary"),
        name="retention_prompt",
    )(proj, proj, proj, proj)


def _retention_step_kernel(q_ref, k_ref, v_ref, z_ref, s_ref, o_ref, sn_ref, acc_ref, *, bb):
    pad = jnp.zeros((HEAD_DIM - bb, HEAD_DIM), F32)
    for h in range(RET_HEADS):
        gamma = math.exp(_LOG_GAMMA[h])
        cols = slice(h * HEAD_DIM, (h + 1) * HEAD_DIM)
        q = q_ref[:, cols].astype(F32)
        k = k_ref[:, cols].astype(F32)
        v = v_ref[:, cols].astype(F32)
        qt = jnp.concatenate([q, pad], axis=0).T
        kt = jnp.concatenate([k, pad], axis=0).T
        for b in range(bb):
            s_new = gamma * s_ref[b, h] + kt[:, b:b + 1] * v[b:b + 1, :]
            sn_ref[b, h] = s_new
            acc_ref[b:b + 1, cols] = jnp.sum(qt[:, b:b + 1] * s_new, axis=0, keepdims=True)
    for h in range(RET_HEADS):
        cols = slice(h * HEAD_DIM, (h + 1) * HEAD_DIM)
        o_ref[:, cols] = _group_norm(acc_ref[:, cols]) * jax.nn.silu(z_ref[:, cols].astype(F32))


def _retention_sample(proj, state, *, bb):
    nb = state.shape[0]
    col_spec = lambda cb: pl.BlockSpec((bb, RET_WIDTH), lambda i: (i, cb))
    state_spec = pl.BlockSpec((bb, RET_HEADS, HEAD_DIM, HEAD_DIM), lambda i: (i, 0, 0, 0))
    return pl.pallas_call(
        functools.partial(_retention_step_kernel, bb=bb),
        grid=(nb // bb,),
        in_specs=[col_spec(0), col_spec(1), col_spec(2), col_spec(3), state_spec],
        out_specs=[pl.BlockSpec((bb, RET_WIDTH), lambda i: (i, 0)), state_spec],
        out_shape=[
            jax.ShapeDtypeStruct((nb, RET_WIDTH), F32),
            jax.ShapeDtypeStruct(state.shape, F32),
        ],
        scratch_shapes=[pltpu.VMEM((bb, RET_WIDTH), F32)],
        compiler_params=_params("arbitrary"),
        name="retention_sample",
    )(proj, proj, proj, proj, state)


def _s5_output_gate(y, z, wglu_ref, bglu_ref):
    y = jax.nn.gelu(y)
    g = jnp.dot(y.astype(BF16), wglu_ref[...], preferred_element_type=F32) + bglu_ref[...]
    return y * jax.nn.sigmoid(g) * jax.nn.silu(z.astype(F32))


def _s5_prompt_kernel(u_ref, z_ref, wb_ref, cw_ref, ar_ref, ai_ref, d_ref, wglu_ref, bglu_ref,
                      o_ref, sr_ref, si_ref, y_scr, *x_scr, sub):
    n_sub = len(x_scr) // 2
    xr_scr, xi_scr = x_scr[:n_sub], x_scr[n_sub:]
    strided = lambda j: pl.ds(j, sub, stride=S5_PITCH)

    @pl.when(pl.program_id(1) == 0)
    def _():
        sr_ref[...] = jnp.zeros_like(sr_ref)
        si_ref[...] = jnp.zeros_like(si_ref)

    ar = ar_ref[...]
    ai = ai_ref[...]
    xr, xi = sr_ref[0], si_ref[0]
    for k in range(n_sub):
        rows = slice(k * sub, (k + 1) * sub)
        for blk in range(S5_BLOCKS):
            ub = u_ref[rows, blk * LANES:(blk + 1) * LANES]
            for j in range(blk * PAIRS_PER_BLOCK, (blk + 1) * PAIRS_PER_BLOCK):
                bu = jnp.dot(ub, wb_ref[j], preferred_element_type=F32)
                xr_scr[k][strided(j), :] = bu[:, :LANES]
                xi_scr[k][strided(j), :] = bu[:, LANES:]
        for t in range(sub):
            step = slice(t * S5_PITCH, t * S5_PITCH + S5_PAIRS)
            xr, xi = (ar * xr - ai * xi + xr_scr[k][step, :],
                      ar * xi + ai * xr + xi_scr[k][step, :])
            xr_scr[k][step, :] = xr
            xi_scr[k][step, :] = xi
        for blk in range(S5_BLOCKS):
            cols = slice(blk * LANES, (blk + 1) * LANES)
            acc = d_ref[:, cols] * u_ref[rows, cols].astype(F32)
            for j in range(blk * PAIRS_PER_BLOCK, (blk + 1) * PAIRS_PER_BLOCK):
                x = jnp.concatenate([xr_scr[k][strided(j), :], xi_scr[k][strided(j), :]], axis=1)
                acc = acc + lax.dot_general(x.astype(BF16), cw_ref[j], (((1,), (1,)), ((), ())),
                                            preferred_element_type=F32)
            y_scr[rows, cols] = acc
        o_ref[rows, :] = _s5_output_gate(
            y_scr[rows, :], z_ref[rows, :], wglu_ref, bglu_ref).astype(BF16)
    sr_ref[0] = xr
    si_ref[0] = xi


def _s5_prompt(proj, wb, cw, ar, ai, d, wglu, bglu, *, batch, seq, tm, sub):
    nt = seq // tm
    row_block = lambda b, i: b * nt + i
    ub_col = (4 * RET_WIDTH) // S5_WIDTH
    state_spec = pl.BlockSpec((1, S5_PAIRS, LANES), lambda b, i: (b, 0, 0))
    return pl.pallas_call(
        functools.partial(_s5_prompt_kernel, sub=sub),
        grid=(batch, nt),
        in_specs=[
            pl.BlockSpec((tm, S5_WIDTH), lambda b, i: (row_block(b, i), ub_col)),
            pl.BlockSpec((tm, S5_WIDTH), lambda b, i: (row_block(b, i), ub_col + 1)),
            _resident(wb.shape), _resident(cw.shape), _resident(ar.shape), _resident(ai.shape),
            _resident(d.shape), _resident(wglu.shape), _resident(bglu.shape),
        ],
        out_specs=[
            pl.BlockSpec((tm, S5_WIDTH), lambda b, i: (row_block(b, i), 0)),
            state_spec, state_spec,
        ],
        out_shape=[
            jax.ShapeDtypeStruct((batch * seq, S5_WIDTH), BF16),
            jax.ShapeDtypeStruct((batch, S5_PAIRS, LANES), F32),
            jax.ShapeDtypeStruct((batch, S5_PAIRS, LANES), F32),
        ],
        scratch_shapes=[pltpu.VMEM((tm, S5_WIDTH), F32)]
        + [pltpu.VMEM((sub * S5_PITCH, LANES), F32)] * (2 * (tm // sub)),
        compiler_params=_params("arbitrary", "arbitrary"),
        name="s5_prompt",
    )(proj, proj, wb, cw, ar, ai, d, wglu, bglu)


def _s5_step_kernel(u_ref, z_ref, x0r_ref, x0i_ref, wb_ref, cw_ref, ar_ref, ai_ref, d_ref,
                    wglu_ref, bglu_ref, o_ref, sr_ref, si_ref, y_scr):
    for blk in range(S5_BLOCKS):
        cols = slice(blk * LANES, (blk + 1) * LANES)
        ub = u_ref[:, cols]
        acc = d_ref[:, cols] * ub.astype(F32)
        for j in range(blk * PAIRS_PER_BLOCK, (blk + 1) * PAIRS_PER_BLOCK):
            pc = slice(j * LANES, (j + 1) * LANES)
            bu = jnp.dot(ub, wb_ref[j], preferred_element_type=F32)
            ar = ar_ref[j:j + 1, :]
            ai = ai_ref[j:j + 1, :]
            x0r = x0r_ref[:, pc]
            x0i = x0i_ref[:, pc]
            nr = ar * x0r - ai * x0i + bu[:, :LANES]
            ni = ar * x0i + ai * x0r + bu[:, LANES:]
            sr_ref[:, pc] = nr
            si_ref[:, pc] = ni
            x = jnp.concatenate([nr, ni], axis=1).astype(BF16)
            acc = acc + lax.dot_general(x, cw_ref[j], (((1,), (1,)), ((), ())),
                                        preferred_element_type=F32)
        y_scr[:, cols] = acc
    o_ref[...] = _s5_output_gate(y_scr[...], z_ref[...], wglu_ref, bglu_ref).astype(BF16)


def _s5_sample(proj, x0r, x0i, wb, cw, ar, ai, d, wglu, bglu):
    nb = proj.shape[0]
    ub_col = (4 * RET_WIDTH) // S5_WIDTH
    full = lambda a: pl.BlockSpec(a.shape, lambda i: (0,) * a.ndim)
    return pl.pallas_call(
        _s5_step_kernel,
        grid=(1,),
        in_specs=[
            pl.BlockSpec((nb, S5_WIDTH), lambda i: (0, ub_col)),
            pl.BlockSpec((nb, S5_WIDTH), lambda i: (0, ub_col + 1)),
            full(x0r), full(x0i), full(wb), full(cw), full(ar), full(ai), full(d), full(wglu), full(bglu),
        ],
        out_specs=[
            pl.BlockSpec((nb, S5_WIDTH), lambda i: (0, 0)),
            full(x0r), full(x0i),
        ],
        out_shape=[
            jax.ShapeDtypeStruct((nb, S5_WIDTH), BF16),
            jax.ShapeDtypeStruct(x0r.shape, F32),
            jax.ShapeDtypeStruct(x0i.shape, F32),
        ],
        scratch_shapes=[pltpu.VMEM((nb, S5_WIDTH), F32)],
        compiler_params=_params("arbitrary"),
        name="s5_sample",
    )(proj, proj, x0r, x0i, wb, cw, ar, ai, d, wglu, bglu)


def _merge_kernel(ya_ref, yb_ref, ga_ref, gb_ref, x_ref, wpa_ref, wpb_ref, wout_ref, gpost_ref, o_ref):
    ya = jnp.dot(ya_ref[...].astype(BF16), wpa_ref[...], preferred_element_type=F32)
    yb = jnp.dot(yb_ref[...].astype(BF16), wpb_ref[...], preferred_element_type=F32)
    merged = (jax.nn.sigmoid(ga_ref[...].astype(F32)) * ya
              + jax.nn.sigmoid(gb_ref[...].astype(F32)) * yb)
    out = jnp.dot(merged.astype(BF16), wout_ref[...], preferred_element_type=F32)
    ms = jnp.mean(out * out, axis=-1, keepdims=True)
    o_ref[...] = x_ref[...] + out * lax.rsqrt(ms + EPS) * gpost_ref[...]


def _merge(ya, yb, proj, x2d, wpa, wpb, wout, gpost, *, tm):
    m = x2d.shape[0]
    ga_col = (4 * RET_WIDTH + 2 * S5_WIDTH) // D_MODEL
    return pl.pallas_call(
        _merge_kernel,
        grid=(m // tm,),
        in_specs=[
            pl.BlockSpec((tm, RET_WIDTH), lambda i: (i, 0)),
            pl.BlockSpec((tm, S5_WIDTH), lambda i: (i, 0)),
            pl.BlockSpec((tm, D_MODEL), lambda i: (i, ga_col)),
            pl.BlockSpec((tm, D_MODEL), lambda i: (i, ga_col + 1)),
            pl.BlockSpec((tm, D_MODEL), lambda i: (i, 0)),
            _resident(wpa.shape), _resident(wpb.shape), _resident(wout.shape), _resident(gpost.shape),
        ],
        out_specs=pl.BlockSpec((tm, D_MODEL), lambda i: (i, 0)),
        out_shape=jax.ShapeDtypeStruct((m, D_MODEL), F32),
        compiler_params=_params("arbitrary"),
        name="merge_out",
    )(ya, yb, proj, proj, x2d, wpa, wpb, wout, gpost)


def _rope_tables(pos):
    half = HEAD_DIM // 2
    inv = ROPE_BASE ** (-jnp.arange(half, dtype=F32) / half)
    ang = pos.astype(F32)[:, None] * inv[None, :]
    cos, sin = jnp.cos(ang), jnp.sin(ang)
    return jnp.concatenate([cos, cos], axis=-1), jnp.concatenate([-sin, sin], axis=-1)


def _s5_discretize(lam_re, lam_im, log_dt, b_re, b_im):
    dt = jnp.exp(log_dt)[:, None]
    mag = jnp.exp(lam_re * dt)
    abar_r = mag * jnp.cos(lam_im * dt)
    abar_i = mag * jnp.sin(lam_im * dt)
    nr, ni = abar_r - 1.0, abar_i
    den = lam_re * lam_re + lam_im * lam_im
    coef_r = (nr * lam_re + ni * lam_im) / den
    coef_i = (ni * lam_re - nr * lam_im) / den
    bbar_r = coef_r[:, :, None] * b_re - coef_i[:, :, None] * b_im
    bbar_i = coef_r[:, :, None] * b_im + coef_i[:, :, None] * b_re
    return abar_r, abar_i, bbar_r, bbar_i


def _s5_pair_weights(bbar_r, bbar_i, c_re, c_im):
    j = lax.broadcasted_iota(jnp.int32, (S5_PAIRS, LANES, 2 * LANES), 0)
    row = lax.broadcasted_iota(jnp.int32, (S5_PAIRS, LANES, 2 * LANES), 1)
    col = lax.broadcasted_iota(jnp.int32, (S5_PAIRS, LANES, 2 * LANES), 2)
    pair_lanes = 2 * S5_GROUP
    keep = ((row // pair_lanes == j % PAIRS_PER_BLOCK)
            & ((row // S5_GROUP) % 2 == (col // S5_P) % 2))

    def expand(re, im, perm):
        base = jnp.stack([re, im]).reshape((2, S5_PAIRS, 2) + re.shape[1:]).transpose(perm)
        base = base.reshape(S5_PAIRS, 1, S5_GROUP, 2 * LANES)
        tiled = jnp.broadcast_to(base, (S5_PAIRS, LANES // S5_GROUP, S5_GROUP, 2 * LANES))
        return jnp.where(keep, tiled.reshape(S5_PAIRS, LANES, 2 * LANES), 0.0).astype(BF16)

    wb = expand(bbar_r, bbar_i, (1, 4, 0, 2, 3))
    cw = expand(c_re, -c_im, (1, 3, 0, 2, 4))
    return wb, cw


def kernel(x_prompt, x_sample, state_ret, state_s5_re, state_s5_im, g_pre, w_in, w_pa, w_pb, w_out, g_post,
           s5_lam_re, s5_lam_im, s5_log_dt, s5_b_re, s5_b_im, s5_c_re, s5_c_im, s5_d, s5_w_glu, s5_b_glu):
    assert w_in.shape[0] == 1, "single trunk layer"
    bp, lp, _ = x_prompt.shape
    bs, ls, _ = x_sample.shape
    assert ls == 1 and lp % RET_CHUNK == 0

    w_pa_b = w_pa[0].astype(BF16)
    w_pb_b = w_pb[0].astype(BF16)
    w_out_b = w_out[0].astype(BF16)
    w_glu_b = s5_w_glu[0].astype(BF16)
    b_glu = s5_b_glu[0].reshape(1, S5_WIDTH)
    gpre = g_pre[0].reshape(1, D_MODEL)
    gpost = g_post[0].reshape(1, D_MODEL)

    abar_r, abar_i, bbar_r, bbar_i = _s5_discretize(
        s5_lam_re[0], s5_lam_im[0], s5_log_dt[0], s5_b_re[0], s5_b_im[0])
    wb, cw = _s5_pair_weights(bbar_r, bbar_i, s5_c_re[0], s5_c_im[0])
    ar = abar_r.reshape(S5_PAIRS, LANES)
    ai = abar_i.reshape(S5_PAIRS, LANES)
    d_skip = s5_d[0].reshape(1, S5_WIDTH)

    xs = x_sample.reshape(bs, D_MODEL)
    cos_s, sin_s = _rope_tables(jnp.full((bs,), PAST_LEN, jnp.int32))
    proj_s, w_in_b = _inproj(xs, gpre, w_in[0], cos_s, sin_s, tm=bs, tn=1024)

    xp = x_prompt.reshape(bp * lp, D_MODEL)
    cos_p, sin_p = _rope_tables(jnp.arange(lp, dtype=jnp.int32))
    (proj_p,) = _inproj(xp, gpre, w_in_b, cos_p, sin_p, tm=1024, tn=1024)
    ya_p, ret_p = _retention_prompt(proj_p, batch=bp, seq=lp, tm=256)
    yb_p, s5r_p, s5i_p = _s5_prompt(proj_p, wb, cw, ar, ai, d_skip, w_glu_b, b_glu, batch=bp, seq=lp, tm=256,
                                    sub=128)
    y_p = _merge(ya_p, yb_p, proj_p, xp, w_pa_b, w_pb_b, w_out_b, gpost, tm=256)

    ya_s, ret_s = _retention_sample(proj_s, state_ret[0], bb=16)
    x0r = state_s5_re[0].reshape(bs, S5_GROUPS * S5_P)
    x0i = state_s5_im[0].reshape(bs, S5_GROUPS * S5_P)
    yb_s, s5r_s, s5i_s = _s5_sample(proj_s, x0r, x0i, wb, cw, ar, ai, d_skip, w_glu_b, b_glu)
    y_s = _merge(ya_s, yb_s, proj_s, xs, w_pa_b, w_pb_b, w_out_b, gpost, tm=bs)

    state_shape = (1, -1, S5_GROUPS, S5_P)
    return (y_p.reshape(bp, lp, D_MODEL), y_s.reshape(bs, 1, D_MODEL),
            ret_p[None], s5r_p.reshape(state_shape), s5i_p.reshape(state_shape),
            ret_s[None], s5r_s.reshape(state_shape), s5i_s.reshape(state_shape))
```

```python
import functools
import math

import jax
import jax.numpy as jnp
import numpy as np
from jax import lax
from jax.experimental import pallas as pl
from jax.experimental.pallas import tpu as pltpu

F32 = jnp.float32
BF16 = jnp.bfloat16

D_MODEL = 2048
RET_WIDTH = 1024
RET_HEADS = 8
HEAD_DIM = RET_WIDTH // RET_HEADS
RET_CHUNK = 128
ROPE_BASE = 10000.0
S5_WIDTH = 1024
S5_GROUP = 16
S5_GROUPS = S5_WIDTH // S5_GROUP
S5_P = 64
PAST_LEN = 16384
EPS = 1e-6
IN_COLS = 4 * RET_WIDTH + 2 * S5_WIDTH + 2 * D_MODEL

LANES = 128
MXU_DIM = 256
S5_PAIRS = S5_GROUPS // 2
PAIRS_PER_BLOCK = LANES // (2 * S5_GROUP)
S5_BLOCKS = S5_WIDTH // LANES
S5_PITCH = 36
VMEM_LIMIT = 56 * 1024 * 1024

_LOG_GAMMA = [float(np.log(np.float32(1.0) - np.float32(2.0) ** np.float32(-5.0 - h)))
              for h in range(RET_HEADS)]


def _params(*sem):
    return pltpu.CompilerParams(dimension_semantics=sem, vmem_limit_bytes=VMEM_LIMIT)


def _resident(shape):
    nd = len(shape)
    return pl.BlockSpec(shape, lambda *_: (0,) * nd, pipeline_mode=pl.Buffered(1))


def _rope(x, cos, sin_signed):
    return x * cos + pltpu.roll(x, HEAD_DIM // 2, 1) * sin_signed


def _inproj_kernel(x_ref, g_ref, w_ref, cos_ref, sin_ref, o_ref, *rest, slab, tn):
    h_ref = rest[-1]
    wcopy_ref = rest[0] if len(rest) == 2 else None
    j = pl.program_id(1)

    @pl.when(j == 0)
    def _():
        g = g_ref[...]

        def body(r, _):
            rows = pl.ds(pl.multiple_of(r * slab, slab), slab)
            x = x_ref[rows, :]
            ms = jnp.mean(x * x, axis=-1, keepdims=True)
            h_ref[rows, :] = (x * lax.rsqrt(ms + EPS) * g).astype(BF16)
            return 0

        lax.fori_loop(0, x_ref.shape[0] // slab, body, 0)

    assert (2 * RET_WIDTH) % tn == 0
    v_tile0 = 2 * RET_WIDTH // tn

    def slabs(epilogue):
        for s in range(tn // MXU_DIM):
            cols = slice(s * MXU_DIM, (s + 1) * MXU_DIM)
            w = w_ref[:, cols]
            if wcopy_ref is not None:
                w = w.astype(BF16)
                wcopy_ref[:, cols] = w
            acc = jnp.dot(h_ref[...], w, preferred_element_type=F32)
            o_ref[:, cols] = epilogue(s, acc).astype(BF16)

    def rotate(s, acc):
        is_k = j * tn + s * MXU_DIM >= RET_WIDTH
        scale = jnp.where(is_k, HEAD_DIM ** -0.5, 1.0).astype(F32)
        cos = cos_ref[...] * scale
        sin = sin_ref[...] * scale
        return jnp.concatenate([_rope(acc[:, hh * HEAD_DIM:(hh + 1) * HEAD_DIM], cos, sin)
                                for hh in range(MXU_DIM // HEAD_DIM)], axis=1)

    @pl.when(j < v_tile0)
    def _():
        slabs(rotate)

    @pl.when(j >= v_tile0)
    def _():
        slabs(lambda s, acc: acc)


def _inproj(x2d, g_pre, w_in, cos, sin, *, tm, tn):
    m, d = x2d.shape
    n = w_in.shape[1]
    pos_tiles = cos.shape[0] // tm
    emit_weights = w_in.dtype != BF16
    assert not emit_weights or m == tm, "each weight tile must be visited once to be copied out"
    out_specs = [pl.BlockSpec((tm, tn), lambda i, j: (i, j))]
    out_shape = [jax.ShapeDtypeStruct((m, n), BF16)]
    if emit_weights:
        out_specs.append(pl.BlockSpec((d, tn), lambda i, j: (0, j)))
        out_shape.append(jax.ShapeDtypeStruct((d, n), BF16))
    return pl.pallas_call(
        functools.partial(_inproj_kernel, slab=min(tm, 128), tn=tn),
        grid=(m // tm, n // tn),
        in_specs=[
            pl.BlockSpec((tm, d), lambda i, j: (i, 0)),
            pl.BlockSpec((1, d), lambda i, j: (0, 0)),
            pl.BlockSpec((d, tn), lambda i, j: (0, j)),
            pl.BlockSpec((tm, HEAD_DIM), lambda i, j: (i % pos_tiles, 0)),
            pl.BlockSpec((tm, HEAD_DIM), lambda i, j: (i % pos_tiles, 0)),
        ],
        out_specs=out_specs,
        out_shape=out_shape,
        scratch_shapes=[pltpu.VMEM((tm, d), BF16)],
        compiler_params=_params("arbitrary", "arbitrary"),
        name="inproj",
    )(x2d, g_pre, w_in, cos, sin)


def _group_norm(o):
    mu = jnp.mean(o, axis=-1, keepdims=True)
    d = o - mu
    var = jnp.mean(d * d, axis=-1, keepdims=True)
    return d * lax.rsqrt(var + EPS)


def _retention_kernel(q_ref, k_ref, v_ref, z_ref, o_ref, s_ref, *, n_chunks):
    c = RET_CHUNK

    @pl.when(pl.program_id(1) == 0)
    def _():
        s_ref[...] = jnp.zeros_like(s_ref)

    row = lax.broadcasted_iota(jnp.int32, (c, c), 0).astype(F32)
    col = lax.broadcasted_iota(jnp.int32, (c, c), 1).astype(F32)
    diff = row - col
    for h in range(RET_HEADS):
        lg = _LOG_GAMMA[h]
        mask = jnp.where(diff >= 0, jnp.exp(jnp.maximum(diff, 0.0) * lg), 0.0)
        q_decay = jnp.exp((row + 1.0) * lg)
        k_decay = jnp.exp((c - 1.0 - row) * lg)
        chunk_decay = math.exp(c * lg)
        cols = slice(h * HEAD_DIM, (h + 1) * HEAD_DIM)
        for ci in range(n_chunks):
            rows = slice(ci * c, (ci + 1) * c)
            q = q_ref[rows, cols]
            k = k_ref[rows, cols]
            v = v_ref[rows, cols]
            s0 = s_ref[0, h]
            scores = lax.dot_general(q, k, (((1,), (1,)), ((), ())),
                                     preferred_element_type=F32) * mask
            inner = jnp.dot(scores.astype(BF16), v, preferred_element_type=F32)
            cross = jnp.dot(q, s0.astype(BF16), preferred_element_type=F32) * q_decay
            kd = (k.astype(F32) * k_decay).astype(BF16)
            s_ref[0, h] = chunk_decay * s0 + lax.dot_general(
                kd, v, (((0,), (0,)), ((), ())), preferred_element_type=F32)
            o = _group_norm(inner + cross)
            o_ref[rows, cols] = (o * jax.nn.silu(z_ref[rows, cols].astype(F32))).astype(BF16)


def _retention_prompt(proj, *, batch, seq, tm):
    nt = seq // tm
    row_block = lambda b, i: b * nt + i
    col_spec = lambda cb: pl.BlockSpec((tm, RET_WIDTH), lambda b, i: (row_block(b, i), cb))
    return pl.pallas_call(
        functools.partial(_retention_kernel, n_chunks=tm // RET_CHUNK),
        grid=(batch, nt),
        in_specs=[col_spec(0), col_spec(1), col_spec(2), col_spec(3)],
        out_specs=[
            pl.BlockSpec((tm, RET_WIDTH), lambda b, i: (row_block(b, i), 0)),
            pl.BlockSpec((1, RET_HEADS, HEAD_DIM, HEAD_DIM), lambda b, i: (b, 0, 0, 0)),
        ],
        out_shape=[
            jax.ShapeDtypeStruct((batch * seq, RET_WIDTH), BF16),
            jax.ShapeDtypeStruct((batch, RET_HEADS, HEAD_DIM, HEAD_DIM), F32),
        ],
        compiler_params=_params("arbitrary", "arbitrary"),
        name="retention_prompt",
    )(proj, proj, proj, proj)


def _retention_step_kernel(q_ref, k_ref, v_ref, z_ref, s_ref, o_ref, sn_ref, acc_ref, *, bb):
    pad = jnp.zeros((HEAD_DIM - bb, HEAD_DIM), F32)
    for h in range(RET_HEADS):
        gamma = math.exp(_LOG_GAMMA[h])
        cols = slice(h * HEAD_DIM, (h + 1) * HEAD_DIM)
        q = q_ref[:, cols].astype(F32)
        k = k_ref[:, cols].astype(F32)
        v = v_ref[:, cols].astype(F32)
        qt = jnp.concatenate([q, pad], axis=0).T
        kt = jnp.concatenate([k, pad], axis=0).T
        for b in range(bb):
            s_new = gamma * s_ref[b, h] + kt[:, b:b + 1] * v[b:b + 1, :]
            sn_ref[b, h] = s_new
            acc_ref[b:b + 1, cols] = jnp.sum(qt[:, b:b + 1] * s_new, axis=0, keepdims=True)
    for h in range(RET_HEADS):
        cols = slice(h * HEAD_DIM, (h + 1) * HEAD_DIM)
        o_ref[:, cols] = _group_norm(acc_ref[:, cols]) * jax.nn.silu(z_ref[:, cols].astype(F32))


def _retention_sample(proj, state, *, bb):
    nb = state.shape[0]
    col_spec = lambda cb: pl.BlockSpec((bb, RET_WIDTH), lambda i: (i, cb))
    state_spec = pl.BlockSpec((bb, RET_HEADS, HEAD_DIM, HEAD_DIM), lambda i: (i, 0, 0, 0))
    return pl.pallas_call(
        functools.partial(_retention_step_kernel, bb=bb),
        grid=(nb // bb,),
        in_specs=[col_spec(0), col_spec(1), col_spec(2), col_spec(3), state_spec],
        out_specs=[pl.BlockSpec((bb, RET_WIDTH), lambda i: (i, 0)), state_spec],
        out_shape=[
            jax.ShapeDtypeStruct((nb, RET_WIDTH), F32),
            jax.ShapeDtypeStruct(state.shape, F32),
        ],
        scratch_shapes=[pltpu.VMEM((bb, RET_WIDTH), F32)],
        compiler_params=_params("arbitrary"),
        name="retention_sample",
    )(proj, proj, proj, proj, state)


def _s5_output_gate(y, z, wglu_ref, bglu_ref):
    y = jax.nn.gelu(y)
    g = jnp.dot(y.astype(BF16), wglu_ref[...], preferred_element_type=F32) + bglu_ref[...]
    return y * jax.nn.sigmoid(g) * jax.nn.silu(z.astype(F32))


def _s5_prompt_kernel(u_ref, z_ref, wb_ref, cw_ref, ar_ref, ai_ref, d_ref, wglu_ref, bglu_ref,
                      o_ref, sr_ref, si_ref, y_scr, *x_scr, sub):
    n_sub = len(x_scr) // 2
    xr_scr, xi_scr = x_scr[:n_sub], x_scr[n_sub:]
    strided = lambda j: pl.ds(j, sub, stride=S5_PITCH)

    @pl.when(pl.program_id(1) == 0)
    def _():
        sr_ref[...] = jnp.zeros_like(sr_ref)
        si_ref[...] = jnp.zeros_like(si_ref)

    ar = ar_ref[...]
    ai = ai_ref[...]
    xr, xi = sr_ref[0], si_ref[0]
    for k in range(n_sub):
        rows = slice(k * sub, (k + 1) * sub)
        for blk in range(S5_BLOCKS):
            ub = u_ref[rows, blk * LANES:(blk + 1) * LANES]
            for j in range(blk * PAIRS_PER_BLOCK, (blk + 1) * PAIRS_PER_BLOCK):
                bu = jnp.dot(ub, wb_ref[j], preferred_element_type=F32)
                xr_scr[k][strided(j), :] = bu[:, :LANES]
                xi_scr[k][strided(j), :] = bu[:, LANES:]
        for t in range(sub):
            step = slice(t * S5_PITCH, t * S5_PITCH + S5_PAIRS)
            xr, xi = (ar * xr - ai * xi + xr_scr[k][step, :],
                      ar * xi + ai * xr + xi_scr[k][step, :])
            xr_scr[k][step, :] = xr
            xi_scr[k][step, :] = xi
        for blk in range(S5_BLOCKS):
            cols = slice(blk * LANES, (blk + 1) * LANES)
            acc = d_ref[:, cols] * u_ref[rows, cols].astype(F32)
            for j in range(blk * PAIRS_PER_BLOCK, (blk + 1) * PAIRS_PER_BLOCK):
                x = jnp.concatenate([xr_scr[k][strided(j), :], xi_scr[k][strided(j), :]], axis=1)
                acc = acc + lax.dot_general(x.astype(BF16), cw_ref[j], (((1,), (1,)), ((), ())),
                                            preferred_element_type=F32)
            y_scr[rows, cols] = acc
        o_ref[rows, :] = _s5_output_gate(
            y_scr[rows, :], z_ref[rows, :], wglu_ref, bglu_ref).astype(BF16)
    sr_ref[0] = xr
    si_ref[0] = xi


def _s5_prompt(proj, wb, cw, ar, ai, d, wglu, bglu, *, batch, seq, tm, sub):
    nt = seq // tm
    row_block = lambda b, i: b * nt + i
    ub_col = (4 * RET_WIDTH) // S5_WIDTH
    state_spec = pl.BlockSpec((1, S5_PAIRS, LANES), lambda b, i: (b, 0, 0))
    return pl.pallas_call(
        functools.partial(_s5_prompt_kernel, sub=sub),
        grid=(batch, nt),
        in_specs=[
            pl.BlockSpec((tm, S5_WIDTH), lambda b, i: (row_block(b, i), ub_col)),
            pl.BlockSpec((tm, S5_WIDTH), lambda b, i: (row_block(b, i), ub_col + 1)),
            _resident(wb.shape), _resident(cw.shape), _resident(ar.shape), _resident(ai.shape),
            _resident(d.shape), _resident(wglu.shape), _resident(bglu.shape),
        ],
        out_specs=[
            pl.BlockSpec((tm, S5_WIDTH), lambda b, i: (row_block(b, i), 0)),
            state_spec, state_spec,
        ],
        out_shape=[
            jax.ShapeDtypeStruct((batch * seq, S5_WIDTH), BF16),
            jax.ShapeDtypeStruct((batch, S5_PAIRS, LANES), F32),
            jax.ShapeDtypeStruct((batch, S5_PAIRS, LANES), F32),
        ],
        scratch_shapes=[pltpu.VMEM((tm, S5_WIDTH), F32)]
        + [pltpu.VMEM((sub * S5_PITCH, LANES), F32)] * (2 * (tm // sub)),
        compiler_params=_params("arbitrary", "arbitrary"),
        name="s5_prompt",
    )(proj, proj, wb, cw, ar, ai, d, wglu, bglu)


def _s5_step_kernel(u_ref, z_ref, x0r_ref, x0i_ref, wb_ref, cw_ref, ar_ref, ai_ref, d_ref,
                    wglu_ref, bglu_ref, o_ref, sr_ref, si_ref, y_scr):
    for blk in range(S5_BLOCKS):
        cols = slice(blk * LANES, (blk + 1) * LANES)
        ub = u_ref[:, cols]
        acc = d_ref[:, cols] * ub.astype(F32)
        for j in range(blk * PAIRS_PER_BLOCK, (blk + 1) * PAIRS_PER_BLOCK):
            pc = slice(j * LANES, (j + 1) * LANES)
            bu = jnp.dot(ub, wb_ref[j], preferred_element_type=F32)
            ar = ar_ref[j:j + 1, :]
            ai = ai_ref[j:j + 1, :]
            x0r = x0r_ref[:, pc]
            x0i = x0i_ref[:, pc]
            nr = ar * x0r - ai * x0i + bu[:, :LANES]
            ni = ar * x0i + ai * x0r + bu[:, LANES:]
            sr_ref[:, pc] = nr
            si_ref[:, pc] = ni
            x = jnp.concatenate([nr, ni], axis=1).astype(BF16)
            acc = acc + lax.dot_general(x, cw_ref[j], (((1,), (1,)), ((), ())),
                                        preferred_element_type=F32)
        y_scr[:, cols] = acc
    o_ref[...] = _s5_output_gate(y_scr[...], z_ref[...], wglu_ref, bglu_ref).astype(BF16)


def _s5_sample(proj, x0r, x0i, wb, cw, ar, ai, d, wglu, bglu):
    nb = proj.shape[0]
    ub_col = (4 * RET_WIDTH) // S5_WIDTH
    full = lambda a: pl.BlockSpec(a.shape, lambda i: (0,) * a.ndim)
    return pl.pallas_call(
        _s5_step_kernel,
        grid=(1,),
        in_specs=[
            pl.BlockSpec((nb, S5_WIDTH), lambda i: (0, ub_col)),
            pl.BlockSpec((nb, S5_WIDTH), lambda i: (0, ub_col + 1)),
            full(x0r), full(x0i), full(wb), full(cw), full(ar), full(ai), full(d), full(wglu), full(bglu),
        ],
        out_specs=[
            pl.BlockSpec((nb, S5_WIDTH), lambda i: (0, 0)),
            full(x0r), full(x0i),
        ],
        out_shape=[
            jax.ShapeDtypeStruct((nb, S5_WIDTH), BF16),
            jax.ShapeDtypeStruct(x0r.shape, F32),
            jax.ShapeDtypeStruct(x0i.shape, F32),
        ],
        scratch_shapes=[pltpu.VMEM((nb, S5_WIDTH), F32)],
        compiler_params=_params("arbitrary"),
        name="s5_sample",
    )(proj, proj, x0r, x0i, wb, cw, ar, ai, d, wglu, bglu)


def _merge_kernel(ya_ref, yb_ref, ga_ref, gb_ref, x_ref, wpa_ref, wpb_ref, wout_ref, gpost_ref, o_ref,
                  merged_scr):
    slabs = [slice(s * MXU_DIM, (s + 1) * MXU_DIM) for s in range(D_MODEL // MXU_DIM)]
    ya_in = ya_ref[...].astype(BF16)
    yb_in = yb_ref[...].astype(BF16)
    for cols in slabs:
        ya = jnp.dot(ya_in, wpa_ref[:, cols], preferred_element_type=F32)
        yb = jnp.dot(yb_in, wpb_ref[:, cols], preferred_element_type=F32)
        merged_scr[:, cols] = (jax.nn.sigmoid(ga_ref[:, cols].astype(F32)) * ya
                               + jax.nn.sigmoid(gb_ref[:, cols].astype(F32)) * yb).astype(BF16)
    sq = jnp.zeros((o_ref.shape[0], 1), F32)
    for cols in slabs:
        out = jnp.dot(merged_scr[...], wout_ref[:, cols], preferred_element_type=F32)
        sq = sq + jnp.sum(out * out, axis=-1, keepdims=True)
        o_ref[:, cols] = out
    inv_rms = lax.rsqrt(sq * (1.0 / D_MODEL) + EPS)
    for cols in slabs:
        o_ref[:, cols] = x_ref[:, cols] + o_ref[:, cols] * inv_rms * gpost_ref[:, cols]


def _merge(ya, yb, proj, x2d, wpa, wpb, wout, gpost, *, tm):
    m = x2d.shape[0]
    ga_col = (4 * RET_WIDTH + 2 * S5_WIDTH) // D_MODEL
    return pl.pallas_call(
        _merge_kernel,
        grid=(m // tm,),
        in_specs=[
            pl.BlockSpec((tm, RET_WIDTH), lambda i: (i, 0)),
            pl.BlockSpec((tm, S5_WIDTH), lambda i: (i, 0)),
            pl.BlockSpec((tm, D_MODEL), lambda i: (i, ga_col)),
            pl.BlockSpec((tm, D_MODEL), lambda i: (i, ga_col + 1)),
            pl.BlockSpec((tm, D_MODEL), lambda i: (i, 0)),
            _resident(wpa.shape), _resident(wpb.shape), _resident(wout.shape), _resident(gpost.shape),
        ],
        out_specs=pl.BlockSpec((tm, D_MODEL), lambda i: (i, 0)),
        out_shape=jax.ShapeDtypeStruct((m, D_MODEL), F32),
        scratch_shapes=[pltpu.VMEM((tm, D_MODEL), BF16)],
        compiler_params=_params("arbitrary"),
        name="merge_out",
    )(ya, yb, proj, proj, x2d, wpa, wpb, wout, gpost)


def _rope_tables(pos):
    half = HEAD_DIM // 2
    inv = ROPE_BASE ** (-jnp.arange(half, dtype=F32) / half)
    ang = pos.astype(F32)[:, None] * inv[None, :]
    cos, sin = jnp.cos(ang), jnp.sin(ang)
    return jnp.concatenate([cos, cos], axis=-1), jnp.concatenate([-sin, sin], axis=-1)


def _s5_discretize(lam_re, lam_im, log_dt, b_re, b_im):
    dt = jnp.exp(log_dt)[:, None]
    mag = jnp.exp(lam_re * dt)
    abar_r = mag * jnp.cos(lam_im * dt)
    abar_i = mag * jnp.sin(lam_im * dt)
    nr, ni = abar_r - 1.0, abar_i
    den = lam_re * lam_re + lam_im * lam_im
    coef_r = (nr * lam_re + ni * lam_im) / den
    coef_i = (ni * lam_re - nr * lam_im) / den
    bbar_r = coef_r[:, :, None] * b_re - coef_i[:, :, None] * b_im
    bbar_i = coef_r[:, :, None] * b_im + coef_i[:, :, None] * b_re
    return abar_r, abar_i, bbar_r, bbar_i


def _s5_pair_weights(bbar_r, bbar_i, c_re, c_im):
    j = lax.broadcasted_iota(jnp.int32, (S5_PAIRS, LANES, 2 * LANES), 0)
    row = lax.broadcasted_iota(jnp.int32, (S5_PAIRS, LANES, 2 * LANES), 1)
    col = lax.broadcasted_iota(jnp.int32, (S5_PAIRS, LANES, 2 * LANES), 2)
    pair_lanes = 2 * S5_GROUP
    keep = ((row // pair_lanes == j % PAIRS_PER_BLOCK)
            & ((row // S5_GROUP) % 2 == (col // S5_P) % 2))

    def expand(re, im, perm):
        base = jnp.stack([re, im]).reshape((2, S5_PAIRS, 2) + re.shape[1:]).transpose(perm)
        base = base.reshape(S5_PAIRS, 1, S5_GROUP, 2 * LANES)
        tiled = jnp.broadcast_to(base, (S5_PAIRS, LANES // S5_GROUP, S5_GROUP, 2 * LANES))
        return jnp.where(keep, tiled.reshape(S5_PAIRS, LANES, 2 * LANES), 0.0).astype(BF16)

    wb = expand(bbar_r, bbar_i, (1, 4, 0, 2, 3))
    cw = expand(c_re, -c_im, (1, 3, 0, 2, 4))
    return wb, cw


def kernel(x_prompt, x_sample, state_ret, state_s5_re, state_s5_im, g_pre, w_in, w_pa, w_pb, w_out, g_post,
           s5_lam_re, s5_lam_im, s5_log_dt, s5_b_re, s5_b_im, s5_c_re, s5_c_im, s5_d, s5_w_glu, s5_b_glu):
    assert w_in.shape[0] == 1, "single trunk layer"
    bp, lp, _ = x_prompt.shape
    bs, ls, _ = x_sample.shape
    assert ls == 1 and lp % RET_CHUNK == 0

    w_pa_b = w_pa[0].astype(BF16)
    w_pb_b = w_pb[0].astype(BF16)
    w_out_b = w_out[0].astype(BF16)
    w_glu_b = s5_w_glu[0].astype(BF16)
    b_glu = s5_b_glu[0].reshape(1, S5_WIDTH)
    gpre = g_pre[0].reshape(1, D_MODEL)
    gpost = g_post[0].reshape(1, D_MODEL)

    abar_r, abar_i, bbar_r, bbar_i = _s5_discretize(
        s5_lam_re[0], s5_lam_im[0], s5_log_dt[0], s5_b_re[0], s5_b_im[0])
    wb, cw = _s5_pair_weights(bbar_r, bbar_i, s5_c_re[0], s5_c_im[0])
    ar = abar_r.reshape(S5_PAIRS, LANES)
    ai = abar_i.reshape(S5_PAIRS, LANES)
    d_skip = s5_d[0].reshape(1, S5_WIDTH)

    xs = x_sample.reshape(bs, D_MODEL)
    cos_s, sin_s = _rope_tables(jnp.full((bs,), PAST_LEN, jnp.int32))
    proj_s, w_in_b = _inproj(xs, gpre, w_in[0], cos_s, sin_s, tm=bs, tn=1024)

    xp = x_prompt.reshape(bp * lp, D_MODEL)
    cos_p, sin_p = _rope_tables(jnp.arange(lp, dtype=jnp.int32))
    (proj_p,) = _inproj(xp, gpre, w_in_b, cos_p, sin_p, tm=1024, tn=2048)
    ya_p, ret_p = _retention_prompt(proj_p, batch=bp, seq=lp, tm=256)
    yb_p, s5r_p, s5i_p = _s5_prompt(proj_p, wb, cw, ar, ai, d_skip, w_glu_b, b_glu, batch=bp, seq=lp, tm=256,
                                    sub=128)
    y_p = _merge(ya_p, yb_p, proj_p, xp, w_pa_b, w_pb_b, w_out_b, gpost, tm=512)

    ya_s, ret_s = _retention_sample(proj_s, state_ret[0], bb=16)
    x0r = state_s5_re[0].reshape(bs, S5_GROUPS * S5_P)
    x0i = state_s5_im[0].reshape(bs, S5_GROUPS * S5_P)
    yb_s, s5r_s, s5i_s = _s5_sample(proj_s, x0r, x0i, wb, cw, ar, ai, d_skip, w_glu_b, b_glu)
    y_s = _merge(ya_s, yb_s, proj_s, xs, w_pa_b, w_pb_b, w_out_b, gpost, tm=bs)

    state_shape = (1, -1, S5_GROUPS, S5_P)
    return (y_p.reshape(bp, lp, D_MODEL), y_s.reshape(bs, 1, D_MODEL),
            ret_p[None], s5r_p.reshape(state_shape), s5i_p.reshape(state_shape),
            ret_s[None], s5r_s.reshape(state_shape), s5i_s.reshape(state_shape))
```

```python
import functools
import math

import jax
import jax.numpy as jnp
import numpy as np
from jax import lax
from jax.experimental import pallas as pl
from jax.experimental.pallas import tpu as pltpu

F32 = jnp.float32
BF16 = jnp.bfloat16

D_MODEL = 2048
RET_WIDTH = 1024
RET_HEADS = 8
HEAD_DIM = RET_WIDTH // RET_HEADS
RET_CHUNK = 128
ROPE_BASE = 10000.0
S5_WIDTH = 1024
S5_GROUP = 16
S5_GROUPS = S5_WIDTH // S5_GROUP
S5_P = 64
PAST_LEN = 16384
EPS = 1e-6
IN_COLS = 4 * RET_WIDTH + 2 * S5_WIDTH + 2 * D_MODEL

LANES = 128
MXU_DIM = 256
S5_PAIRS = S5_GROUPS // 2
PAIRS_PER_BLOCK = LANES // (2 * S5_GROUP)
S5_BLOCKS = S5_WIDTH // LANES
S5_PITCH = 36
VMEM_LIMIT = 56 * 1024 * 1024

_LOG_GAMMA = [float(np.log(np.float32(1.0) - np.float32(2.0) ** np.float32(-5.0 - h)))
              for h in range(RET_HEADS)]


def _params(*sem):
    return pltpu.CompilerParams(dimension_semantics=sem, vmem_limit_bytes=VMEM_LIMIT)


def _resident(shape):
    nd = len(shape)
    return pl.BlockSpec(shape, lambda *_: (0,) * nd, pipeline_mode=pl.Buffered(1))


def _rope(x, cos, sin_signed):
    return x * cos + pltpu.roll(x, HEAD_DIM // 2, 1) * sin_signed


def _inproj_kernel(x_ref, g_ref, w_ref, cos_ref, sin_ref, o_ref, *rest, slab, tn):
    h_ref = rest[-1]
    wcopy_ref = rest[0] if len(rest) == 2 else None
    j = pl.program_id(1)

    @pl.when(j == 0)
    def _():
        g = g_ref[...]

        def body(r, _):
            rows = pl.ds(pl.multiple_of(r * slab, slab), slab)
            x = x_ref[rows, :]
            ms = jnp.mean(x * x, axis=-1, keepdims=True)
            h_ref[rows, :] = (x * lax.rsqrt(ms + EPS) * g).astype(BF16)
            return 0

        lax.fori_loop(0, x_ref.shape[0] // slab, body, 0)

    def rotate(s, acc):
        is_k = j * tn + s * MXU_DIM >= RET_WIDTH
        scale = jnp.where(is_k, HEAD_DIM ** -0.5, 1.0).astype(F32)
        cos = cos_ref[...] * scale
        sin = sin_ref[...] * scale
        return jnp.concatenate([_rope(acc[:, hh * HEAD_DIM:(hh + 1) * HEAD_DIM], cos, sin)
                                for hh in range(MXU_DIM // HEAD_DIM)], axis=1)

    epilogues = {"rope": rotate, "plain": lambda s, acc: acc}
    col_kinds = (("rope", 2 * RET_WIDTH), ("plain", IN_COLS - 2 * RET_WIDTH))
    slab_kinds = [kind for kind, width in col_kinds for _ in range(width // MXU_DIM)]
    per_tile = tn // MXU_DIM
    tiles_by_pattern = {}
    for tile in range(len(slab_kinds) // per_tile):
        pattern = tuple(slab_kinds[tile * per_tile:(tile + 1) * per_tile])
        tiles_by_pattern.setdefault(pattern, []).append(tile)

    def slabs(pattern):
        for s, kind in enumerate(pattern):
            cols = slice(s * MXU_DIM, (s + 1) * MXU_DIM)
            w = w_ref[:, cols]
            if wcopy_ref is not None:
                w = w.astype(BF16)
                wcopy_ref[:, cols] = w
            acc = jnp.dot(h_ref[...], w, preferred_element_type=F32)
            o_ref[:, cols] = epilogues[kind](s, acc).astype(BF16)

    for pattern, tiles in tiles_by_pattern.items():
        is_this_kind = functools.reduce(jnp.logical_or, [j == tile for tile in tiles])
        pl.when(is_this_kind)(functools.partial(slabs, pattern))


def _inproj(x2d, g_pre, w_in, cos, sin, *, tm, tn):
    m, d = x2d.shape
    n = w_in.shape[1]
    pos_tiles = cos.shape[0] // tm
    emit_weights = w_in.dtype != BF16
    assert not emit_weights or m == tm, "each weight tile must be visited once to be copied out"
    out_specs = [pl.BlockSpec((tm, tn), lambda i, j: (i, j))]
    out_shape = [jax.ShapeDtypeStruct((m, n), BF16)]
    if emit_weights:
        out_specs.append(pl.BlockSpec((d, tn), lambda i, j: (0, j)))
        out_shape.append(jax.ShapeDtypeStruct((d, n), BF16))
    return pl.pallas_call(
        functools.partial(_inproj_kernel, slab=min(tm, 128), tn=tn),
        grid=(m // tm, n // tn),
        in_specs=[
            pl.BlockSpec((tm, d), lambda i, j: (i, 0)),
            pl.BlockSpec((1, d), lambda i, j: (0, 0)),
            pl.BlockSpec((d, tn), lambda i, j: (0, j)),
            pl.BlockSpec((tm, HEAD_DIM), lambda i, j: (i % pos_tiles, 0)),
            pl.BlockSpec((tm, HEAD_DIM), lambda i, j: (i % pos_tiles, 0)),
        ],
        out_specs=out_specs,
        out_shape=out_shape,
        scratch_shapes=[pltpu.VMEM((tm, d), BF16)],
        compiler_params=_params("arbitrary", "arbitrary"),
        name="inproj",
    )(x2d, g_pre, w_in, cos, sin)


def _group_norm(o):
    mu = jnp.mean(o, axis=-1, keepdims=True)
    d = o - mu
    var = jnp.mean(d * d, axis=-1, keepdims=True)
    return d * lax.rsqrt(var + EPS)


def _retention_tile(q_ref, k_ref, v_ref, z_ref, o_ref, s_ref):
    c = RET_CHUNK
    n_chunks = q_ref.shape[0] // c
    row = lax.broadcasted_iota(jnp.int32, (c, c), 0).astype(F32)
    col = lax.broadcasted_iota(jnp.int32, (c, c), 1).astype(F32)
    diff = row - col
    for h in range(RET_HEADS):
        lg = _LOG_GAMMA[h]
        mask = jnp.where(diff >= 0, jnp.exp(jnp.maximum(diff, 0.0) * lg), 0.0)
        q_decay = jnp.exp((row + 1.0) * lg)
        k_decay = jnp.exp((c - 1.0 - row) * lg)
        chunk_decay = math.exp(c * lg)
        cols = slice(h * HEAD_DIM, (h + 1) * HEAD_DIM)
        for ci in range(n_chunks):
            rows = slice(ci * c, (ci + 1) * c)
            q = q_ref[rows, cols]
            k = k_ref[rows, cols]
            v = v_ref[rows, cols]
            s0 = s_ref[0, h]
            scores = lax.dot_general(q, k, (((1,), (1,)), ((), ())),
                                     preferred_element_type=F32) * mask
            inner = jnp.dot(scores.astype(BF16), v, preferred_element_type=F32)
            cross = jnp.dot(q, s0.astype(BF16), preferred_element_type=F32) * q_decay
            kd = (k.astype(F32) * k_decay).astype(BF16)
            s_ref[0, h] = chunk_decay * s0 + lax.dot_general(
                kd, v, (((0,), (0,)), ((), ())), preferred_element_type=F32)
            o = _group_norm(inner + cross)
            o_ref[rows, cols] = (o * jax.nn.silu(z_ref[rows, cols].astype(F32))).astype(BF16)


def _retention_step_kernel(q_ref, k_ref, v_ref, z_ref, s_ref, o_ref, sn_ref, acc_ref, *, bb):
    pad = jnp.zeros((HEAD_DIM - bb, HEAD_DIM), F32)
    for h in range(RET_HEADS):
        gamma = math.exp(_LOG_GAMMA[h])
        cols = slice(h * HEAD_DIM, (h + 1) * HEAD_DIM)
        q = q_ref[:, cols].astype(F32)
        k = k_ref[:, cols].astype(F32)
        v = v_ref[:, cols].astype(F32)
        qt = jnp.concatenate([q, pad], axis=0).T
        kt = jnp.concatenate([k, pad], axis=0).T
        for b in range(bb):
            s_new = gamma * s_ref[b, h] + kt[:, b:b + 1] * v[b:b + 1, :]
            sn_ref[b, h] = s_new
            acc_ref[b:b + 1, cols] = jnp.sum(qt[:, b:b + 1] * s_new, axis=0, keepdims=True)
    for h in range(RET_HEADS):
        cols = slice(h * HEAD_DIM, (h + 1) * HEAD_DIM)
        o_ref[:, cols] = _group_norm(acc_ref[:, cols]) * jax.nn.silu(z_ref[:, cols].astype(F32))


def _retention_sample(proj, state, *, bb):
    nb = state.shape[0]
    col_spec = lambda cb: pl.BlockSpec((bb, RET_WIDTH), lambda i: (i, cb))
    state_spec = pl.BlockSpec((bb, RET_HEADS, HEAD_DIM, HEAD_DIM), lambda i: (i, 0, 0, 0))
    return pl.pallas_call(
        functools.partial(_retention_step_kernel, bb=bb),
        grid=(nb // bb,),
        in_specs=[col_spec(0), col_spec(1), col_spec(2), col_spec(3), state_spec],
        out_specs=[pl.BlockSpec((bb, RET_WIDTH), lambda i: (i, 0)), state_spec],
        out_shape=[
            jax.ShapeDtypeStruct((nb, RET_WIDTH), F32),
            jax.ShapeDtypeStruct(state.shape, F32),
        ],
        scratch_shapes=[pltpu.VMEM((bb, RET_WIDTH), F32)],
        compiler_params=_params("arbitrary"),
        name="retention_sample",
    )(proj, proj, proj, proj, state)


def _s5_output_gate(y, z, wglu_ref, bglu_ref):
    y = jax.nn.gelu(y)
    g = jnp.dot(y.astype(BF16), wglu_ref[...], preferred_element_type=F32) + bglu_ref[...]
    return y * jax.nn.sigmoid(g) * jax.nn.silu(z.astype(F32))


def _s5_tile(u_ref, z_ref, wb_ref, cw_ref, ar_ref, ai_ref, d_ref, wglu_ref, bglu_ref,
             o_ref, sr_ref, si_ref, y_scr, x_scr, *, sub):
    n_sub = len(x_scr) // 2
    xr_scr, xi_scr = x_scr[:n_sub], x_scr[n_sub:]
    strided = lambda j: pl.ds(j, sub, stride=S5_PITCH)
    ar = ar_ref[...]
    ai = ai_ref[...]
    xr, xi = sr_ref[0], si_ref[0]
    for k in range(n_sub):
        rows = slice(k * sub, (k + 1) * sub)
        for blk in range(S5_BLOCKS):
            ub = u_ref[rows, blk * LANES:(blk + 1) * LANES]
            for j in range(blk * PAIRS_PER_BLOCK, (blk + 1) * PAIRS_PER_BLOCK):
                bu = jnp.dot(ub, wb_ref[j], preferred_element_type=F32)
                xr_scr[k][strided(j), :] = bu[:, :LANES]
                xi_scr[k][strided(j), :] = bu[:, LANES:]
        for t in range(sub):
            step = slice(t * S5_PITCH, t * S5_PITCH + S5_PAIRS)
            xr, xi = (ar * xr - ai * xi + xr_scr[k][step, :],
                      ar * xi + ai * xr + xi_scr[k][step, :])
            xr_scr[k][step, :] = xr
            xi_scr[k][step, :] = xi
        for blk in range(S5_BLOCKS):
            cols = slice(blk * LANES, (blk + 1) * LANES)
            acc = d_ref[:, cols] * u_ref[rows, cols].astype(F32)
            for j in range(blk * PAIRS_PER_BLOCK, (blk + 1) * PAIRS_PER_BLOCK):
                x = jnp.concatenate([xr_scr[k][strided(j), :], xi_scr[k][strided(j), :]], axis=1)
                acc = acc + lax.dot_general(x.astype(BF16), cw_ref[j], (((1,), (1,)), ((), ())),
                                            preferred_element_type=F32)
            y_scr[rows, cols] = acc
        o_ref[rows, :] = _s5_output_gate(
            y_scr[rows, :], z_ref[rows, :], wglu_ref, bglu_ref).astype(BF16)
    sr_ref[0] = xr
    si_ref[0] = xi


def _mixers_kernel(q_ref, k_ref, v_ref, za_ref, u_ref, zb_ref, wb_ref, cw_ref, ar_ref, ai_ref, d_ref,
                   wglu_ref, bglu_ref, ya_ref, ret_ref, yb_ref, sr_ref, si_ref, y_scr, *x_scr, sub):
    @pl.when(pl.program_id(1) == 0)
    def _():
        ret_ref[...] = jnp.zeros_like(ret_ref)
        sr_ref[...] = jnp.zeros_like(sr_ref)
        si_ref[...] = jnp.zeros_like(si_ref)

    _retention_tile(q_ref, k_ref, v_ref, za_ref, ya_ref, ret_ref)
    _s5_tile(u_ref, zb_ref, wb_ref, cw_ref, ar_ref, ai_ref, d_ref, wglu_ref, bglu_ref,
             yb_ref, sr_ref, si_ref, y_scr, x_scr, sub=sub)


def _mixers_prompt(proj, wb, cw, ar, ai, d, wglu, bglu, *, batch, seq, tm, sub):
    nt = seq // tm
    row_block = lambda b, i: b * nt + i
    col_spec = lambda cb: pl.BlockSpec((tm, RET_WIDTH), lambda b, i: (row_block(b, i), cb))
    s5_state_spec = pl.BlockSpec((1, S5_PAIRS, LANES), lambda b, i: (b, 0, 0))
    assert RET_WIDTH == S5_WIDTH
    return pl.pallas_call(
        functools.partial(_mixers_kernel, sub=sub),
        grid=(batch, nt),
        in_specs=[
            col_spec(0), col_spec(1), col_spec(2), col_spec(3), col_spec(4), col_spec(5),
            _resident(wb.shape), _resident(cw.shape), _resident(ar.shape), _resident(ai.shape),
            _resident(d.shape), _resident(wglu.shape), _resident(bglu.shape),
        ],
        out_specs=[
            pl.BlockSpec((tm, RET_WIDTH), lambda b, i: (row_block(b, i), 0)),
            pl.BlockSpec((1, RET_HEADS, HEAD_DIM, HEAD_DIM), lambda b, i: (b, 0, 0, 0)),
            pl.BlockSpec((tm, S5_WIDTH), lambda b, i: (row_block(b, i), 0)),
            s5_state_spec, s5_state_spec,
        ],
        out_shape=[
            jax.ShapeDtypeStruct((batch * seq, RET_WIDTH), BF16),
            jax.ShapeDtypeStruct((batch, RET_HEADS, HEAD_DIM, HEAD_DIM), F32),
            jax.ShapeDtypeStruct((batch * seq, S5_WIDTH), BF16),
            jax.ShapeDtypeStruct((batch, S5_PAIRS, LANES), F32),
            jax.ShapeDtypeStruct((batch, S5_PAIRS, LANES), F32),
        ],
        scratch_shapes=[pltpu.VMEM((tm, S5_WIDTH), F32)]
        + [pltpu.VMEM((sub * S5_PITCH, LANES), F32)] * (2 * (tm // sub)),
        compiler_params=_params("arbitrary", "arbitrary"),
        name="mixers_prompt",
    )(proj, proj, proj, proj, proj, proj, wb, cw, ar, ai, d, wglu, bglu)


def _s5_step_kernel(u_ref, z_ref, x0r_ref, x0i_ref, wb_ref, cw_ref, ar_ref, ai_ref, d_ref,
                    wglu_ref, bglu_ref, o_ref, sr_ref, si_ref, y_scr):
    for blk in range(S5_BLOCKS):
        cols = slice(blk * LANES, (blk + 1) * LANES)
        ub = u_ref[:, cols]
        acc = d_ref[:, cols] * ub.astype(F32)
        for j in range(blk * PAIRS_PER_BLOCK, (blk + 1) * PAIRS_PER_BLOCK):
            pc = slice(j * LANES, (j + 1) * LANES)
            bu = jnp.dot(ub, wb_ref[j], preferred_element_type=F32)
            ar = ar_ref[j:j + 1, :]
            ai = ai_ref[j:j + 1, :]
            x0r = x0r_ref[:, pc]
            x0i = x0i_ref[:, pc]
            nr = ar * x0r - ai * x0i + bu[:, :LANES]
            ni = ar * x0i + ai * x0r + bu[:, LANES:]
            sr_ref[:, pc] = nr
            si_ref[:, pc] = ni
            x = jnp.concatenate([nr, ni], axis=1).astype(BF16)
            acc = acc + lax.dot_general(x, cw_ref[j], (((1,), (1,)), ((), ())),
                                        preferred_element_type=F32)
        y_scr[:, cols] = acc
    o_ref[...] = _s5_output_gate(y_scr[...], z_ref[...], wglu_ref, bglu_ref).astype(BF16)


def _s5_sample(proj, x0r, x0i, wb, cw, ar, ai, d, wglu, bglu):
    nb = proj.shape[0]
    ub_col = (4 * RET_WIDTH) // S5_WIDTH
    full = lambda a: pl.BlockSpec(a.shape, lambda i: (0,) * a.ndim)
    return pl.pallas_call(
        _s5_step_kernel,
        grid=(1,),
        in_specs=[
            pl.BlockSpec((nb, S5_WIDTH), lambda i: (0, ub_col)),
            pl.BlockSpec((nb, S5_WIDTH), lambda i: (0, ub_col + 1)),
            full(x0r), full(x0i), full(wb), full(cw), full(ar), full(ai), full(d), full(wglu), full(bglu),
        ],
        out_specs=[
            pl.BlockSpec((nb, S5_WIDTH), lambda i: (0, 0)),
            full(x0r), full(x0i),
        ],
        out_shape=[
            jax.ShapeDtypeStruct((nb, S5_WIDTH), BF16),
            jax.ShapeDtypeStruct(x0r.shape, F32),
            jax.ShapeDtypeStruct(x0i.shape, F32),
        ],
        scratch_shapes=[pltpu.VMEM((nb, S5_WIDTH), F32)],
        compiler_params=_params("arbitrary"),
        name="s5_sample",
    )(proj, proj, x0r, x0i, wb, cw, ar, ai, d, wglu, bglu)


def _merge_kernel(ya_ref, yb_ref, ga_ref, gb_ref, x_ref, wpa_ref, wpb_ref, wout_ref, gpost_ref, o_ref,
                  merged_scr):
    slabs = [slice(s * MXU_DIM, (s + 1) * MXU_DIM) for s in range(D_MODEL // MXU_DIM)]
    ya_in = ya_ref[...].astype(BF16)
    yb_in = yb_ref[...].astype(BF16)
    for cols in slabs:
        ya = jnp.dot(ya_in, wpa_ref[:, cols], preferred_element_type=F32)
        yb = jnp.dot(yb_in, wpb_ref[:, cols], preferred_element_type=F32)
        merged_scr[:, cols] = (jax.nn.sigmoid(ga_ref[:, cols].astype(F32)) * ya
                               + jax.nn.sigmoid(gb_ref[:, cols].astype(F32)) * yb).astype(BF16)
    sq = jnp.zeros((o_ref.shape[0], 1), F32)
    for cols in slabs:
        out = jnp.dot(merged_scr[...], wout_ref[:, cols], preferred_element_type=F32)
        sq = sq + jnp.sum(out * out, axis=-1, keepdims=True)
        o_ref[:, cols] = out
    inv_rms = lax.rsqrt(sq * (1.0 / D_MODEL) + EPS)
    for cols in slabs:
        o_ref[:, cols] = x_ref[:, cols] + o_ref[:, cols] * inv_rms * gpost_ref[:, cols]


def _merge(ya, yb, proj, x2d, wpa, wpb, wout, gpost, *, tm):
    m = x2d.shape[0]
    ga_col = (4 * RET_WIDTH + 2 * S5_WIDTH) // D_MODEL
    return pl.pallas_call(
        _merge_kernel,
        grid=(m // tm,),
        in_specs=[
            pl.BlockSpec((tm, RET_WIDTH), lambda i: (i, 0)),
            pl.BlockSpec((tm, S5_WIDTH), lambda i: (i, 0)),
            pl.BlockSpec((tm, D_MODEL), lambda i: (i, ga_col)),
            pl.BlockSpec((tm, D_MODEL), lambda i: (i, ga_col + 1)),
            pl.BlockSpec((tm, D_MODEL), lambda i: (i, 0)),
            _resident(wpa.shape), _resident(wpb.shape), _resident(wout.shape), _resident(gpost.shape),
        ],
        out_specs=pl.BlockSpec((tm, D_MODEL), lambda i: (i, 0)),
        out_shape=jax.ShapeDtypeStruct((m, D_MODEL), F32),
        scratch_shapes=[pltpu.VMEM((tm, D_MODEL), BF16)],
        compiler_params=_params("arbitrary"),
        name="merge_out",
    )(ya, yb, proj, proj, x2d, wpa, wpb, wout, gpost)


def _rope_tables(pos):
    half = HEAD_DIM // 2
    inv = ROPE_BASE ** (-jnp.arange(half, dtype=F32) / half)
    ang = pos.astype(F32)[:, None] * inv[None, :]
    cos, sin = jnp.cos(ang), jnp.sin(ang)
    return jnp.concatenate([cos, cos], axis=-1), jnp.concatenate([-sin, sin], axis=-1)


def _s5_discretize(lam_re, lam_im, log_dt, b_re, b_im):
    dt = jnp.exp(log_dt)[:, None]
    mag = jnp.exp(lam_re * dt)
    abar_r = mag * jnp.cos(lam_im * dt)
    abar_i = mag * jnp.sin(lam_im * dt)
    nr, ni = abar_r - 1.0, abar_i
    den = lam_re * lam_re + lam_im * lam_im
    coef_r = (nr * lam_re + ni * lam_im) / den
    coef_i = (ni * lam_re - nr * lam_im) / den
    bbar_r = coef_r[:, :, None] * b_re - coef_i[:, :, None] * b_im
    bbar_i = coef_r[:, :, None] * b_im + coef_i[:, :, None] * b_re
    return abar_r, abar_i, bbar_r, bbar_i


def _s5_pair_weights(bbar_r, bbar_i, c_re, c_im):
    j = lax.broadcasted_iota(jnp.int32, (S5_PAIRS, LANES, 2 * LANES), 0)
    row = lax.broadcasted_iota(jnp.int32, (S5_PAIRS, LANES, 2 * LANES), 1)
    col = lax.broadcasted_iota(jnp.int32, (S5_PAIRS, LANES, 2 * LANES), 2)
    pair_lanes = 2 * S5_GROUP
    keep = ((row // pair_lanes == j % PAIRS_PER_BLOCK)
            & ((row // S5_GROUP) % 2 == (col // S5_P) % 2))

    def expand(re, im, perm):
        base = jnp.stack([re, im]).reshape((2, S5_PAIRS, 2) + re.shape[1:]).transpose(perm)
        base = base.reshape(S5_PAIRS, 1, S5_GROUP, 2 * LANES)
        tiled = jnp.broadcast_to(base, (S5_PAIRS, LANES // S5_GROUP, S5_GROUP, 2 * LANES))
        return jnp.where(keep, tiled.reshape(S5_PAIRS, LANES, 2 * LANES), 0.0).astype(BF16)

    wb = expand(bbar_r, bbar_i, (1, 4, 0, 2, 3))
    cw = expand(c_re, -c_im, (1, 3, 0, 2, 4))
    return wb, cw


def kernel(x_prompt, x_sample, state_ret, state_s5_re, state_s5_im, g_pre, w_in, w_pa, w_pb, w_out, g_post,
           s5_lam_re, s5_lam_im, s5_log_dt, s5_b_re, s5_b_im, s5_c_re, s5_c_im, s5_d, s5_w_glu, s5_b_glu):
    assert w_in.shape[0] == 1, "single trunk layer"
    bp, lp, _ = x_prompt.shape
    bs, ls, _ = x_sample.shape
    assert ls == 1 and lp % RET_CHUNK == 0

    w_pa_b = w_pa[0].astype(BF16)
    w_pb_b = w_pb[0].astype(BF16)
    w_out_b = w_out[0].astype(BF16)
    w_glu_b = s5_w_glu[0].astype(BF16)
    b_glu = s5_b_glu[0].reshape(1, S5_WIDTH)
    gpre = g_pre[0].reshape(1, D_MODEL)
    gpost = g_post[0].reshape(1, D_MODEL)

    abar_r, abar_i, bbar_r, bbar_i = _s5_discretize(
        s5_lam_re[0], s5_lam_im[0], s5_log_dt[0], s5_b_re[0], s5_b_im[0])
    wb, cw = _s5_pair_weights(bbar_r, bbar_i, s5_c_re[0], s5_c_im[0])
    ar = abar_r.reshape(S5_PAIRS, LANES)
    ai = abar_i.reshape(S5_PAIRS, LANES)
    d_skip = s5_d[0].reshape(1, S5_WIDTH)

    xs = x_sample.reshape(bs, D_MODEL)
    cos_s, sin_s = _rope_tables(jnp.full((bs,), PAST_LEN, jnp.int32))
    proj_s, w_in_b = _inproj(xs, gpre, w_in[0], cos_s, sin_s, tm=bs, tn=1024)

    xp = x_prompt.reshape(bp * lp, D_MODEL)
    cos_p, sin_p = _rope_tables(jnp.arange(lp, dtype=jnp.int32))
    (proj_p,) = _inproj(xp, gpre, w_in_b, cos_p, sin_p, tm=1024, tn=2048)
    ya_p, ret_p, yb_p, s5r_p, s5i_p = _mixers_prompt(
        proj_p, wb, cw, ar, ai, d_skip, w_glu_b, b_glu, batch=bp, seq=lp, tm=256, sub=128)
    y_p = _merge(ya_p, yb_p, proj_p, xp, w_pa_b, w_pb_b, w_out_b, gpost, tm=512)

    ya_s, ret_s = _retention_sample(proj_s, state_ret[0], bb=16)
    x0r = state_s5_re[0].reshape(bs, S5_GROUPS * S5_P)
    x0i = state_s5_im[0].reshape(bs, S5_GROUPS * S5_P)
    yb_s, s5r_s, s5i_s = _s5_sample(proj_s, x0r, x0i, wb, cw, ar, ai, d_skip, w_glu_b, b_glu)
    y_s = _merge(ya_s, yb_s, proj_s, xs, w_pa_b, w_pb_b, w_out_b, gpost, tm=bs)

    state_shape = (1, -1, S5_GROUPS, S5_P)
    return (y_p.reshape(bp, lp, D_MODEL), y_s.reshape(bs, 1, D_MODEL),
            ret_p[None], s5r_p.reshape(state_shape), s5i_p.reshape(state_shape),
            ret_s[None], s5r_s.reshape(state_shape), s5i_s.reshape(state_shape))
```

```python
import functools
import math

import jax
import jax.numpy as jnp
import numpy as np
from jax import lax
from jax.experimental import pallas as pl
from jax.experimental.pallas import tpu as pltpu

F32 = jnp.float32
BF16 = jnp.bfloat16

D_MODEL = 2048
RET_WIDTH = 1024
RET_HEADS = 8
HEAD_DIM = RET_WIDTH // RET_HEADS
RET_CHUNK = 128
ROPE_BASE = 10000.0
S5_WIDTH = 1024
S5_GROUP = 16
S5_GROUPS = S5_WIDTH // S5_GROUP
S5_P = 64
PAST_LEN = 16384
EPS = 1e-6
IN_COLS = 4 * RET_WIDTH + 2 * S5_WIDTH + 2 * D_MODEL

LANES = 128
MXU_DIM = 256
BF16_SUBLANES = 16
S5_PAIRS = S5_GROUPS // 2
PAIRS_PER_BLOCK = LANES // (2 * S5_GROUP)
S5_BLOCKS = S5_WIDTH // LANES
S5_PITCH = 36
VMEM_LIMIT = 56 * 1024 * 1024

_LOG_GAMMA = [float(np.log(np.float32(1.0) - np.float32(2.0) ** np.float32(-5.0 - h)))
              for h in range(RET_HEADS)]


def _params(*sem):
    return pltpu.CompilerParams(dimension_semantics=sem, vmem_limit_bytes=VMEM_LIMIT)


def _resident(shape):
    nd = len(shape)
    return pl.BlockSpec(shape, lambda *_: (0,) * nd, pipeline_mode=pl.Buffered(1))


def _rope(x, cos, sin_signed):
    return x * cos + pltpu.roll(x, HEAD_DIM // 2, 1) * sin_signed


def _inproj_kernel(x_ref, g_ref, w_ref, cos_ref, sin_ref, *rest, slab, tn, emit_weights, n_side):
    side_in, rest = rest[:n_side], rest[n_side:]
    o_ref, h_ref = rest[0], rest[-1]
    wcopy_ref = rest[1] if emit_weights else None
    side_out = rest[len(rest) - 1 - n_side:-1]
    j = pl.program_id(1)

    @pl.when(j == 0)
    def _():
        g = g_ref[...]

        def body(r, _):
            rows = pl.ds(pl.multiple_of(r * slab, slab), slab)
            x = x_ref[rows, :]
            ms = jnp.mean(x * x, axis=-1, keepdims=True)
            h_ref[rows, :] = (x * lax.rsqrt(ms + EPS) * g).astype(BF16)
            return 0

        lax.fori_loop(0, x_ref.shape[0] // slab, body, 0)

    def rotate(s, acc):
        is_k = j * tn + s * MXU_DIM >= RET_WIDTH
        scale = jnp.where(is_k, HEAD_DIM ** -0.5, 1.0).astype(F32)
        cos = cos_ref[...] * scale
        sin = sin_ref[...] * scale
        return jnp.concatenate([_rope(acc[:, hh * HEAD_DIM:(hh + 1) * HEAD_DIM], cos, sin)
                                for hh in range(MXU_DIM // HEAD_DIM)], axis=1)

    epilogues = {"rope": rotate, "plain": lambda s, acc: acc}
    col_kinds = (("rope", 2 * RET_WIDTH), ("plain", IN_COLS - 2 * RET_WIDTH))
    slab_kinds = [kind for kind, width in col_kinds for _ in range(width // MXU_DIM)]
    per_tile = tn // MXU_DIM
    tiles_by_pattern = {}
    for tile in range(len(slab_kinds) // per_tile):
        pattern = tuple(slab_kinds[tile * per_tile:(tile + 1) * per_tile])
        tiles_by_pattern.setdefault(pattern, []).append(tile)

    def slabs(pattern):
        for src_ref, dst_ref in zip(side_in, side_out):
            dst_ref[...] = src_ref[...].astype(BF16)
        for s, kind in enumerate(pattern):
            cols = slice(s * MXU_DIM, (s + 1) * MXU_DIM)
            w = w_ref[:, cols]
            if wcopy_ref is not None:
                w = w.astype(BF16)
                wcopy_ref[:, cols] = w
            acc = jnp.dot(h_ref[...], w, preferred_element_type=F32)
            o_ref[:, cols] = epilogues[kind](s, acc).astype(BF16)

    for pattern, tiles in tiles_by_pattern.items():
        is_this_kind = functools.reduce(jnp.logical_or, [j == tile for tile in tiles])
        pl.when(is_this_kind)(functools.partial(slabs, pattern))


def _inproj(x2d, g_pre, w_in, cos, sin, *, tm, tn, side_casts=()):
    m, d = x2d.shape
    n = w_in.shape[1]
    pos_tiles = cos.shape[0] // tm
    emit_weights = w_in.dtype != BF16
    assert not emit_weights or m == tm, "each weight tile must be visited once to be copied out"
    n_col_tiles = n // tn
    n_steps = (m // tm) * n_col_tiles
    in_specs = [
        pl.BlockSpec((tm, d), lambda i, j: (i, 0)),
        pl.BlockSpec((1, d), lambda i, j: (0, 0)),
        pl.BlockSpec((d, tn), lambda i, j: (0, j)),
        pl.BlockSpec((tm, HEAD_DIM), lambda i, j: (i % pos_tiles, 0)),
        pl.BlockSpec((tm, HEAD_DIM), lambda i, j: (i % pos_tiles, 0)),
    ]
    out_specs = [pl.BlockSpec((tm, tn), lambda i, j: (i, j))]
    out_shape = [jax.ShapeDtypeStruct((m, n), BF16)]
    if emit_weights:
        out_specs.append(pl.BlockSpec((d, tn), lambda i, j: (0, j)))
        out_shape.append(jax.ShapeDtypeStruct((d, n), BF16))
    for a in side_casts:
        rows = -(-a.shape[0] // n_steps)
        rows = -(-rows // BF16_SUBLANES) * BF16_SUBLANES
        assert a.shape[0] % rows == 0
        last = a.shape[0] // rows - 1
        spec = pl.BlockSpec((rows, a.shape[1]),
                            lambda i, j, last=last: (jnp.minimum(i * n_col_tiles + j, last), 0))
        in_specs.append(spec)
        out_specs.append(spec)
        out_shape.append(jax.ShapeDtypeStruct(a.shape, BF16))
    return pl.pallas_call(
        functools.partial(_inproj_kernel, slab=min(tm, 128), tn=tn, emit_weights=emit_weights,
                          n_side=len(side_casts)),
        grid=(m // tm, n_col_tiles),
        in_specs=in_specs,
        out_specs=out_specs,
        out_shape=out_shape,
        scratch_shapes=[pltpu.VMEM((tm, d), BF16)],
        compiler_params=_params("arbitrary", "arbitrary"),
        name="inproj",
    )(x2d, g_pre, w_in, cos, sin, *side_casts)


def _group_norm(o):
    mu = jnp.mean(o, axis=-1, keepdims=True)
    d = o - mu
    var = jnp.mean(d * d, axis=-1, keepdims=True)
    return d * lax.rsqrt(var + EPS)


def _retention_tile(q_ref, k_ref, v_ref, z_ref, o_ref, s_ref):
    c = RET_CHUNK
    n_chunks = q_ref.shape[0] // c
    row = lax.broadcasted_iota(jnp.int32, (c, c), 0).astype(F32)
    col = lax.broadcasted_iota(jnp.int32, (c, c), 1).astype(F32)
    diff = row - col
    for h in range(RET_HEADS):
        lg = _LOG_GAMMA[h]
        mask = jnp.where(diff >= 0, jnp.exp(jnp.maximum(diff, 0.0) * lg), 0.0)
        q_decay = jnp.exp((row + 1.0) * lg)
        k_decay = jnp.exp((c - 1.0 - row) * lg)
        chunk_decay = math.exp(c * lg)
        cols = slice(h * HEAD_DIM, (h + 1) * HEAD_DIM)
        for ci in range(n_chunks):
            rows = slice(ci * c, (ci + 1) * c)
            q = q_ref[rows, cols]
            k = k_ref[rows, cols]
            v = v_ref[rows, cols]
            s0 = s_ref[0, h]
            scores = lax.dot_general(q, k, (((1,), (1,)), ((), ())),
                                     preferred_element_type=F32) * mask
            inner = jnp.dot(scores.astype(BF16), v, preferred_element_type=F32)
            cross = jnp.dot(q, s0.astype(BF16), preferred_element_type=F32) * q_decay
            kd = (k.astype(F32) * k_decay).astype(BF16)
            s_ref[0, h] = chunk_decay * s0 + lax.dot_general(
                kd, v, (((0,), (0,)), ((), ())), preferred_element_type=F32)
            o = _group_norm(inner + cross)
            o_ref[rows, cols] = (o * jax.nn.silu(z_ref[rows, cols].astype(F32))).astype(BF16)


def _retention_step_kernel(q_ref, k_ref, v_ref, z_ref, s_ref, o_ref, sn_ref, acc_ref, *, bb):
    pad = jnp.zeros((HEAD_DIM - bb, HEAD_DIM), F32)
    for h in range(RET_HEADS):
        gamma = math.exp(_LOG_GAMMA[h])
        cols = slice(h * HEAD_DIM, (h + 1) * HEAD_DIM)
        q = q_ref[:, cols].astype(F32)
        k = k_ref[:, cols].astype(F32)
        v = v_ref[:, cols].astype(F32)
        qt = jnp.concatenate([q, pad], axis=0).T
        kt = jnp.concatenate([k, pad], axis=0).T
        for b in range(bb):
            s_new = gamma * s_ref[b, h] + kt[:, b:b + 1] * v[b:b + 1, :]
            sn_ref[b, h] = s_new
            acc_ref[b:b + 1, cols] = jnp.sum(qt[:, b:b + 1] * s_new, axis=0, keepdims=True)
    for h in range(RET_HEADS):
        cols = slice(h * HEAD_DIM, (h + 1) * HEAD_DIM)
        o_ref[:, cols] = _group_norm(acc_ref[:, cols]) * jax.nn.silu(z_ref[:, cols].astype(F32))


def _retention_sample(proj, state, *, bb):
    nb = state.shape[0]
    col_spec = lambda cb: pl.BlockSpec((bb, RET_WIDTH), lambda i: (i, cb))
    state_spec = pl.BlockSpec((bb, RET_HEADS, HEAD_DIM, HEAD_DIM), lambda i: (i, 0, 0, 0))
    return pl.pallas_call(
        functools.partial(_retention_step_kernel, bb=bb),
        grid=(nb // bb,),
        in_specs=[col_spec(0), col_spec(1), col_spec(2), col_spec(3), state_spec],
        out_specs=[pl.BlockSpec((bb, RET_WIDTH), lambda i: (i, 0)), state_spec],
        out_shape=[
            jax.ShapeDtypeStruct((nb, RET_WIDTH), F32),
            jax.ShapeDtypeStruct(state.shape, F32),
        ],
        scratch_shapes=[pltpu.VMEM((bb, RET_WIDTH), F32)],
        compiler_params=_params("arbitrary"),
        name="retention_sample",
    )(proj, proj, proj, proj, state)


def _s5_output_gate(y, z, wglu_ref, bglu_ref):
    y = jax.nn.gelu(y)
    g = jnp.dot(y.astype(BF16), wglu_ref[...], preferred_element_type=F32) + bglu_ref[...]
    return y * jax.nn.sigmoid(g) * jax.nn.silu(z.astype(F32))


def _s5_tile(u_ref, z_ref, wb_ref, cw_ref, ar_ref, ai_ref, d_ref, wglu_ref, bglu_ref,
             o_ref, sr_ref, si_ref, y_scr, x_scr, *, sub):
    n_sub = len(x_scr) // 2
    xr_scr, xi_scr = x_scr[:n_sub], x_scr[n_sub:]
    strided = lambda j: pl.ds(j, sub, stride=S5_PITCH)
    ar = ar_ref[...]
    ai = ai_ref[...]
    xr, xi = sr_ref[0], si_ref[0]
    for k in range(n_sub):
        rows = slice(k * sub, (k + 1) * sub)
        for blk in range(S5_BLOCKS):
            ub = u_ref[rows, blk * LANES:(blk + 1) * LANES]
            for j in range(blk * PAIRS_PER_BLOCK, (blk + 1) * PAIRS_PER_BLOCK):
                bu = jnp.dot(ub, wb_ref[j], preferred_element_type=F32)
                xr_scr[k][strided(j), :] = bu[:, :LANES]
                xi_scr[k][strided(j), :] = bu[:, LANES:]
        for t in range(sub):
            step = slice(t * S5_PITCH, t * S5_PITCH + S5_PAIRS)
            xr, xi = (ar * xr - ai * xi + xr_scr[k][step, :],
                      ar * xi + ai * xr + xi_scr[k][step, :])
            xr_scr[k][step, :] = xr
            xi_scr[k][step, :] = xi
        for blk in range(S5_BLOCKS):
            cols = slice(blk * LANES, (blk + 1) * LANES)
            acc = d_ref[:, cols] * u_ref[rows, cols].astype(F32)
            for j in range(blk * PAIRS_PER_BLOCK, (blk + 1) * PAIRS_PER_BLOCK):
                x = jnp.concatenate([xr_scr[k][strided(j), :], xi_scr[k][strided(j), :]], axis=1)
                acc = acc + lax.dot_general(x.astype(BF16), cw_ref[j], (((1,), (1,)), ((), ())),
                                            preferred_element_type=F32)
            y_scr[rows, cols] = acc
        o_ref[rows, :] = _s5_output_gate(
            y_scr[rows, :], z_ref[rows, :], wglu_ref, bglu_ref).astype(BF16)
    sr_ref[0] = xr
    si_ref[0] = xi


def _mixers_kernel(q_ref, k_ref, v_ref, za_ref, u_ref, zb_ref, wb_ref, cw_ref, ar_ref, ai_ref, d_ref,
                   wglu_ref, bglu_ref, ya_ref, ret_ref, yb_ref, sr_ref, si_ref, y_scr, *x_scr, sub):
    @pl.when(pl.program_id(1) == 0)
    def _():
        ret_ref[...] = jnp.zeros_like(ret_ref)
        sr_ref[...] = jnp.zeros_like(sr_ref)
        si_ref[...] = jnp.zeros_like(si_ref)

    _retention_tile(q_ref, k_ref, v_ref, za_ref, ya_ref, ret_ref)
    _s5_tile(u_ref, zb_ref, wb_ref, cw_ref, ar_ref, ai_ref, d_ref, wglu_ref, bglu_ref,
             yb_ref, sr_ref, si_ref, y_scr, x_scr, sub=sub)


def _mixers_prompt(proj, wb, cw, ar, ai, d, wglu, bglu, *, batch, seq, tm, sub):
    nt = seq // tm
    row_block = lambda b, i: b * nt + i
    col_spec = lambda cb: pl.BlockSpec((tm, RET_WIDTH), lambda b, i: (row_block(b, i), cb))
    s5_state_spec = pl.BlockSpec((1, S5_PAIRS, LANES), lambda b, i: (b, 0, 0))
    assert RET_WIDTH == S5_WIDTH
    return pl.pallas_call(
        functools.partial(_mixers_kernel, sub=sub),
        grid=(batch, nt),
        in_specs=[
            col_spec(0), col_spec(1), col_spec(2), col_spec(3), col_spec(4), col_spec(5),
            _resident(wb.shape), _resident(cw.shape), _resident(ar.shape), _resident(ai.shape),
            _resident(d.shape), _resident(wglu.shape), _resident(bglu.shape),
        ],
        out_specs=[
            pl.BlockSpec((tm, RET_WIDTH), lambda b, i: (row_block(b, i), 0)),
            pl.BlockSpec((1, RET_HEADS, HEAD_DIM, HEAD_DIM), lambda b, i: (b, 0, 0, 0)),
            pl.BlockSpec((tm, S5_WIDTH), lambda b, i: (row_block(b, i), 0)),
            s5_state_spec, s5_state_spec,
        ],
        out_shape=[
            jax.ShapeDtypeStruct((batch * seq, RET_WIDTH), BF16),
            jax.ShapeDtypeStruct((batch, RET_HEADS, HEAD_DIM, HEAD_DIM), F32),
            jax.ShapeDtypeStruct((batch * seq, S5_WIDTH), BF16),
            jax.ShapeDtypeStruct((batch, S5_PAIRS, LANES), F32),
            jax.ShapeDtypeStruct((batch, S5_PAIRS, LANES), F32),
        ],
        scratch_shapes=[pltpu.VMEM((tm, S5_WIDTH), F32)]
        + [pltpu.VMEM((sub * S5_PITCH, LANES), F32)] * (2 * (tm // sub)),
        compiler_params=_params("arbitrary", "arbitrary"),
        name="mixers_prompt",
    )(proj, proj, proj, proj, proj, proj, wb, cw, ar, ai, d, wglu, bglu)


def _s5_step_kernel(u_ref, z_ref, x0r_ref, x0i_ref, wb_ref, cw_ref, ar_ref, ai_ref, d_ref,
                    wglu_ref, bglu_ref, o_ref, sr_ref, si_ref, y_scr):
    for blk in range(S5_BLOCKS):
        cols = slice(blk * LANES, (blk + 1) * LANES)
        ub = u_ref[:, cols]
        acc = d_ref[:, cols] * ub.astype(F32)
        for j in range(blk * PAIRS_PER_BLOCK, (blk + 1) * PAIRS_PER_BLOCK):
            pc = slice(j * LANES, (j + 1) * LANES)
            bu = jnp.dot(ub, wb_ref[j], preferred_element_type=F32)
            ar = ar_ref[j:j + 1, :]
            ai = ai_ref[j:j + 1, :]
            x0r = x0r_ref[:, pc]
            x0i = x0i_ref[:, pc]
            nr = ar * x0r - ai * x0i + bu[:, :LANES]
            ni = ar * x0i + ai * x0r + bu[:, LANES:]
            sr_ref[:, pc] = nr
            si_ref[:, pc] = ni
            x = jnp.concatenate([nr, ni], axis=1).astype(BF16)
            acc = acc + lax.dot_general(x, cw_ref[j], (((1,), (1,)), ((), ())),
                                        preferred_element_type=F32)
        y_scr[:, cols] = acc
    o_ref[...] = _s5_output_gate(y_scr[...], z_ref[...], wglu_ref, bglu_ref).astype(BF16)


def _s5_sample(proj, x0r, x0i, wb, cw, ar, ai, d, wglu, bglu):
    nb = proj.shape[0]
    ub_col = (4 * RET_WIDTH) // S5_WIDTH
    full = lambda a: pl.BlockSpec(a.shape, lambda i: (0,) * a.ndim)
    return pl.pallas_call(
        _s5_step_kernel,
        grid=(1,),
        in_specs=[
            pl.BlockSpec((nb, S5_WIDTH), lambda i: (0, ub_col)),
            pl.BlockSpec((nb, S5_WIDTH), lambda i: (0, ub_col + 1)),
            full(x0r), full(x0i), full(wb), full(cw), full(ar), full(ai), full(d), full(wglu), full(bglu),
        ],
        out_specs=[
            pl.BlockSpec((nb, S5_WIDTH), lambda i: (0, 0)),
            full(x0r), full(x0i),
        ],
        out_shape=[
            jax.ShapeDtypeStruct((nb, S5_WIDTH), BF16),
            jax.ShapeDtypeStruct(x0r.shape, F32),
            jax.ShapeDtypeStruct(x0i.shape, F32),
        ],
        scratch_shapes=[pltpu.VMEM((nb, S5_WIDTH), F32)],
        compiler_params=_params("arbitrary"),
        name="s5_sample",
    )(proj, proj, x0r, x0i, wb, cw, ar, ai, d, wglu, bglu)


def _merge_kernel(ya_ref, yb_ref, ga_ref, gb_ref, x_ref, wpa_ref, wpb_ref, wout_ref, gpost_ref, o_ref,
                  merged_scr):
    slabs = [slice(s * MXU_DIM, (s + 1) * MXU_DIM) for s in range(D_MODEL // MXU_DIM)]
    ya_in = ya_ref[...].astype(BF16)
    yb_in = yb_ref[...].astype(BF16)
    for cols in slabs:
        ya = jnp.dot(ya_in, wpa_ref[:, cols], preferred_element_type=F32)
        yb = jnp.dot(yb_in, wpb_ref[:, cols], preferred_element_type=F32)
        merged_scr[:, cols] = (jax.nn.sigmoid(ga_ref[:, cols].astype(F32)) * ya
                               + jax.nn.sigmoid(gb_ref[:, cols].astype(F32)) * yb).astype(BF16)
    sq = jnp.zeros((o_ref.shape[0], 1), F32)
    for cols in slabs:
        out = jnp.dot(merged_scr[...], wout_ref[:, cols], preferred_element_type=F32)
        sq = sq + jnp.sum(out * out, axis=-1, keepdims=True)
        o_ref[:, cols] = out
    inv_rms = lax.rsqrt(sq * (1.0 / D_MODEL) + EPS)
    for cols in slabs:
        o_ref[:, cols] = x_ref[:, cols] + o_ref[:, cols] * inv_rms * gpost_ref[:, cols]


def _merge(ya, yb, proj, x2d, wpa, wpb, wout, gpost, *, tm):
    m = x2d.shape[0]
    ga_col = (4 * RET_WIDTH + 2 * S5_WIDTH) // D_MODEL
    return pl.pallas_call(
        _merge_kernel,
        grid=(m // tm,),
        in_specs=[
            pl.BlockSpec((tm, RET_WIDTH), lambda i: (i, 0)),
            pl.BlockSpec((tm, S5_WIDTH), lambda i: (i, 0)),
            pl.BlockSpec((tm, D_MODEL), lambda i: (i, ga_col)),
            pl.BlockSpec((tm, D_MODEL), lambda i: (i, ga_col + 1)),
            pl.BlockSpec((tm, D_MODEL), lambda i: (i, 0)),
            _resident(wpa.shape), _resident(wpb.shape), _resident(wout.shape), _resident(gpost.shape),
        ],
        out_specs=pl.BlockSpec((tm, D_MODEL), lambda i: (i, 0)),
        out_shape=jax.ShapeDtypeStruct((m, D_MODEL), F32),
        scratch_shapes=[pltpu.VMEM((tm, D_MODEL), BF16)],
        compiler_params=_params("arbitrary"),
        name="merge_out",
    )(ya, yb, proj, proj, x2d, wpa, wpb, wout, gpost)


def _rope_tables(pos):
    half = HEAD_DIM // 2
    inv = ROPE_BASE ** (-jnp.arange(half, dtype=F32) / half)
    ang = pos.astype(F32)[:, None] * inv[None, :]
    cos, sin = jnp.cos(ang), jnp.sin(ang)
    return jnp.concatenate([cos, cos], axis=-1), jnp.concatenate([-sin, sin], axis=-1)


def _s5_discretize(lam_re, lam_im, log_dt, b_re, b_im):
    dt = jnp.exp(log_dt)[:, None]
    mag = jnp.exp(lam_re * dt)
    abar_r = mag * jnp.cos(lam_im * dt)
    abar_i = mag * jnp.sin(lam_im * dt)
    nr, ni = abar_r - 1.0, abar_i
    den = lam_re * lam_re + lam_im * lam_im
    coef_r = (nr * lam_re + ni * lam_im) / den
    coef_i = (ni * lam_re - nr * lam_im) / den
    bbar_r = coef_r[:, :, None] * b_re - coef_i[:, :, None] * b_im
    bbar_i = coef_r[:, :, None] * b_im + coef_i[:, :, None] * b_re
    return abar_r, abar_i, bbar_r, bbar_i


def _s5_pair_weights(bbar_r, bbar_i, c_re, c_im):
    j = lax.broadcasted_iota(jnp.int32, (S5_PAIRS, LANES, 2 * LANES), 0)
    row = lax.broadcasted_iota(jnp.int32, (S5_PAIRS, LANES, 2 * LANES), 1)
    col = lax.broadcasted_iota(jnp.int32, (S5_PAIRS, LANES, 2 * LANES), 2)
    pair_lanes = 2 * S5_GROUP
    keep = ((row // pair_lanes == j % PAIRS_PER_BLOCK)
            & ((row // S5_GROUP) % 2 == (col // S5_P) % 2))

    def expand(re, im, perm):
        base = jnp.stack([re, im]).reshape((2, S5_PAIRS, 2) + re.shape[1:]).transpose(perm)
        base = base.reshape(S5_PAIRS, 1, S5_GROUP, 2 * LANES)
        tiled = jnp.broadcast_to(base, (S5_PAIRS, LANES // S5_GROUP, S5_GROUP, 2 * LANES))
        return jnp.where(keep, tiled.reshape(S5_PAIRS, LANES, 2 * LANES), 0.0).astype(BF16)

    wb = expand(bbar_r, bbar_i, (1, 4, 0, 2, 3))
    cw = expand(c_re, -c_im, (1, 3, 0, 2, 4))
    return wb, cw


def kernel(x_prompt, x_sample, state_ret, state_s5_re, state_s5_im, g_pre, w_in, w_pa, w_pb, w_out, g_post,
           s5_lam_re, s5_lam_im, s5_log_dt, s5_b_re, s5_b_im, s5_c_re, s5_c_im, s5_d, s5_w_glu, s5_b_glu):
    assert w_in.shape[0] == 1, "single trunk layer"
    bp, lp, _ = x_prompt.shape
    bs, ls, _ = x_sample.shape
    assert ls == 1 and lp % RET_CHUNK == 0

    b_glu = s5_b_glu[0].reshape(1, S5_WIDTH)
    gpre = g_pre[0].reshape(1, D_MODEL)
    gpost = g_post[0].reshape(1, D_MODEL)

    abar_r, abar_i, bbar_r, bbar_i = _s5_discretize(
        s5_lam_re[0], s5_lam_im[0], s5_log_dt[0], s5_b_re[0], s5_b_im[0])
    wb, cw = _s5_pair_weights(bbar_r, bbar_i, s5_c_re[0], s5_c_im[0])
    ar = abar_r.reshape(S5_PAIRS, LANES)
    ai = abar_i.reshape(S5_PAIRS, LANES)
    d_skip = s5_d[0].reshape(1, S5_WIDTH)

    xs = x_sample.reshape(bs, D_MODEL)
    cos_s, sin_s = _rope_tables(jnp.full((bs,), PAST_LEN, jnp.int32))
    proj_s, w_in_b = _inproj(xs, gpre, w_in[0], cos_s, sin_s, tm=bs, tn=1024)

    xp = x_prompt.reshape(bp * lp, D_MODEL)
    cos_p, sin_p = _rope_tables(jnp.arange(lp, dtype=jnp.int32))
    proj_p, w_pa_b, w_pb_b, w_out_b, w_glu_b = _inproj(
        xp, gpre, w_in_b, cos_p, sin_p, tm=1024, tn=2048,
        side_casts=(w_pa[0], w_pb[0], w_out[0], s5_w_glu[0]))
    ya_p, ret_p, yb_p, s5r_p, s5i_p = _mixers_prompt(
        proj_p, wb, cw, ar, ai, d_skip, w_glu_b, b_glu, batch=bp, seq=lp, tm=256, sub=128)
    y_p = _merge(ya_p, yb_p, proj_p, xp, w_pa_b, w_pb_b, w_out_b, gpost, tm=512)

    ya_s, ret_s = _retention_sample(proj_s, state_ret[0], bb=16)
    x0r = state_s5_re[0].reshape(bs, S5_GROUPS * S5_P)
    x0i = state_s5_im[0].reshape(bs, S5_GROUPS * S5_P)
    yb_s, s5r_s, s5i_s = _s5_sample(proj_s, x0r, x0i, wb, cw, ar, ai, d_skip, w_glu_b, b_glu)
    y_s = _merge(ya_s, yb_s, proj_s, xs, w_pa_b, w_pb_b, w_out_b, gpost, tm=bs)

    state_shape = (1, -1, S5_GROUPS, S5_P)
    return (y_p.reshape(bp, lp, D_MODEL), y_s.reshape(bs, 1, D_MODEL),
            ret_p[None], s5r_p.reshape(state_shape), s5i_p.reshape(state_shape),
            ret_s[None], s5r_s.reshape(state_shape), s5i_s.reshape(state_shape))
```

```python
import functools
import math

import jax
import jax.numpy as jnp
import numpy as np
from jax import lax
from jax.experimental import pallas as pl
from jax.experimental.pallas import tpu as pltpu

F32 = jnp.float32
BF16 = jnp.bfloat16

D_MODEL = 2048
RET_WIDTH = 1024
RET_HEADS = 8
HEAD_DIM = RET_WIDTH // RET_HEADS
RET_CHUNK = 128
ROPE_BASE = 10000.0
S5_WIDTH = 1024
S5_GROUP = 16
S5_GROUPS = S5_WIDTH // S5_GROUP
S5_P = 64
PAST_LEN = 16384
EPS = 1e-6
IN_COLS = 4 * RET_WIDTH + 2 * S5_WIDTH + 2 * D_MODEL

LANES = 128
MXU_DIM = 256
BF16_SUBLANES = 16
S5_PAIRS = S5_GROUPS // 2
PAIRS_PER_BLOCK = LANES // (2 * S5_GROUP)
S5_BLOCKS = S5_WIDTH // LANES
S5_PITCH = 36
VMEM_LIMIT = 56 * 1024 * 1024

INPROJ_ROWS = 1024
INPROJ_COLS_PROMPT = 2048
INPROJ_COLS_SAMPLE = 1024
MIXER_ROWS = 256
S5_SUBTILE_ROWS = 128
MERGE_ROWS = 512
RET_SAMPLE_SEQS = 16

_LOG_GAMMA = [float(np.log(np.float32(1.0) - np.float32(2.0) ** np.float32(-5.0 - h)))
              for h in range(RET_HEADS)]


def _params(*sem):
    return pltpu.CompilerParams(dimension_semantics=sem, vmem_limit_bytes=VMEM_LIMIT)


def _resident(shape):
    nd = len(shape)
    return pl.BlockSpec(shape, lambda *_: (0,) * nd, pipeline_mode=pl.Buffered(1))


def _rope(x, cos, sin_signed):
    return x * cos + pltpu.roll(x, HEAD_DIM // 2, 1) * sin_signed


def _inproj_kernel(x_ref, g_ref, w_ref, cos_ref, sin_ref, *rest, slab, tn, emit_weights, n_side):
    side_in, rest = rest[:n_side], rest[n_side:]
    o_ref, h_ref = rest[0], rest[-1]
    wcopy_ref = rest[1] if emit_weights else None
    side_out = rest[len(rest) - 1 - n_side:-1]
    j = pl.program_id(1)

    @pl.when(j == 0)
    def _():
        g = g_ref[...]

        def body(r, _):
            rows = pl.ds(pl.multiple_of(r * slab, slab), slab)
            x = x_ref[rows, :]
            ms = jnp.mean(x * x, axis=-1, keepdims=True)
            h_ref[rows, :] = (x * lax.rsqrt(ms + EPS) * g).astype(BF16)
            return 0

        lax.fori_loop(0, x_ref.shape[0] // slab, body, 0)

    def rotate(s, acc):
        is_k = j * tn + s * MXU_DIM >= RET_WIDTH
        scale = jnp.where(is_k, HEAD_DIM ** -0.5, 1.0).astype(F32)
        cos = cos_ref[...] * scale
        sin = sin_ref[...] * scale
        return jnp.concatenate([_rope(acc[:, hh * HEAD_DIM:(hh + 1) * HEAD_DIM], cos, sin)
                                for hh in range(MXU_DIM // HEAD_DIM)], axis=1)

    epilogues = {"rope": rotate, "plain": lambda s, acc: acc}
    col_kinds = (("rope", 2 * RET_WIDTH), ("plain", IN_COLS - 2 * RET_WIDTH))
    slab_kinds = [kind for kind, width in col_kinds for _ in range(width // MXU_DIM)]
    per_tile = tn // MXU_DIM
    tiles_by_pattern = {}
    for tile in range(len(slab_kinds) // per_tile):
        pattern = tuple(slab_kinds[tile * per_tile:(tile + 1) * per_tile])
        tiles_by_pattern.setdefault(pattern, []).append(tile)

    def slabs(pattern):
        for src_ref, dst_ref in zip(side_in, side_out):
            dst_ref[...] = src_ref[...].astype(BF16)
        for s, kind in enumerate(pattern):
            cols = slice(s * MXU_DIM, (s + 1) * MXU_DIM)
            w = w_ref[:, cols]
            if wcopy_ref is not None:
                w = w.astype(BF16)
                wcopy_ref[:, cols] = w
            acc = jnp.dot(h_ref[...], w, preferred_element_type=F32)
            o_ref[:, cols] = epilogues[kind](s, acc).astype(BF16)

    for pattern, tiles in tiles_by_pattern.items():
        is_this_kind = functools.reduce(jnp.logical_or, [j == tile for tile in tiles])
        pl.when(is_this_kind)(functools.partial(slabs, pattern))


def _inproj(x2d, g_pre, w_in, cos, sin, *, tm, tn, side_casts=()):
    m, d = x2d.shape
    n = w_in.shape[1]
    pos_tiles = cos.shape[0] // tm
    emit_weights = w_in.dtype != BF16
    assert not emit_weights or m == tm, "each weight tile must be visited once to be copied out"
    n_col_tiles = n // tn
    n_steps = (m // tm) * n_col_tiles
    in_specs = [
        pl.BlockSpec((tm, d), lambda i, j: (i, 0)),
        pl.BlockSpec((1, d), lambda i, j: (0, 0)),
        pl.BlockSpec((d, tn), lambda i, j: (0, j)),
        pl.BlockSpec((tm, HEAD_DIM), lambda i, j: (i % pos_tiles, 0)),
        pl.BlockSpec((tm, HEAD_DIM), lambda i, j: (i % pos_tiles, 0)),
    ]
    out_specs = [pl.BlockSpec((tm, tn), lambda i, j: (i, j))]
    out_shape = [jax.ShapeDtypeStruct((m, n), BF16)]
    if emit_weights:
        out_specs.append(pl.BlockSpec((d, tn), lambda i, j: (0, j)))
        out_shape.append(jax.ShapeDtypeStruct((d, n), BF16))
    for a in side_casts:
        rows = -(-a.shape[0] // n_steps)
        rows = -(-rows // BF16_SUBLANES) * BF16_SUBLANES
        assert a.shape[0] % rows == 0
        last = a.shape[0] // rows - 1
        spec = pl.BlockSpec((rows, a.shape[1]),
                            lambda i, j, last=last: (jnp.minimum(i * n_col_tiles + j, last), 0))
        in_specs.append(spec)
        out_specs.append(spec)
        out_shape.append(jax.ShapeDtypeStruct(a.shape, BF16))
    return pl.pallas_call(
        functools.partial(_inproj_kernel, slab=min(tm, 128), tn=tn, emit_weights=emit_weights,
                          n_side=len(side_casts)),
        grid=(m // tm, n_col_tiles),
        in_specs=in_specs,
        out_specs=out_specs,
        out_shape=out_shape,
        scratch_shapes=[pltpu.VMEM((tm, d), BF16)],
        compiler_params=_params("arbitrary", "arbitrary"),
        name="inproj",
    )(x2d, g_pre, w_in, cos, sin, *side_casts)


def _group_norm(o):
    mu = jnp.mean(o, axis=-1, keepdims=True)
    d = o - mu
    var = jnp.mean(d * d, axis=-1, keepdims=True)
    return d * lax.rsqrt(var + EPS)


def _retention_tile(q_ref, k_ref, v_ref, z_ref, o_ref, s_ref):
    c = RET_CHUNK
    row = lax.broadcasted_iota(jnp.int32, (c, c), 0).astype(F32)
    col = lax.broadcasted_iota(jnp.int32, (c, c), 1).astype(F32)
    diff = row - col
    for h in range(RET_HEADS):
        lg = _LOG_GAMMA[h]
        mask = jnp.where(diff >= 0, jnp.exp(jnp.maximum(diff, 0.0) * lg), 0.0)
        q_decay = jnp.exp((row + 1.0) * lg)
        k_decay = jnp.exp((c - 1.0 - row) * lg)
        chunk_decay = math.exp(c * lg)
        cols = slice(h * HEAD_DIM, (h + 1) * HEAD_DIM)
        for ci in range(q_ref.shape[0] // c):
            rows = slice(ci * c, (ci + 1) * c)
            q = q_ref[rows, cols]
            k = k_ref[rows, cols]
            v = v_ref[rows, cols]
            s0 = s_ref[0, h]
            scores = lax.dot_general(q, k, (((1,), (1,)), ((), ())),
                                     preferred_element_type=F32) * mask
            inner = jnp.dot(scores.astype(BF16), v, preferred_element_type=F32)
            cross = jnp.dot(q, s0.astype(BF16), preferred_element_type=F32) * q_decay
            kd = (k.astype(F32) * k_decay).astype(BF16)
            s_ref[0, h] = chunk_decay * s0 + lax.dot_general(
                kd, v, (((0,), (0,)), ((), ())), preferred_element_type=F32)
            o = _group_norm(inner + cross)
            o_ref[rows, cols] = (o * jax.nn.silu(z_ref[rows, cols].astype(F32))).astype(BF16)
            yield


def _retention_step_kernel(q_ref, k_ref, v_ref, z_ref, s_ref, o_ref, sn_ref, acc_ref, *, bb):
    pad = jnp.zeros((HEAD_DIM - bb, HEAD_DIM), F32)
    for h in range(RET_HEADS):
        gamma = math.exp(_LOG_GAMMA[h])
        cols = slice(h * HEAD_DIM, (h + 1) * HEAD_DIM)
        q = q_ref[:, cols].astype(F32)
        k = k_ref[:, cols].astype(F32)
        v = v_ref[:, cols].astype(F32)
        qt = jnp.concatenate([q, pad], axis=0).T
        kt = jnp.concatenate([k, pad], axis=0).T
        for b in range(bb):
            s_new = gamma * s_ref[b, h] + kt[:, b:b + 1] * v[b:b + 1, :]
            sn_ref[b, h] = s_new
            acc_ref[b:b + 1, cols] = jnp.sum(qt[:, b:b + 1] * s_new, axis=0, keepdims=True)
    for h in range(RET_HEADS):
        cols = slice(h * HEAD_DIM, (h + 1) * HEAD_DIM)
        o_ref[:, cols] = _group_norm(acc_ref[:, cols]) * jax.nn.silu(z_ref[:, cols].astype(F32))


def _retention_sample(proj, state, *, bb):
    nb = state.shape[0]
    col_spec = lambda cb: pl.BlockSpec((bb, RET_WIDTH), lambda i: (i, cb))
    state_spec = pl.BlockSpec((bb, RET_HEADS, HEAD_DIM, HEAD_DIM), lambda i: (i, 0, 0, 0))
    return pl.pallas_call(
        functools.partial(_retention_step_kernel, bb=bb),
        grid=(nb // bb,),
        in_specs=[col_spec(0), col_spec(1), col_spec(2), col_spec(3), state_spec],
        out_specs=[pl.BlockSpec((bb, RET_WIDTH), lambda i: (i, 0)), state_spec],
        out_shape=[
            jax.ShapeDtypeStruct((nb, RET_WIDTH), F32),
            jax.ShapeDtypeStruct(state.shape, F32),
        ],
        scratch_shapes=[pltpu.VMEM((bb, RET_WIDTH), F32)],
        compiler_params=_params("arbitrary"),
        name="retention_sample",
    )(proj, proj, proj, proj, state)


def _s5_output_gate(y, z, wglu_ref, bglu_ref):
    y = jax.nn.gelu(y)
    g = jnp.dot(y.astype(BF16), wglu_ref[...], preferred_element_type=F32) + bglu_ref[...]
    return y * jax.nn.sigmoid(g) * jax.nn.silu(z.astype(F32))


def _s5_tile(u_ref, z_ref, wb_ref, cw_ref, ar_ref, ai_ref, d_ref, wglu_ref, bglu_ref,
             o_ref, sr_ref, si_ref, y_scr, x_scr, *, sub, filler=(), n_filler=0):
    n_sub = len(x_scr) // 2
    xr_scr, xi_scr = x_scr[:n_sub], x_scr[n_sub:]
    strided = lambda j: pl.ds(j, sub, stride=S5_PITCH)
    filler = iter(filler)
    progress = [0, 0]
    n_slots = n_sub * S5_PAIRS

    def fill():
        progress[0] += 1
        while progress[1] < n_filler and progress[1] * n_slots < progress[0] * n_filler:
            next(filler)
            progress[1] += 1

    for k in range(n_sub):
        rows = slice(k * sub, (k + 1) * sub)
        for blk in range(S5_BLOCKS):
            ub = u_ref[rows, blk * LANES:(blk + 1) * LANES]
            for j in range(blk * PAIRS_PER_BLOCK, (blk + 1) * PAIRS_PER_BLOCK):
                bu = jnp.dot(ub, wb_ref[j], preferred_element_type=F32)
                xr_scr[k][strided(j), :] = bu[:, :LANES]
                xi_scr[k][strided(j), :] = bu[:, LANES:]
                fill()
    ar = ar_ref[...]
    ai = ai_ref[...]
    xr, xi = sr_ref[0], si_ref[0]
    for k in range(n_sub):
        rows = slice(k * sub, (k + 1) * sub)
        for t in range(sub):
            step = slice(t * S5_PITCH, t * S5_PITCH + S5_PAIRS)
            xr, xi = (ar * xr - ai * xi + xr_scr[k][step, :],
                      ar * xi + ai * xr + xi_scr[k][step, :])
            xr_scr[k][step, :] = xr
            xi_scr[k][step, :] = xi
        for blk in range(S5_BLOCKS):
            cols = slice(blk * LANES, (blk + 1) * LANES)
            acc = d_ref[:, cols] * u_ref[rows, cols].astype(F32)
            for j in range(blk * PAIRS_PER_BLOCK, (blk + 1) * PAIRS_PER_BLOCK):
                x = jnp.concatenate([xr_scr[k][strided(j), :], xi_scr[k][strided(j), :]], axis=1)
                acc = acc + lax.dot_general(x.astype(BF16), cw_ref[j], (((1,), (1,)), ((), ())),
                                            preferred_element_type=F32)
            y_scr[rows, cols] = acc
        o_ref[rows, :] = _s5_output_gate(
            y_scr[rows, :], z_ref[rows, :], wglu_ref, bglu_ref).astype(BF16)
    sr_ref[0] = xr
    si_ref[0] = xi


def _mixers_kernel(q_ref, k_ref, v_ref, za_ref, u_ref, zb_ref, wb_ref, cw_ref, ar_ref, ai_ref, d_ref,
                   wglu_ref, bglu_ref, ya_ref, ret_ref, yb_ref, sr_ref, si_ref, y_scr, *x_scr, sub):
    @pl.when(pl.program_id(1) == 0)
    def _():
        ret_ref[...] = jnp.zeros_like(ret_ref)
        sr_ref[...] = jnp.zeros_like(sr_ref)
        si_ref[...] = jnp.zeros_like(si_ref)

    retention_units = RET_HEADS * (q_ref.shape[0] // RET_CHUNK)
    _s5_tile(u_ref, zb_ref, wb_ref, cw_ref, ar_ref, ai_ref, d_ref, wglu_ref, bglu_ref,
             yb_ref, sr_ref, si_ref, y_scr, x_scr, sub=sub,
             filler=_retention_tile(q_ref, k_ref, v_ref, za_ref, ya_ref, ret_ref),
             n_filler=retention_units)


def _mixers_prompt(proj, wb, cw, ar, ai, d, wglu, bglu, *, batch, seq, tm, sub):
    nt = seq // tm
    row_block = lambda b, i: b * nt + i
    col_spec = lambda cb: pl.BlockSpec((tm, RET_WIDTH), lambda b, i: (row_block(b, i), cb))
    s5_state_spec = pl.BlockSpec((1, S5_PAIRS, LANES), lambda b, i: (b, 0, 0))
    assert RET_WIDTH == S5_WIDTH
    return pl.pallas_call(
        functools.partial(_mixers_kernel, sub=sub),
        grid=(batch, nt),
        in_specs=[
            col_spec(0), col_spec(1), col_spec(2), col_spec(3), col_spec(4), col_spec(5),
            _resident(wb.shape), _resident(cw.shape), _resident(ar.shape), _resident(ai.shape),
            _resident(d.shape), _resident(wglu.shape), _resident(bglu.shape),
        ],
        out_specs=[
            pl.BlockSpec((tm, RET_WIDTH), lambda b, i: (row_block(b, i), 0)),
            pl.BlockSpec((1, RET_HEADS, HEAD_DIM, HEAD_DIM), lambda b, i: (b, 0, 0, 0)),
            pl.BlockSpec((tm, S5_WIDTH), lambda b, i: (row_block(b, i), 0)),
            s5_state_spec, s5_state_spec,
        ],
        out_shape=[
            jax.ShapeDtypeStruct((batch * seq, RET_WIDTH), BF16),
            jax.ShapeDtypeStruct((batch, RET_HEADS, HEAD_DIM, HEAD_DIM), F32),
            jax.ShapeDtypeStruct((batch * seq, S5_WIDTH), BF16),
            jax.ShapeDtypeStruct((batch, S5_PAIRS, LANES), F32),
            jax.ShapeDtypeStruct((batch, S5_PAIRS, LANES), F32),
        ],
        scratch_shapes=[pltpu.VMEM((tm, S5_WIDTH), F32)]
        + [pltpu.VMEM((sub * S5_PITCH, LANES), F32)] * (2 * (tm // sub)),
        compiler_params=_params("arbitrary", "arbitrary"),
        name="mixers_prompt",
    )(proj, proj, proj, proj, proj, proj, wb, cw, ar, ai, d, wglu, bglu)


def _s5_step_kernel(u_ref, z_ref, x0r_ref, x0i_ref, wb_ref, cw_ref, ar_ref, ai_ref, d_ref,
                    wglu_ref, bglu_ref, o_ref, sr_ref, si_ref, y_scr):
    for blk in range(S5_BLOCKS):
        cols = slice(blk * LANES, (blk + 1) * LANES)
        ub = u_ref[:, cols]
        acc = d_ref[:, cols] * ub.astype(F32)
        for j in range(blk * PAIRS_PER_BLOCK, (blk + 1) * PAIRS_PER_BLOCK):
            pc = slice(j * LANES, (j + 1) * LANES)
            bu = jnp.dot(ub, wb_ref[j], preferred_element_type=F32)
            ar = ar_ref[j:j + 1, :]
            ai = ai_ref[j:j + 1, :]
            x0r = x0r_ref[:, pc]
            x0i = x0i_ref[:, pc]
            nr = ar * x0r - ai * x0i + bu[:, :LANES]
            ni = ar * x0i + ai * x0r + bu[:, LANES:]
            sr_ref[:, pc] = nr
            si_ref[:, pc] = ni
            x = jnp.concatenate([nr, ni], axis=1).astype(BF16)
            acc = acc + lax.dot_general(x, cw_ref[j], (((1,), (1,)), ((), ())),
                                        preferred_element_type=F32)
        y_scr[:, cols] = acc
    o_ref[...] = _s5_output_gate(y_scr[...], z_ref[...], wglu_ref, bglu_ref).astype(BF16)


def _s5_sample(proj, x0r, x0i, wb, cw, ar, ai, d, wglu, bglu):
    nb = proj.shape[0]
    ub_col = (4 * RET_WIDTH) // S5_WIDTH
    full = lambda a: pl.BlockSpec(a.shape, lambda i: (0,) * a.ndim)
    return pl.pallas_call(
        _s5_step_kernel,
        grid=(1,),
        in_specs=[
            pl.BlockSpec((nb, S5_WIDTH), lambda i: (0, ub_col)),
            pl.BlockSpec((nb, S5_WIDTH), lambda i: (0, ub_col + 1)),
            full(x0r), full(x0i), full(wb), full(cw), full(ar), full(ai), full(d), full(wglu), full(bglu),
        ],
        out_specs=[
            pl.BlockSpec((nb, S5_WIDTH), lambda i: (0, 0)),
            full(x0r), full(x0i),
        ],
        out_shape=[
            jax.ShapeDtypeStruct((nb, S5_WIDTH), BF16),
            jax.ShapeDtypeStruct(x0r.shape, F32),
            jax.ShapeDtypeStruct(x0i.shape, F32),
        ],
        scratch_shapes=[pltpu.VMEM((nb, S5_WIDTH), F32)],
        compiler_params=_params("arbitrary"),
        name="s5_sample",
    )(proj, proj, x0r, x0i, wb, cw, ar, ai, d, wglu, bglu)


def _merge_kernel(ya_ref, yb_ref, ga_ref, gb_ref, x_ref, wpa_ref, wpb_ref, wout_ref, gpost_ref, o_ref,
                  merged_scr):
    slabs = [slice(s * MXU_DIM, (s + 1) * MXU_DIM) for s in range(D_MODEL // MXU_DIM)]
    ya_in = ya_ref[...].astype(BF16)
    yb_in = yb_ref[...].astype(BF16)
    for cols in slabs:
        ya = jnp.dot(ya_in, wpa_ref[:, cols], preferred_element_type=F32)
        yb = jnp.dot(yb_in, wpb_ref[:, cols], preferred_element_type=F32)
        merged_scr[:, cols] = (jax.nn.sigmoid(ga_ref[:, cols].astype(F32)) * ya
                               + jax.nn.sigmoid(gb_ref[:, cols].astype(F32)) * yb).astype(BF16)
    sq = jnp.zeros((o_ref.shape[0], 1), F32)
    for cols in slabs:
        out = jnp.dot(merged_scr[...], wout_ref[:, cols], preferred_element_type=F32)
        sq = sq + jnp.sum(out * out, axis=-1, keepdims=True)
        o_ref[:, cols] = out
    inv_rms = lax.rsqrt(sq * (1.0 / D_MODEL) + EPS)
    for cols in slabs:
        o_ref[:, cols] = x_ref[:, cols] + o_ref[:, cols] * inv_rms * gpost_ref[:, cols]


def _merge(ya, yb, proj, x2d, wpa, wpb, wout, gpost, *, tm):
    m = x2d.shape[0]
    ga_col = (4 * RET_WIDTH + 2 * S5_WIDTH) // D_MODEL
    return pl.pallas_call(
        _merge_kernel,
        grid=(m // tm,),
        in_specs=[
            pl.BlockSpec((tm, RET_WIDTH), lambda i: (i, 0)),
            pl.BlockSpec((tm, S5_WIDTH), lambda i: (i, 0)),
            pl.BlockSpec((tm, D_MODEL), lambda i: (i, ga_col)),
            pl.BlockSpec((tm, D_MODEL), lambda i: (i, ga_col + 1)),
            pl.BlockSpec((tm, D_MODEL), lambda i: (i, 0)),
            _resident(wpa.shape), _resident(wpb.shape), _resident(wout.shape), _resident(gpost.shape),
        ],
        out_specs=pl.BlockSpec((tm, D_MODEL), lambda i: (i, 0)),
        out_shape=jax.ShapeDtypeStruct((m, D_MODEL), F32),
        scratch_shapes=[pltpu.VMEM((tm, D_MODEL), BF16)],
        compiler_params=_params("arbitrary"),
        name="merge_out",
    )(ya, yb, proj, proj, x2d, wpa, wpb, wout, gpost)


def _rope_tables(pos):
    half = HEAD_DIM // 2
    inv = ROPE_BASE ** (-jnp.arange(half, dtype=F32) / half)
    ang = pos.astype(F32)[:, None] * inv[None, :]
    cos, sin = jnp.cos(ang), jnp.sin(ang)
    return jnp.concatenate([cos, cos], axis=-1), jnp.concatenate([-sin, sin], axis=-1)


def _s5_discretize(lam_re, lam_im, log_dt, b_re, b_im):
    dt = jnp.exp(log_dt)[:, None]
    mag = jnp.exp(lam_re * dt)
    abar_r = mag * jnp.cos(lam_im * dt)
    abar_i = mag * jnp.sin(lam_im * dt)
    nr, ni = abar_r - 1.0, abar_i
    den = lam_re * lam_re + lam_im * lam_im
    coef_r = (nr * lam_re + ni * lam_im) / den
    coef_i = (ni * lam_re - nr * lam_im) / den
    bbar_r = coef_r[:, :, None] * b_re - coef_i[:, :, None] * b_im
    bbar_i = coef_r[:, :, None] * b_im + coef_i[:, :, None] * b_re
    return abar_r, abar_i, bbar_r, bbar_i


def _s5_pair_weights(bbar_r, bbar_i, c_re, c_im):
    j = lax.broadcasted_iota(jnp.int32, (S5_PAIRS, LANES, 2 * LANES), 0)
    row = lax.broadcasted_iota(jnp.int32, (S5_PAIRS, LANES, 2 * LANES), 1)
    col = lax.broadcasted_iota(jnp.int32, (S5_PAIRS, LANES, 2 * LANES), 2)
    pair_lanes = 2 * S5_GROUP
    keep = ((row // pair_lanes == j % PAIRS_PER_BLOCK)
            & ((row // S5_GROUP) % 2 == (col // S5_P) % 2))

    def expand(re, im, perm):
        base = jnp.stack([re, im]).reshape((2, S5_PAIRS, 2) + re.shape[1:]).transpose(perm)
        base = base.reshape(S5_PAIRS, 1, S5_GROUP, 2 * LANES)
        tiled = jnp.broadcast_to(base, (S5_PAIRS, LANES // S5_GROUP, S5_GROUP, 2 * LANES))
        return jnp.where(keep, tiled.reshape(S5_PAIRS, LANES, 2 * LANES), 0.0).astype(BF16)

    wb = expand(bbar_r, bbar_i, (1, 4, 0, 2, 3))
    cw = expand(c_re, -c_im, (1, 3, 0, 2, 4))
    return wb, cw


def kernel(x_prompt, x_sample, state_ret, state_s5_re, state_s5_im, g_pre, w_in, w_pa, w_pb, w_out, g_post,
           s5_lam_re, s5_lam_im, s5_log_dt, s5_b_re, s5_b_im, s5_c_re, s5_c_im, s5_d, s5_w_glu, s5_b_glu):
    assert w_in.shape[0] == 1, "single trunk layer"
    bp, lp, _ = x_prompt.shape
    bs, ls, _ = x_sample.shape
    assert ls == 1 and lp % RET_CHUNK == 0

    b_glu = s5_b_glu[0].reshape(1, S5_WIDTH)
    gpre = g_pre[0].reshape(1, D_MODEL)
    gpost = g_post[0].reshape(1, D_MODEL)

    abar_r, abar_i, bbar_r, bbar_i = _s5_discretize(
        s5_lam_re[0], s5_lam_im[0], s5_log_dt[0], s5_b_re[0], s5_b_im[0])
    wb, cw = _s5_pair_weights(bbar_r, bbar_i, s5_c_re[0], s5_c_im[0])
    ar = abar_r.reshape(S5_PAIRS, LANES)
    ai = abar_i.reshape(S5_PAIRS, LANES)
    d_skip = s5_d[0].reshape(1, S5_WIDTH)

    xs = x_sample.reshape(bs, D_MODEL)
    cos_s, sin_s = _rope_tables(jnp.full((bs,), PAST_LEN, jnp.int32))
    proj_s, w_in_b = _inproj(xs, gpre, w_in[0], cos_s, sin_s, tm=bs, tn=INPROJ_COLS_SAMPLE)

    xp = x_prompt.reshape(bp * lp, D_MODEL)
    cos_p, sin_p = _rope_tables(jnp.arange(lp, dtype=jnp.int32))
    proj_p, w_pa_b, w_pb_b, w_out_b, w_glu_b = _inproj(
        xp, gpre, w_in_b, cos_p, sin_p, tm=INPROJ_ROWS, tn=INPROJ_COLS_PROMPT,
        side_casts=(w_pa[0], w_pb[0], w_out[0], s5_w_glu[0]))
    ya_p, ret_p, yb_p, s5r_p, s5i_p = _mixers_prompt(
        proj_p, wb, cw, ar, ai, d_skip, w_glu_b, b_glu, batch=bp, seq=lp, tm=MIXER_ROWS,
        sub=S5_SUBTILE_ROWS)
    y_p = _merge(ya_p, yb_p, proj_p, xp, w_pa_b, w_pb_b, w_out_b, gpost, tm=MERGE_ROWS)

    ya_s, ret_s = _retention_sample(proj_s, state_ret[0], bb=RET_SAMPLE_SEQS)
    x0r = state_s5_re[0].reshape(bs, S5_GROUPS * S5_P)
    x0i = state_s5_im[0].reshape(bs, S5_GROUPS * S5_P)
    yb_s, s5r_s, s5i_s = _s5_sample(proj_s, x0r, x0i, wb, cw, ar, ai, d_skip, w_glu_b, b_glu)
    y_s = _merge(ya_s, yb_s, proj_s, xs, w_pa_b, w_pb_b, w_out_b, gpost, tm=bs)

    state_shape = (1, -1, S5_GROUPS, S5_P)
    return (y_p.reshape(bp, lp, D_MODEL), y_s.reshape(bs, 1, D_MODEL),
            ret_p[None], s5r_p.reshape(state_shape), s5i_p.reshape(state_shape),
            ret_s[None], s5r_s.reshape(state_shape), s5i_s.reshape(state_shape))
```

```python
import functools
import math

import jax
import jax.numpy as jnp
import numpy as np
from jax import lax
from jax.experimental import pallas as pl
from jax.experimental.pallas import tpu as pltpu

F32 = jnp.float32
BF16 = jnp.bfloat16

D_MODEL = 2048
RET_WIDTH = 1024
RET_HEADS = 8
HEAD_DIM = RET_WIDTH // RET_HEADS
RET_CHUNK = 128
ROPE_BASE = 10000.0
S5_WIDTH = 1024
S5_GROUP = 16
S5_GROUPS = S5_WIDTH // S5_GROUP
S5_P = 64
PAST_LEN = 16384
EPS = 1e-6
IN_COLS = 4 * RET_WIDTH + 2 * S5_WIDTH + 2 * D_MODEL

LANES = 128
MXU_DIM = 256
BF16_SUBLANES = 16
S5_PAIRS = S5_GROUPS // 2
PAIRS_PER_BLOCK = LANES // (2 * S5_GROUP)
S5_BLOCKS = S5_WIDTH // LANES
S5_PITCH = 36
VMEM_LIMIT = 56 * 1024 * 1024

INPROJ_ROWS = 1024
INPROJ_COLS_PROMPT = 2048
INPROJ_COLS_SAMPLE = 1024
MIXER_ROWS = 512
S5_SUBTILE_ROWS = 128
MERGE_ROWS = 512
RET_SAMPLE_SEQS = 16

_LOG_GAMMA = [float(np.log(np.float32(1.0) - np.float32(2.0) ** np.float32(-5.0 - h)))
              for h in range(RET_HEADS)]


def _params(*sem):
    return pltpu.CompilerParams(dimension_semantics=sem, vmem_limit_bytes=VMEM_LIMIT)


def _resident(shape):
    nd = len(shape)
    return pl.BlockSpec(shape, lambda *_: (0,) * nd, pipeline_mode=pl.Buffered(1))


def _rope(x, cos, sin_signed):
    return x * cos + pltpu.roll(x, HEAD_DIM // 2, 1) * sin_signed


def _inproj_kernel(x_ref, g_ref, w_ref, cos_ref, sin_ref, *rest, slab, tn, emit_weights, n_side):
    side_in, rest = rest[:n_side], rest[n_side:]
    o_ref, h_ref = rest[0], rest[-1]
    wcopy_ref = rest[1] if emit_weights else None
    side_out = rest[len(rest) - 1 - n_side:-1]
    j = pl.program_id(1)

    @pl.when(j == 0)
    def _():
        g = g_ref[...]

        def body(r, _):
            rows = pl.ds(pl.multiple_of(r * slab, slab), slab)
            x = x_ref[rows, :]
            ms = jnp.mean(x * x, axis=-1, keepdims=True)
            h_ref[rows, :] = (x * lax.rsqrt(ms + EPS) * g).astype(BF16)
            return 0

        lax.fori_loop(0, x_ref.shape[0] // slab, body, 0)

    def rotate(s, acc):
        is_k = j * tn + s * MXU_DIM >= RET_WIDTH
        scale = jnp.where(is_k, HEAD_DIM ** -0.5, 1.0).astype(F32)
        cos = cos_ref[...] * scale
        sin = sin_ref[...] * scale
        return jnp.concatenate([_rope(acc[:, hh * HEAD_DIM:(hh + 1) * HEAD_DIM], cos, sin)
                                for hh in range(MXU_DIM // HEAD_DIM)], axis=1)

    epilogues = {"rope": rotate, "plain": lambda s, acc: acc}
    col_kinds = (("rope", 2 * RET_WIDTH), ("plain", IN_COLS - 2 * RET_WIDTH))
    slab_kinds = [kind for kind, width in col_kinds for _ in range(width // MXU_DIM)]
    per_tile = tn // MXU_DIM
    tiles_by_pattern = {}
    for tile in range(len(slab_kinds) // per_tile):
        pattern = tuple(slab_kinds[tile * per_tile:(tile + 1) * per_tile])
        tiles_by_pattern.setdefault(pattern, []).append(tile)

    def slabs(pattern):
        for src_ref, dst_ref in zip(side_in, side_out):
            dst_ref[...] = src_ref[...].astype(BF16)
        for s, kind in enumerate(pattern):
            cols = slice(s * MXU_DIM, (s + 1) * MXU_DIM)
            w = w_ref[:, cols]
            if wcopy_ref is not None:
                w = w.astype(BF16)
                wcopy_ref[:, cols] = w
            acc = jnp.dot(h_ref[...], w, preferred_element_type=F32)
            o_ref[:, cols] = epilogues[kind](s, acc).astype(BF16)

    for pattern, tiles in tiles_by_pattern.items():
        is_this_kind = functools.reduce(jnp.logical_or, [j == tile for tile in tiles])
        pl.when(is_this_kind)(functools.partial(slabs, pattern))


def _inproj(x2d, g_pre, w_in, cos, sin, *, tm, tn, side_casts=()):
    m, d = x2d.shape
    n = w_in.shape[1]
    pos_tiles = cos.shape[0] // tm
    emit_weights = w_in.dtype != BF16
    assert not emit_weights or m == tm, "each weight tile must be visited once to be copied out"
    n_col_tiles = n // tn
    n_steps = (m // tm) * n_col_tiles
    in_specs = [
        pl.BlockSpec((tm, d), lambda i, j: (i, 0)),
        pl.BlockSpec((1, d), lambda i, j: (0, 0)),
        pl.BlockSpec((d, tn), lambda i, j: (0, j)),
        pl.BlockSpec((tm, HEAD_DIM), lambda i, j: (i % pos_tiles, 0)),
        pl.BlockSpec((tm, HEAD_DIM), lambda i, j: (i % pos_tiles, 0)),
    ]
    out_specs = [pl.BlockSpec((tm, tn), lambda i, j: (i, j))]
    out_shape = [jax.ShapeDtypeStruct((m, n), BF16)]
    if emit_weights:
        out_specs.append(pl.BlockSpec((d, tn), lambda i, j: (0, j)))
        out_shape.append(jax.ShapeDtypeStruct((d, n), BF16))
    for a in side_casts:
        rows = -(-a.shape[0] // n_steps)
        rows = -(-rows // BF16_SUBLANES) * BF16_SUBLANES
        assert a.shape[0] % rows == 0
        last = a.shape[0] // rows - 1
        spec = pl.BlockSpec((rows, a.shape[1]),
                            lambda i, j, last=last: (jnp.minimum(i * n_col_tiles + j, last), 0))
        in_specs.append(spec)
        out_specs.append(spec)
        out_shape.append(jax.ShapeDtypeStruct(a.shape, BF16))
    return pl.pallas_call(
        functools.partial(_inproj_kernel, slab=min(tm, 128), tn=tn, emit_weights=emit_weights,
                          n_side=len(side_casts)),
        grid=(m // tm, n_col_tiles),
        in_specs=in_specs,
        out_specs=out_specs,
        out_shape=out_shape,
        scratch_shapes=[pltpu.VMEM((tm, d), BF16)],
        compiler_params=_params("arbitrary", "arbitrary"),
        name="inproj",
    )(x2d, g_pre, w_in, cos, sin, *side_casts)


def _group_norm(o):
    mu = jnp.mean(o, axis=-1, keepdims=True)
    d = o - mu
    var = jnp.mean(d * d, axis=-1, keepdims=True)
    return d * lax.rsqrt(var + EPS)


def _retention_tile(q_ref, k_ref, v_ref, z_ref, o_ref, s_ref):
    c = RET_CHUNK
    row = lax.broadcasted_iota(jnp.int32, (c, c), 0).astype(F32)
    col = lax.broadcasted_iota(jnp.int32, (c, c), 1).astype(F32)
    diff = row - col
    for h in range(RET_HEADS):
        lg = _LOG_GAMMA[h]
        mask = jnp.where(diff >= 0, jnp.exp(jnp.maximum(diff, 0.0) * lg), 0.0)
        q_decay = jnp.exp((row + 1.0) * lg)
        k_decay = jnp.exp((c - 1.0 - row) * lg)
        chunk_decay = math.exp(c * lg)
        cols = slice(h * HEAD_DIM, (h + 1) * HEAD_DIM)
        for ci in range(q_ref.shape[0] // c):
            rows = slice(ci * c, (ci + 1) * c)
            q = q_ref[rows, cols]
            k = k_ref[rows, cols]
            v = v_ref[rows, cols]
            s0 = s_ref[0, h]
            scores = lax.dot_general(q, k, (((1,), (1,)), ((), ())),
                                     preferred_element_type=F32) * mask
            inner = jnp.dot(scores.astype(BF16), v, preferred_element_type=F32)
            cross = jnp.dot(q, s0.astype(BF16), preferred_element_type=F32) * q_decay
            kd = (k.astype(F32) * k_decay).astype(BF16)
            s_ref[0, h] = chunk_decay * s0 + lax.dot_general(
                kd, v, (((0,), (0,)), ((), ())), preferred_element_type=F32)
            o = _group_norm(inner + cross)
            o_ref[rows, cols] = (o * jax.nn.silu(z_ref[rows, cols].astype(F32))).astype(BF16)
            yield


def _retention_step_kernel(q_ref, k_ref, v_ref, z_ref, s_ref, o_ref, sn_ref, acc_ref, *, bb):
    pad = jnp.zeros((HEAD_DIM - bb, HEAD_DIM), F32)
    for h in range(RET_HEADS):
        gamma = math.exp(_LOG_GAMMA[h])
        cols = slice(h * HEAD_DIM, (h + 1) * HEAD_DIM)
        q = q_ref[:, cols].astype(F32)
        k = k_ref[:, cols].astype(F32)
        v = v_ref[:, cols].astype(F32)
        qt = jnp.concatenate([q, pad], axis=0).T
        kt = jnp.concatenate([k, pad], axis=0).T
        for b in range(bb):
            s_new = gamma * s_ref[b, h] + kt[:, b:b + 1] * v[b:b + 1, :]
            sn_ref[b, h] = s_new
            acc_ref[b:b + 1, cols] = jnp.sum(qt[:, b:b + 1] * s_new, axis=0, keepdims=True)
    for h in range(RET_HEADS):
        cols = slice(h * HEAD_DIM, (h + 1) * HEAD_DIM)
        o_ref[:, cols] = _group_norm(acc_ref[:, cols]) * jax.nn.silu(z_ref[:, cols].astype(F32))


def _retention_sample(proj, state, *, bb):
    nb = state.shape[0]
    col_spec = lambda cb: pl.BlockSpec((bb, RET_WIDTH), lambda i: (i, cb))
    state_spec = pl.BlockSpec((bb, RET_HEADS, HEAD_DIM, HEAD_DIM), lambda i: (i, 0, 0, 0))
    return pl.pallas_call(
        functools.partial(_retention_step_kernel, bb=bb),
        grid=(nb // bb,),
        in_specs=[col_spec(0), col_spec(1), col_spec(2), col_spec(3), state_spec],
        out_specs=[pl.BlockSpec((bb, RET_WIDTH), lambda i: (i, 0)), state_spec],
        out_shape=[
            jax.ShapeDtypeStruct((nb, RET_WIDTH), F32),
            jax.ShapeDtypeStruct(state.shape, F32),
        ],
        scratch_shapes=[pltpu.VMEM((bb, RET_WIDTH), F32)],
        compiler_params=_params("arbitrary"),
        name="retention_sample",
    )(proj, proj, proj, proj, state)


def _s5_output_gate(y, z, wglu_ref, bglu_ref):
    y = jax.nn.gelu(y)
    g = jnp.dot(y.astype(BF16), wglu_ref[...], preferred_element_type=F32) + bglu_ref[...]
    return y * jax.nn.sigmoid(g) * jax.nn.silu(z.astype(F32))


def _s5_tile(u_ref, z_ref, wb_ref, cw_ref, ar_ref, ai_ref, d_ref, wglu_ref, bglu_ref,
             o_ref, sr_ref, si_ref, y_scr, x_scr, *, sub, filler=(), n_filler=0):
    n_sub = len(x_scr) // 2
    xr_scr, xi_scr = x_scr[:n_sub], x_scr[n_sub:]
    strided = lambda j: pl.ds(j, sub, stride=S5_PITCH)
    filler = iter(filler)
    progress = [0, 0]
    n_slots = n_sub * S5_PAIRS

    def fill():
        progress[0] += 1
        while progress[1] < n_filler and progress[1] * n_slots < progress[0] * n_filler:
            next(filler)
            progress[1] += 1

    for k in range(n_sub):
        rows = slice(k * sub, (k + 1) * sub)
        for blk in range(S5_BLOCKS):
            ub = u_ref[rows, blk * LANES:(blk + 1) * LANES]
            for j in range(blk * PAIRS_PER_BLOCK, (blk + 1) * PAIRS_PER_BLOCK):
                bu = jnp.dot(ub, wb_ref[j], preferred_element_type=F32)
                xr_scr[k][strided(j), :] = bu[:, :LANES]
                xi_scr[k][strided(j), :] = bu[:, LANES:]
                fill()
    ar = ar_ref[...]
    ai = ai_ref[...]
    xr, xi = sr_ref[0], si_ref[0]
    for k in range(n_sub):
        rows = slice(k * sub, (k + 1) * sub)
        for t in range(sub):
            step = slice(t * S5_PITCH, t * S5_PITCH + S5_PAIRS)
            xr, xi = (ar * xr - ai * xi + xr_scr[k][step, :],
                      ar * xi + ai * xr + xi_scr[k][step, :])
            xr_scr[k][step, :] = xr
            xi_scr[k][step, :] = xi
        for blk in range(S5_BLOCKS):
            cols = slice(blk * LANES, (blk + 1) * LANES)
            acc = d_ref[:, cols] * u_ref[rows, cols].astype(F32)
            for j in range(blk * PAIRS_PER_BLOCK, (blk + 1) * PAIRS_PER_BLOCK):
                x = jnp.concatenate([xr_scr[k][strided(j), :], xi_scr[k][strided(j), :]], axis=1)
                acc = acc + lax.dot_general(x.astype(BF16), cw_ref[j], (((1,), (1,)), ((), ())),
                                            preferred_element_type=F32)
            y_scr[rows, cols] = acc
        o_ref[rows, :] = _s5_output_gate(
            y_scr[rows, :], z_ref[rows, :], wglu_ref, bglu_ref).astype(BF16)
    sr_ref[0] = xr
    si_ref[0] = xi


def _mixers_kernel(q_ref, k_ref, v_ref, za_ref, u_ref, zb_ref, wb_ref, cw_ref, ar_ref, ai_ref, d_ref,
                   wglu_ref, bglu_ref, ya_ref, ret_ref, yb_ref, sr_ref, si_ref, y_scr, *x_scr, sub):
    @pl.when(pl.program_id(1) == 0)
    def _():
        ret_ref[...] = jnp.zeros_like(ret_ref)
        sr_ref[...] = jnp.zeros_like(sr_ref)
        si_ref[...] = jnp.zeros_like(si_ref)

    retention_units = RET_HEADS * (q_ref.shape[0] // RET_CHUNK)
    _s5_tile(u_ref, zb_ref, wb_ref, cw_ref, ar_ref, ai_ref, d_ref, wglu_ref, bglu_ref,
             yb_ref, sr_ref, si_ref, y_scr, x_scr, sub=sub,
             filler=_retention_tile(q_ref, k_ref, v_ref, za_ref, ya_ref, ret_ref),
             n_filler=retention_units)


def _mixers_prompt(proj, wb, cw, ar, ai, d, wglu, bglu, *, batch, seq, tm, sub):
    nt = seq // tm
    row_block = lambda b, i: b * nt + i
    col_spec = lambda cb: pl.BlockSpec((tm, RET_WIDTH), lambda b, i: (row_block(b, i), cb))
    s5_state_spec = pl.BlockSpec((1, S5_PAIRS, LANES), lambda b, i: (b, 0, 0))
    assert RET_WIDTH == S5_WIDTH
    return pl.pallas_call(
        functools.partial(_mixers_kernel, sub=sub),
        grid=(batch, nt),
        in_specs=[
            col_spec(0), col_spec(1), col_spec(2), col_spec(3), col_spec(4), col_spec(5),
            _resident(wb.shape), _resident(cw.shape), _resident(ar.shape), _resident(ai.shape),
            _resident(d.shape), _resident(wglu.shape), _resident(bglu.shape),
        ],
        out_specs=[
            pl.BlockSpec((tm, RET_WIDTH), lambda b, i: (row_block(b, i), 0)),
            pl.BlockSpec((1, RET_HEADS, HEAD_DIM, HEAD_DIM), lambda b, i: (b, 0, 0, 0)),
            pl.BlockSpec((tm, S5_WIDTH), lambda b, i: (row_block(b, i), 0)),
            s5_state_spec, s5_state_spec,
        ],
        out_shape=[
            jax.ShapeDtypeStruct((batch * seq, RET_WIDTH), BF16),
            jax.ShapeDtypeStruct((batch, RET_HEADS, HEAD_DIM, HEAD_DIM), F32),
            jax.ShapeDtypeStruct((batch * seq, S5_WIDTH), BF16),
            jax.ShapeDtypeStruct((batch, S5_PAIRS, LANES), F32),
            jax.ShapeDtypeStruct((batch, S5_PAIRS, LANES), F32),
        ],
        scratch_shapes=[pltpu.VMEM((tm, S5_WIDTH), F32)]
        + [pltpu.VMEM((sub * S5_PITCH, LANES), F32)] * (2 * (tm // sub)),
        compiler_params=_params("arbitrary", "arbitrary"),
        name="mixers_prompt",
    )(proj, proj, proj, proj, proj, proj, wb, cw, ar, ai, d, wglu, bglu)


def _s5_step_kernel(u_ref, z_ref, x0r_ref, x0i_ref, wb_ref, cw_ref, ar_ref, ai_ref, d_ref,
                    wglu_ref, bglu_ref, o_ref, sr_ref, si_ref, y_scr):
    for blk in range(S5_BLOCKS):
        cols = slice(blk * LANES, (blk + 1) * LANES)
        ub = u_ref[:, cols]
        acc = d_ref[:, cols] * ub.astype(F32)
        for j in range(blk * PAIRS_PER_BLOCK, (blk + 1) * PAIRS_PER_BLOCK):
            pc = slice(j * LANES, (j + 1) * LANES)
            bu = jnp.dot(ub, wb_ref[j], preferred_element_type=F32)
            ar = ar_ref[j:j + 1, :]
            ai = ai_ref[j:j + 1, :]
            x0r = x0r_ref[:, pc]
            x0i = x0i_ref[:, pc]
            nr = ar * x0r - ai * x0i + bu[:, :LANES]
            ni = ar * x0i + ai * x0r + bu[:, LANES:]
            sr_ref[:, pc] = nr
            si_ref[:, pc] = ni
            x = jnp.concatenate([nr, ni], axis=1).astype(BF16)
            acc = acc + lax.dot_general(x, cw_ref[j], (((1,), (1,)), ((), ())),
                                        preferred_element_type=F32)
        y_scr[:, cols] = acc
    o_ref[...] = _s5_output_gate(y_scr[...], z_ref[...], wglu_ref, bglu_ref).astype(BF16)


def _s5_sample(proj, x0r, x0i, wb, cw, ar, ai, d, wglu, bglu):
    nb = proj.shape[0]
    ub_col = (4 * RET_WIDTH) // S5_WIDTH
    full = lambda a: pl.BlockSpec(a.shape, lambda i: (0,) * a.ndim)
    return pl.pallas_call(
        _s5_step_kernel,
        grid=(1,),
        in_specs=[
            pl.BlockSpec((nb, S5_WIDTH), lambda i: (0, ub_col)),
            pl.BlockSpec((nb, S5_WIDTH), lambda i: (0, ub_col + 1)),
            full(x0r), full(x0i), full(wb), full(cw), full(ar), full(ai), full(d), full(wglu), full(bglu),
        ],
        out_specs=[
            pl.BlockSpec((nb, S5_WIDTH), lambda i: (0, 0)),
            full(x0r), full(x0i),
        ],
        out_shape=[
            jax.ShapeDtypeStruct((nb, S5_WIDTH), BF16),
            jax.ShapeDtypeStruct(x0r.shape, F32),
            jax.ShapeDtypeStruct(x0i.shape, F32),
        ],
        scratch_shapes=[pltpu.VMEM((nb, S5_WIDTH), F32)],
        compiler_params=_params("arbitrary"),
        name="s5_sample",
    )(proj, proj, x0r, x0i, wb, cw, ar, ai, d, wglu, bglu)


def _merge_kernel(ya_ref, yb_ref, ga_ref, gb_ref, x_ref, wpa_ref, wpb_ref, wout_ref, gpost_ref, o_ref,
                  merged_scr):
    slabs = [slice(s * MXU_DIM, (s + 1) * MXU_DIM) for s in range(D_MODEL // MXU_DIM)]
    ya_in = ya_ref[...].astype(BF16)
    yb_in = yb_ref[...].astype(BF16)
    for cols in slabs:
        ya = jnp.dot(ya_in, wpa_ref[:, cols], preferred_element_type=F32)
        yb = jnp.dot(yb_in, wpb_ref[:, cols], preferred_element_type=F32)
        merged_scr[:, cols] = (jax.nn.sigmoid(ga_ref[:, cols].astype(F32)) * ya
                               + jax.nn.sigmoid(gb_ref[:, cols].astype(F32)) * yb).astype(BF16)
    sq = jnp.zeros((o_ref.shape[0], 1), F32)
    for cols in slabs:
        out = jnp.dot(merged_scr[...], wout_ref[:, cols], preferred_element_type=F32)
        sq = sq + jnp.sum(out * out, axis=-1, keepdims=True)
        o_ref[:, cols] = out
    inv_rms = lax.rsqrt(sq * (1.0 / D_MODEL) + EPS)
    for cols in slabs:
        o_ref[:, cols] = x_ref[:, cols] + o_ref[:, cols] * inv_rms * gpost_ref[:, cols]


def _merge(ya, yb, proj, x2d, wpa, wpb, wout, gpost, *, tm):
    m = x2d.shape[0]
    ga_col = (4 * RET_WIDTH + 2 * S5_WIDTH) // D_MODEL
    return pl.pallas_call(
        _merge_kernel,
        grid=(m // tm,),
        in_specs=[
            pl.BlockSpec((tm, RET_WIDTH), lambda i: (i, 0)),
            pl.BlockSpec((tm, S5_WIDTH), lambda i: (i, 0)),
            pl.BlockSpec((tm, D_MODEL), lambda i: (i, ga_col)),
            pl.BlockSpec((tm, D_MODEL), lambda i: (i, ga_col + 1)),
            pl.BlockSpec((tm, D_MODEL), lambda i: (i, 0)),
            _resident(wpa.shape), _resident(wpb.shape), _resident(wout.shape), _resident(gpost.shape),
        ],
        out_specs=pl.BlockSpec((tm, D_MODEL), lambda i: (i, 0)),
        out_shape=jax.ShapeDtypeStruct((m, D_MODEL), F32),
        scratch_shapes=[pltpu.VMEM((tm, D_MODEL), BF16)],
        compiler_params=_params("arbitrary"),
        name="merge_out",
    )(ya, yb, proj, proj, x2d, wpa, wpb, wout, gpost)


def _rope_tables(pos):
    half = HEAD_DIM // 2
    inv = ROPE_BASE ** (-jnp.arange(half, dtype=F32) / half)
    ang = pos.astype(F32)[:, None] * inv[None, :]
    cos, sin = jnp.cos(ang), jnp.sin(ang)
    return jnp.concatenate([cos, cos], axis=-1), jnp.concatenate([-sin, sin], axis=-1)


def _s5_discretize(lam_re, lam_im, log_dt, b_re, b_im):
    dt = jnp.exp(log_dt)[:, None]
    mag = jnp.exp(lam_re * dt)
    abar_r = mag * jnp.cos(lam_im * dt)
    abar_i = mag * jnp.sin(lam_im * dt)
    nr, ni = abar_r - 1.0, abar_i
    den = lam_re * lam_re + lam_im * lam_im
    coef_r = (nr * lam_re + ni * lam_im) / den
    coef_i = (ni * lam_re - nr * lam_im) / den
    bbar_r = coef_r[:, :, None] * b_re - coef_i[:, :, None] * b_im
    bbar_i = coef_r[:, :, None] * b_im + coef_i[:, :, None] * b_re
    return abar_r, abar_i, bbar_r, bbar_i


def _s5_pair_weights(bbar_r, bbar_i, c_re, c_im):
    j = lax.broadcasted_iota(jnp.int32, (S5_PAIRS, LANES, 2 * LANES), 0)
    row = lax.broadcasted_iota(jnp.int32, (S5_PAIRS, LANES, 2 * LANES), 1)
    col = lax.broadcasted_iota(jnp.int32, (S5_PAIRS, LANES, 2 * LANES), 2)
    pair_lanes = 2 * S5_GROUP
    keep = ((row // pair_lanes == j % PAIRS_PER_BLOCK)
            & ((row // S5_GROUP) % 2 == (col // S5_P) % 2))

    def expand(re, im, perm):
        base = jnp.stack([re, im]).reshape((2, S5_PAIRS, 2) + re.shape[1:]).transpose(perm)
        base = base.reshape(S5_PAIRS, 1, S5_GROUP, 2 * LANES)
        tiled = jnp.broadcast_to(base, (S5_PAIRS, LANES // S5_GROUP, S5_GROUP, 2 * LANES))
        return jnp.where(keep, tiled.reshape(S5_PAIRS, LANES, 2 * LANES), 0.0).astype(BF16)

    wb = expand(bbar_r, bbar_i, (1, 4, 0, 2, 3))
    cw = expand(c_re, -c_im, (1, 3, 0, 2, 4))
    return wb, cw


def kernel(x_prompt, x_sample, state_ret, state_s5_re, state_s5_im, g_pre, w_in, w_pa, w_pb, w_out, g_post,
           s5_lam_re, s5_lam_im, s5_log_dt, s5_b_re, s5_b_im, s5_c_re, s5_c_im, s5_d, s5_w_glu, s5_b_glu):
    assert w_in.shape[0] == 1, "single trunk layer"
    bp, lp, _ = x_prompt.shape
    bs, ls, _ = x_sample.shape
    assert ls == 1 and lp % RET_CHUNK == 0

    b_glu = s5_b_glu[0].reshape(1, S5_WIDTH)
    gpre = g_pre[0].reshape(1, D_MODEL)
    gpost = g_post[0].reshape(1, D_MODEL)

    abar_r, abar_i, bbar_r, bbar_i = _s5_discretize(
        s5_lam_re[0], s5_lam_im[0], s5_log_dt[0], s5_b_re[0], s5_b_im[0])
    wb, cw = _s5_pair_weights(bbar_r, bbar_i, s5_c_re[0], s5_c_im[0])
    ar = abar_r.reshape(S5_PAIRS, LANES)
    ai = abar_i.reshape(S5_PAIRS, LANES)
    d_skip = s5_d[0].reshape(1, S5_WIDTH)

    xs = x_sample.reshape(bs, D_MODEL)
    cos_s, sin_s = _rope_tables(jnp.full((bs,), PAST_LEN, jnp.int32))
    proj_s, w_in_b = _inproj(xs, gpre, w_in[0], cos_s, sin_s, tm=bs, tn=INPROJ_COLS_SAMPLE)

    xp = x_prompt.reshape(bp * lp, D_MODEL)
    cos_p, sin_p = _rope_tables(jnp.arange(lp, dtype=jnp.int32))
    proj_p, w_pa_b, w_pb_b, w_out_b, w_glu_b = _inproj(
        xp, gpre, w_in_b, cos_p, sin_p, tm=INPROJ_ROWS, tn=INPROJ_COLS_PROMPT,
        side_casts=(w_pa[0], w_pb[0], w_out[0], s5_w_glu[0]))
    ya_p, ret_p, yb_p, s5r_p, s5i_p = _mixers_prompt(
        proj_p, wb, cw, ar, ai, d_skip, w_glu_b, b_glu, batch=bp, seq=lp, tm=MIXER_ROWS,
        sub=S5_SUBTILE_ROWS)
    y_p = _merge(ya_p, yb_p, proj_p, xp, w_pa_b, w_pb_b, w_out_b, gpost, tm=MERGE_ROWS)

    ya_s, ret_s = _retention_sample(proj_s, state_ret[0], bb=RET_SAMPLE_SEQS)
    x0r = state_s5_re[0].reshape(bs, S5_GROUPS * S5_P)
    x0i = state_s5_im[0].reshape(bs, S5_GROUPS * S5_P)
    yb_s, s5r_s, s5i_s = _s5_sample(proj_s, x0r, x0i, wb, cw, ar, ai, d_skip, w_glu_b, b_glu)
    y_s = _merge(ya_s, yb_s, proj_s, xs, w_pa_b, w_pb_b, w_out_b, gpost, tm=bs)

    state_shape = (1, -1, S5_GROUPS, S5_P)
    return (y_p.reshape(bp, lp, D_MODEL), y_s.reshape(bs, 1, D_MODEL),
            ret_p[None], s5r_p.reshape(state_shape), s5i_p.reshape(state_shape),
            ret_s[None], s5r_s.reshape(state_shape), s5i_s.reshape(state_shape))
```

```python
import functools
import math
from typing import Callable, NamedTuple

import jax
import jax.numpy as jnp
import numpy as np
from jax import lax
from jax.experimental import pallas as pl
from jax.experimental.pallas import tpu as pltpu

F32 = jnp.float32
BF16 = jnp.bfloat16

D_MODEL = 2048
RET_WIDTH = 1024
RET_HEADS = 8
HEAD_DIM = RET_WIDTH // RET_HEADS
RET_CHUNK = 128
ROPE_BASE = 10000.0
S5_WIDTH = 1024
S5_GROUP = 16
S5_GROUPS = S5_WIDTH // S5_GROUP
S5_P = 64
PAST_LEN = 16384
EPS = 1e-6
IN_COLS = 4 * RET_WIDTH + 2 * S5_WIDTH + 2 * D_MODEL

LANES = 128
MXU_DIM = 256
BF16_SUBLANES = 16
S5_PAIRS = S5_GROUPS // 2
PAIRS_PER_BLOCK = LANES // (2 * S5_GROUP)
S5_BLOCKS = S5_WIDTH // LANES
S5_PITCH = 36
VMEM_LIMIT = 60 * 1024 * 1024

INPROJ_ROWS = 1024
INPROJ_COLS_PROMPT = 2048
INPROJ_COLS_SAMPLE = 1024
MIXER_ROWS = 512
S5_SUBTILE_ROWS = 128
MERGE_ROWS = 512
RET_SAMPLE_SEQS = 4

_LOG_GAMMA = [float(np.log(np.float32(1.0) - np.float32(2.0) ** np.float32(-5.0 - h)))
              for h in range(RET_HEADS)]


def _params(*sem):
    return pltpu.CompilerParams(dimension_semantics=sem, vmem_limit_bytes=VMEM_LIMIT)


def _resident(shape):
    nd = len(shape)
    return pl.BlockSpec(shape, lambda *_: (0,) * nd, pipeline_mode=pl.Buffered(1))


def _rope(x, cos, sin_signed):
    return x * cos + pltpu.roll(x, HEAD_DIM // 2, 1) * sin_signed


class _Rider(NamedTuple):
    arrays: tuple
    in_specs: tuple
    out_specs: tuple
    out_shapes: tuple
    body: Callable
    n_units: int


def _inproj_kernel(x_ref, g_ref, w_ref, cos_ref, sin_ref, *rest, slab, tn, emit_weights, riders):
    n_in = sum(len(r.in_specs) for r in riders)
    n_out = sum(len(r.out_specs) for r in riders)
    rider_in, rest = rest[:n_in], rest[n_in:]
    o_ref, h_ref = rest[0], rest[-1]
    wcopy_ref = rest[1] if emit_weights else None
    rider_out = rest[len(rest) - 1 - n_out:-1]
    j = pl.program_id(1)

    def rider_units():
        i = o = 0
        for r in riders:
            yield from r.body(rider_in[i:i + len(r.in_specs)], rider_out[o:o + len(r.out_specs)])
            i, o = i + len(r.in_specs), o + len(r.out_specs)

    @pl.when(j == 0)
    def _():
        g = g_ref[...]

        def body(r, _):
            rows = pl.ds(pl.multiple_of(r * slab, slab), slab)
            x = x_ref[rows, :]
            ms = jnp.mean(x * x, axis=-1, keepdims=True)
            h_ref[rows, :] = (x * lax.rsqrt(ms + EPS) * g).astype(BF16)
            return 0

        lax.fori_loop(0, x_ref.shape[0] // slab, body, 0)

    def rotate(s, acc):
        is_k = j * tn + s * MXU_DIM >= RET_WIDTH
        scale = jnp.where(is_k, HEAD_DIM ** -0.5, 1.0).astype(F32)
        cos = cos_ref[...] * scale
        sin = sin_ref[...] * scale
        return jnp.concatenate([_rope(acc[:, hh * HEAD_DIM:(hh + 1) * HEAD_DIM], cos, sin)
                                for hh in range(MXU_DIM // HEAD_DIM)], axis=1)

    epilogues = {"rope": rotate, "plain": lambda s, acc: acc}
    col_kinds = (("rope", 2 * RET_WIDTH), ("plain", IN_COLS - 2 * RET_WIDTH))
    slab_kinds = [kind for kind, width in col_kinds for _ in range(width // MXU_DIM)]
    per_tile = tn // MXU_DIM
    tiles_by_pattern = {}
    for tile in range(len(slab_kinds) // per_tile):
        pattern = tuple(slab_kinds[tile * per_tile:(tile + 1) * per_tile])
        tiles_by_pattern.setdefault(pattern, []).append(tile)

    def slabs(pattern):
        units, n_units, emitted = rider_units(), sum(r.n_units for r in riders), 0
        for s, kind in enumerate(pattern):
            cols = slice(s * MXU_DIM, (s + 1) * MXU_DIM)
            w = w_ref[:, cols]
            if wcopy_ref is not None:
                w = w.astype(BF16)
                wcopy_ref[:, cols] = w
            acc = jnp.dot(h_ref[...], w, preferred_element_type=F32)
            while emitted * len(pattern) < (s + 1) * n_units:
                next(units)
                emitted += 1
            o_ref[:, cols] = epilogues[kind](s, acc).astype(BF16)
        assert next(units, None) is None, "a rider emitted more units than it declared"

    for pattern, tiles in tiles_by_pattern.items():
        is_this_kind = functools.reduce(jnp.logical_or, [j == tile for tile in tiles])
        pl.when(is_this_kind)(functools.partial(slabs, pattern))


def _cast_rider(a, step_of, n_steps):
    rows = -(-a.shape[0] // n_steps)
    rows = -(-rows // BF16_SUBLANES) * BF16_SUBLANES
    assert a.shape[0] % rows == 0
    last = a.shape[0] // rows - 1
    spec = pl.BlockSpec((rows, a.shape[1]), lambda i, j: (jnp.minimum(step_of(i, j), last), 0))

    def body(ins, outs):
        outs[0][...] = ins[0][...].astype(BF16)
        yield

    return _Rider((a,), (spec,), (spec,), (jax.ShapeDtypeStruct(a.shape, BF16),), body, 1)


def _inproj(x2d, g_pre, w_in, cos, sin, *, tm, tn, make_riders=()):
    m, d = x2d.shape
    n = w_in.shape[1]
    pos_tiles = cos.shape[0] // tm
    emit_weights = w_in.dtype != BF16
    assert not emit_weights or m == tm, "each weight tile must be visited once to be copied out"
    n_col_tiles = n // tn
    n_steps = (m // tm) * n_col_tiles
    riders = tuple(make(lambda i, j: i * n_col_tiles + j, n_steps) for make in make_riders)
    in_specs = [
        pl.BlockSpec((tm, d), lambda i, j: (i, 0)),
        pl.BlockSpec((1, d), lambda i, j: (0, 0)),
        pl.BlockSpec((d, tn), lambda i, j: (0, j)),
        pl.BlockSpec((tm, HEAD_DIM), lambda i, j: (i % pos_tiles, 0)),
        pl.BlockSpec((tm, HEAD_DIM), lambda i, j: (i % pos_tiles, 0)),
    ]
    out_specs = [pl.BlockSpec((tm, tn), lambda i, j: (i, j))]
    out_shape = [jax.ShapeDtypeStruct((m, n), BF16)]
    if emit_weights:
        out_specs.append(pl.BlockSpec((d, tn), lambda i, j: (0, j)))
        out_shape.append(jax.ShapeDtypeStruct((d, n), BF16))
    for r in riders:
        in_specs += r.in_specs
        out_specs += r.out_specs
        out_shape += r.out_shapes
    return pl.pallas_call(
        functools.partial(_inproj_kernel, slab=min(tm, 128), tn=tn, emit_weights=emit_weights,
                          riders=riders),
        grid=(m // tm, n_col_tiles),
        in_specs=in_specs,
        out_specs=out_specs,
        out_shape=out_shape,
        scratch_shapes=[pltpu.VMEM((tm, d), BF16)],
        compiler_params=_params("arbitrary", "arbitrary"),
        name="inproj",
    )(x2d, g_pre, w_in, cos, sin, *[a for r in riders for a in r.arrays])


def _group_norm(o):
    mu = jnp.mean(o, axis=-1, keepdims=True)
    d = o - mu
    var = jnp.mean(d * d, axis=-1, keepdims=True)
    return d * lax.rsqrt(var + EPS)


def _retention_tile(q_ref, k_ref, v_ref, z_ref, o_ref, s_ref):
    c = RET_CHUNK
    row = lax.broadcasted_iota(jnp.int32, (c, c), 0).astype(F32)
    col = lax.broadcasted_iota(jnp.int32, (c, c), 1).astype(F32)
    diff = row - col
    for h in range(RET_HEADS):
        lg = _LOG_GAMMA[h]
        mask = jnp.where(diff >= 0, jnp.exp(jnp.maximum(diff, 0.0) * lg), 0.0)
        q_decay = jnp.exp((row + 1.0) * lg)
        k_decay = jnp.exp((c - 1.0 - row) * lg)
        chunk_decay = math.exp(c * lg)
        cols = slice(h * HEAD_DIM, (h + 1) * HEAD_DIM)
        for ci in range(q_ref.shape[0] // c):
            rows = slice(ci * c, (ci + 1) * c)
            q = q_ref[rows, cols]
            k = k_ref[rows, cols]
            v = v_ref[rows, cols]
            s0 = s_ref[0, h]
            scores = lax.dot_general(q, k, (((1,), (1,)), ((), ())),
                                     preferred_element_type=F32) * mask
            inner = jnp.dot(scores.astype(BF16), v, preferred_element_type=F32)
            cross = jnp.dot(q, s0.astype(BF16), preferred_element_type=F32) * q_decay
            kd = (k.astype(F32) * k_decay).astype(BF16)
            s_ref[0, h] = chunk_decay * s0 + lax.dot_general(
                kd, v, (((0,), (0,)), ((), ())), preferred_element_type=F32)
            o = _group_norm(inner + cross)
            o_ref[rows, cols] = (o * jax.nn.silu(z_ref[rows, cols].astype(F32))).astype(BF16)
            yield


def _retention_step_rider(proj, state, step_of, n_steps):
    groups, bb, _ = proj.shape
    assert groups <= n_steps
    group = lambda i, j: jnp.minimum(step_of(i, j), groups - 1)
    col_spec = lambda cb: pl.BlockSpec((1, bb, RET_WIDTH), lambda i, j: (group(i, j), 0, cb))
    state_spec = pl.BlockSpec((bb, RET_HEADS, HEAD_DIM, HEAD_DIM), lambda i, j: (group(i, j), 0, 0, 0))

    def body(ins, outs):
        q_ref, k_ref, v_ref, z_ref, s_ref = ins
        o_ref, sn_ref = outs
        pad = jnp.zeros((HEAD_DIM - bb, HEAD_DIM), F32)
        for h in range(RET_HEADS):
            gamma = math.exp(_LOG_GAMMA[h])
            cols = slice(h * HEAD_DIM, (h + 1) * HEAD_DIM)
            q, k, v = q_ref[0, :, cols], k_ref[0, :, cols], v_ref[0, :, cols]
            qt = jnp.concatenate([q, pad], axis=0).T
            kt = jnp.concatenate([k, pad], axis=0).T
            o = []
            for b in range(bb):
                s_new = gamma * s_ref[b, h] + kt[:, b:b + 1] * v[b:b + 1, :]
                sn_ref[b, h] = s_new
                o.append(jnp.sum(qt[:, b:b + 1] * s_new, axis=0, keepdims=True))
            o_ref[0, :, cols] = _group_norm(jnp.concatenate(o, axis=0)) * jax.nn.silu(z_ref[0, :, cols])
            yield

    return _Rider(
        (proj, proj, proj, proj, state),
        (col_spec(0), col_spec(1), col_spec(2), col_spec(3), state_spec),
        (pl.BlockSpec((1, bb, RET_WIDTH), lambda i, j: (group(i, j), 0, 0)), state_spec),
        (jax.ShapeDtypeStruct((groups, bb, RET_WIDTH), F32), jax.ShapeDtypeStruct(state.shape, F32)),
        body, RET_HEADS)


def _s5_output_gate(y, z, wglu_ref, bglu_ref):
    y = jax.nn.gelu(y)
    g = jnp.dot(y.astype(BF16), wglu_ref[...], preferred_element_type=F32) + bglu_ref[...]
    return y * jax.nn.sigmoid(g) * jax.nn.silu(z.astype(F32))


def _s5_tile(u_ref, z_ref, wb_ref, cw_ref, ar_ref, ai_ref, d_ref, wglu_ref, bglu_ref,
             o_ref, sr_ref, si_ref, y_scr, x_scr, *, sub, filler=(), n_filler=0):
    n_sub = len(x_scr) // 2
    xr_scr, xi_scr = x_scr[:n_sub], x_scr[n_sub:]
    strided = lambda j: pl.ds(j, sub, stride=S5_PITCH)
    filler = iter(filler)
    progress = [0, 0]
    n_slots = n_sub * S5_PAIRS

    def fill():
        progress[0] += 1
        while progress[1] < n_filler and progress[1] * n_slots < progress[0] * n_filler:
            next(filler)
            progress[1] += 1

    for k in range(n_sub):
        rows = slice(k * sub, (k + 1) * sub)
        for blk in range(S5_BLOCKS):
            ub = u_ref[rows, blk * LANES:(blk + 1) * LANES]
            for j in range(blk * PAIRS_PER_BLOCK, (blk + 1) * PAIRS_PER_BLOCK):
                bu = jnp.dot(ub, wb_ref[j], preferred_element_type=F32)
                xr_scr[k][strided(j), :] = bu[:, :LANES]
                xi_scr[k][strided(j), :] = bu[:, LANES:]
                fill()
    ar = ar_ref[...]
    ai = ai_ref[...]
    xr, xi = sr_ref[0], si_ref[0]
    for k in range(n_sub):
        rows = slice(k * sub, (k + 1) * sub)
        for t in range(sub):
            step = slice(t * S5_PITCH, t * S5_PITCH + S5_PAIRS)
            xr, xi = (ar * xr - ai * xi + xr_scr[k][step, :],
                      ar * xi + ai * xr + xi_scr[k][step, :])
            xr_scr[k][step, :] = xr
            xi_scr[k][step, :] = xi
        for blk in range(S5_BLOCKS):
            cols = slice(blk * LANES, (blk + 1) * LANES)
            acc = d_ref[:, cols] * u_ref[rows, cols].astype(F32)
            for j in range(blk * PAIRS_PER_BLOCK, (blk + 1) * PAIRS_PER_BLOCK):
                x = jnp.concatenate([xr_scr[k][strided(j), :], xi_scr[k][strided(j), :]], axis=1)
                acc = acc + lax.dot_general(x.astype(BF16), cw_ref[j], (((1,), (1,)), ((), ())),
                                            preferred_element_type=F32)
            y_scr[rows, cols] = acc
        o_ref[rows, :] = _s5_output_gate(
            y_scr[rows, :], z_ref[rows, :], wglu_ref, bglu_ref).astype(BF16)
    sr_ref[0] = xr
    si_ref[0] = xi


def _mixers_kernel(q_ref, k_ref, v_ref, za_ref, u_ref, zb_ref, wb_ref, cw_ref, ar_ref, ai_ref, d_ref,
                   wglu_ref, bglu_ref, ya_ref, ret_ref, yb_ref, sr_ref, si_ref, y_scr, *x_scr, sub):
    @pl.when(pl.program_id(1) == 0)
    def _():
        ret_ref[...] = jnp.zeros_like(ret_ref)
        sr_ref[...] = jnp.zeros_like(sr_ref)
        si_ref[...] = jnp.zeros_like(si_ref)

    retention_units = RET_HEADS * (q_ref.shape[0] // RET_CHUNK)
    _s5_tile(u_ref, zb_ref, wb_ref, cw_ref, ar_ref, ai_ref, d_ref, wglu_ref, bglu_ref,
             yb_ref, sr_ref, si_ref, y_scr, x_scr, sub=sub,
             filler=_retention_tile(q_ref, k_ref, v_ref, za_ref, ya_ref, ret_ref),
             n_filler=retention_units)


def _mixers_prompt(proj, wb, cw, ar, ai, d, wglu, bglu, *, batch, seq, tm, sub):
    nt = seq // tm
    row_block = lambda b, i: b * nt + i
    col_spec = lambda cb: pl.BlockSpec((tm, RET_WIDTH), lambda b, i: (row_block(b, i), cb))
    s5_state_spec = pl.BlockSpec((1, S5_PAIRS, LANES), lambda b, i: (b, 0, 0))
    assert RET_WIDTH == S5_WIDTH
    return pl.pallas_call(
        functools.partial(_mixers_kernel, sub=sub),
        grid=(batch, nt),
        in_specs=[
            col_spec(0), col_spec(1), col_spec(2), col_spec(3), col_spec(4), col_spec(5),
            _resident(wb.shape), _resident(cw.shape), _resident(ar.shape), _resident(ai.shape),
            _resident(d.shape), _resident(wglu.shape), _resident(bglu.shape),
        ],
        out_specs=[
            pl.BlockSpec((tm, RET_WIDTH), lambda b, i: (row_block(b, i), 0)),
            pl.BlockSpec((1, RET_HEADS, HEAD_DIM, HEAD_DIM), lambda b, i: (b, 0, 0, 0)),
            pl.BlockSpec((tm, S5_WIDTH), lambda b, i: (row_block(b, i), 0)),
            s5_state_spec, s5_state_spec,
        ],
        out_shape=[
            jax.ShapeDtypeStruct((batch * seq, RET_WIDTH), BF16),
            jax.ShapeDtypeStruct((batch, RET_HEADS, HEAD_DIM, HEAD_DIM), F32),
            jax.ShapeDtypeStruct((batch * seq, S5_WIDTH), BF16),
            jax.ShapeDtypeStruct((batch, S5_PAIRS, LANES), F32),
            jax.ShapeDtypeStruct((batch, S5_PAIRS, LANES), F32),
        ],
        scratch_shapes=[pltpu.VMEM((tm, S5_WIDTH), F32)]
        + [pltpu.VMEM((sub * S5_PITCH, LANES), F32)] * (2 * (tm // sub)),
        compiler_params=_params("arbitrary", "arbitrary"),
        name="mixers_prompt",
    )(proj, proj, proj, proj, proj, proj, wb, cw, ar, ai, d, wglu, bglu)


def _s5_step_kernel(u_ref, z_ref, x0r_ref, x0i_ref, wb_ref, cw_ref, ar_ref, ai_ref, d_ref,
                    wglu_ref, bglu_ref, o_ref, sr_ref, si_ref, y_scr):
    for blk in range(S5_BLOCKS):
        cols = slice(blk * LANES, (blk + 1) * LANES)
        ub = u_ref[:, cols]
        acc = d_ref[:, cols] * ub.astype(F32)
        for j in range(blk * PAIRS_PER_BLOCK, (blk + 1) * PAIRS_PER_BLOCK):
            pc = slice(j * LANES, (j + 1) * LANES)
            bu = jnp.dot(ub, wb_ref[j], preferred_element_type=F32)
            ar = ar_ref[j:j + 1, :]
            ai = ai_ref[j:j + 1, :]
            x0r = x0r_ref[:, pc]
            x0i = x0i_ref[:, pc]
            nr = ar * x0r - ai * x0i + bu[:, :LANES]
            ni = ar * x0i + ai * x0r + bu[:, LANES:]
            sr_ref[:, pc] = nr
            si_ref[:, pc] = ni
            x = jnp.concatenate([nr, ni], axis=1).astype(BF16)
            acc = acc + lax.dot_general(x, cw_ref[j], (((1,), (1,)), ((), ())),
                                        preferred_element_type=F32)
        y_scr[:, cols] = acc
    o_ref[...] = _s5_output_gate(y_scr[...], z_ref[...], wglu_ref, bglu_ref).astype(BF16)


def _s5_sample(proj, x0r, x0i, wb, cw, ar, ai, d, wglu, bglu):
    nb = proj.shape[0]
    ub_col = (4 * RET_WIDTH) // S5_WIDTH
    full = lambda a: pl.BlockSpec(a.shape, lambda i: (0,) * a.ndim)
    return pl.pallas_call(
        _s5_step_kernel,
        grid=(1,),
        in_specs=[
            pl.BlockSpec((nb, S5_WIDTH), lambda i: (0, ub_col)),
            pl.BlockSpec((nb, S5_WIDTH), lambda i: (0, ub_col + 1)),
            full(x0r), full(x0i), full(wb), full(cw), full(ar), full(ai), full(d), full(wglu), full(bglu),
        ],
        out_specs=[
            pl.BlockSpec((nb, S5_WIDTH), lambda i: (0, 0)),
            full(x0r), full(x0i),
        ],
        out_shape=[
            jax.ShapeDtypeStruct((nb, S5_WIDTH), BF16),
            jax.ShapeDtypeStruct(x0r.shape, F32),
            jax.ShapeDtypeStruct(x0i.shape, F32),
        ],
        scratch_shapes=[pltpu.VMEM((nb, S5_WIDTH), F32)],
        compiler_params=_params("arbitrary"),
        name="s5_sample",
    )(proj, proj, x0r, x0i, wb, cw, ar, ai, d, wglu, bglu)


def _merge_kernel(ya_ref, yb_ref, ga_ref, gb_ref, x_ref, wpa_ref, wpb_ref, wout_ref, gpost_ref, o_ref,
                  merged_scr):
    slabs = [slice(s * MXU_DIM, (s + 1) * MXU_DIM) for s in range(D_MODEL // MXU_DIM)]
    ya_in = ya_ref[...].astype(BF16)
    yb_in = yb_ref[...].astype(BF16)
    for cols in slabs:
        ya = jnp.dot(ya_in, wpa_ref[:, cols], preferred_element_type=F32)
        yb = jnp.dot(yb_in, wpb_ref[:, cols], preferred_element_type=F32)
        merged_scr[:, cols] = (jax.nn.sigmoid(ga_ref[:, cols].astype(F32)) * ya
                               + jax.nn.sigmoid(gb_ref[:, cols].astype(F32)) * yb).astype(BF16)
    sq = jnp.zeros((o_ref.shape[0], 1), F32)
    for cols in slabs:
        out = jnp.dot(merged_scr[...], wout_ref[:, cols], preferred_element_type=F32)
        sq = sq + jnp.sum(out * out, axis=-1, keepdims=True)
        o_ref[:, cols] = out
    inv_rms = lax.rsqrt(sq * (1.0 / D_MODEL) + EPS)
    for cols in slabs:
        o_ref[:, cols] = x_ref[:, cols] + o_ref[:, cols] * inv_rms * gpost_ref[:, cols]


def _merge(ya, yb, proj, x2d, wpa, wpb, wout, gpost, *, tm):
    m = x2d.shape[0]
    ga_col = (4 * RET_WIDTH + 2 * S5_WIDTH) // D_MODEL
    return pl.pallas_call(
        _merge_kernel,
        grid=(m // tm,),
        in_specs=[
            pl.BlockSpec((tm, RET_WIDTH), lambda i: (i, 0)),
            pl.BlockSpec((tm, S5_WIDTH), lambda i: (i, 0)),
            pl.BlockSpec((tm, D_MODEL), lambda i: (i, ga_col)),
            pl.BlockSpec((tm, D_MODEL), lambda i: (i, ga_col + 1)),
            pl.BlockSpec((tm, D_MODEL), lambda i: (i, 0)),
            _resident(wpa.shape), _resident(wpb.shape), _resident(wout.shape), _resident(gpost.shape),
        ],
        out_specs=pl.BlockSpec((tm, D_MODEL), lambda i: (i, 0)),
        out_shape=jax.ShapeDtypeStruct((m, D_MODEL), F32),
        scratch_shapes=[pltpu.VMEM((tm, D_MODEL), BF16)],
        compiler_params=_params("arbitrary"),
        name="merge_out",
    )(ya, yb, proj, proj, x2d, wpa, wpb, wout, gpost)


def _rope_tables(pos):
    half = HEAD_DIM // 2
    inv = ROPE_BASE ** (-jnp.arange(half, dtype=F32) / half)
    ang = pos.astype(F32)[:, None] * inv[None, :]
    cos, sin = jnp.cos(ang), jnp.sin(ang)
    return jnp.concatenate([cos, cos], axis=-1), jnp.concatenate([-sin, sin], axis=-1)


def _s5_discretize(lam_re, lam_im, log_dt, b_re, b_im):
    dt = jnp.exp(log_dt)[:, None]
    mag = jnp.exp(lam_re * dt)
    abar_r = mag * jnp.cos(lam_im * dt)
    abar_i = mag * jnp.sin(lam_im * dt)
    nr, ni = abar_r - 1.0, abar_i
    den = lam_re * lam_re + lam_im * lam_im
    coef_r = (nr * lam_re + ni * lam_im) / den
    coef_i = (ni * lam_re - nr * lam_im) / den
    bbar_r = coef_r[:, :, None] * b_re - coef_i[:, :, None] * b_im
    bbar_i = coef_r[:, :, None] * b_im + coef_i[:, :, None] * b_re
    return abar_r, abar_i, bbar_r, bbar_i


def _s5_pair_weights(bbar_r, bbar_i, c_re, c_im):
    j = lax.broadcasted_iota(jnp.int32, (S5_PAIRS, LANES, 2 * LANES), 0)
    row = lax.broadcasted_iota(jnp.int32, (S5_PAIRS, LANES, 2 * LANES), 1)
    col = lax.broadcasted_iota(jnp.int32, (S5_PAIRS, LANES, 2 * LANES), 2)
    pair_lanes = 2 * S5_GROUP
    keep = ((row // pair_lanes == j % PAIRS_PER_BLOCK)
            & ((row // S5_GROUP) % 2 == (col // S5_P) % 2))

    def expand(re, im, perm):
        base = jnp.stack([re, im]).reshape((2, S5_PAIRS, 2) + re.shape[1:]).transpose(perm)
        base = base.reshape(S5_PAIRS, 1, S5_GROUP, 2 * LANES)
        tiled = jnp.broadcast_to(base, (S5_PAIRS, LANES // S5_GROUP, S5_GROUP, 2 * LANES))
        return jnp.where(keep, tiled.reshape(S5_PAIRS, LANES, 2 * LANES), 0.0).astype(BF16)

    wb = expand(bbar_r, bbar_i, (1, 4, 0, 2, 3))
    cw = expand(c_re, -c_im, (1, 3, 0, 2, 4))
    return wb, cw


def kernel(x_prompt, x_sample, state_ret, state_s5_re, state_s5_im, g_pre, w_in, w_pa, w_pb, w_out, g_post,
           s5_lam_re, s5_lam_im, s5_log_dt, s5_b_re, s5_b_im, s5_c_re, s5_c_im, s5_d, s5_w_glu, s5_b_glu):
    assert w_in.shape[0] == 1, "single trunk layer"
    bp, lp, _ = x_prompt.shape
    bs, ls, _ = x_sample.shape
    assert ls == 1 and lp % RET_CHUNK == 0

    b_glu = s5_b_glu[0].reshape(1, S5_WIDTH)
    gpre = g_pre[0].reshape(1, D_MODEL)
    gpost = g_post[0].reshape(1, D_MODEL)

    abar_r, abar_i, bbar_r, bbar_i = _s5_discretize(
        s5_lam_re[0], s5_lam_im[0], s5_log_dt[0], s5_b_re[0], s5_b_im[0])
    wb, cw = _s5_pair_weights(bbar_r, bbar_i, s5_c_re[0], s5_c_im[0])
    ar = abar_r.reshape(S5_PAIRS, LANES)
    ai = abar_i.reshape(S5_PAIRS, LANES)
    d_skip = s5_d[0].reshape(1, S5_WIDTH)

    xs = x_sample.reshape(bs, D_MODEL)
    cos_s, sin_s = _rope_tables(jnp.full((bs,), PAST_LEN, jnp.int32))
    proj_s, w_in_b = _inproj(xs, gpre, w_in[0], cos_s, sin_s, tm=bs, tn=INPROJ_COLS_SAMPLE)

    xp = x_prompt.reshape(bp * lp, D_MODEL)
    cos_p, sin_p = _rope_tables(jnp.arange(lp, dtype=jnp.int32))
    proj_s_grouped = proj_s.astype(F32).reshape(bs // RET_SAMPLE_SEQS, RET_SAMPLE_SEQS, IN_COLS)
    proj_p, w_pa_b, w_pb_b, w_out_b, w_glu_b, ya_s, ret_s = _inproj(
        xp, gpre, w_in_b, cos_p, sin_p, tm=INPROJ_ROWS, tn=INPROJ_COLS_PROMPT,
        make_riders=[functools.partial(_cast_rider, w) for w in (w_pa[0], w_pb[0], w_out[0], s5_w_glu[0])]
        + [functools.partial(_retention_step_rider, proj_s_grouped, state_ret[0])])
    ya_s = ya_s.reshape(bs, RET_WIDTH)
    ya_p, ret_p, yb_p, s5r_p, s5i_p = _mixers_prompt(
        proj_p, wb, cw, ar, ai, d_skip, w_glu_b, b_glu, batch=bp, seq=lp, tm=MIXER_ROWS,
        sub=S5_SUBTILE_ROWS)
    y_p = _merge(ya_p, yb_p, proj_p, xp, w_pa_b, w_pb_b, w_out_b, gpost, tm=MERGE_ROWS)

    x0r = state_s5_re[0].reshape(bs, S5_GROUPS * S5_P)
    x0i = state_s5_im[0].reshape(bs, S5_GROUPS * S5_P)
    yb_s, s5r_s, s5i_s = _s5_sample(proj_s, x0r, x0i, wb, cw, ar, ai, d_skip, w_glu_b, b_glu)
    y_s = _merge(ya_s, yb_s, proj_s, xs, w_pa_b, w_pb_b, w_out_b, gpost, tm=bs)

    state_shape = (1, -1, S5_GROUPS, S5_P)
    return (y_p.reshape(bp, lp, D_MODEL), y_s.reshape(bs, 1, D_MODEL),
            ret_p[None], s5r_p.reshape(state_shape), s5i_p.reshape(state_shape),
            ret_s[None], s5r_s.reshape(state_shape), s5i_s.reshape(state_shape))
```

```python
import functools
import math
from typing import Callable, NamedTuple

import jax
import jax.numpy as jnp
import numpy as np
from jax import lax
from jax.experimental import pallas as pl
from jax.experimental.pallas import tpu as pltpu

F32 = jnp.float32
BF16 = jnp.bfloat16

D_MODEL = 2048
RET_WIDTH = 1024
RET_HEADS = 8
HEAD_DIM = RET_WIDTH // RET_HEADS
RET_CHUNK = 128
ROPE_BASE = 10000.0
S5_WIDTH = 1024
S5_GROUP = 16
S5_GROUPS = S5_WIDTH // S5_GROUP
S5_P = 64
PAST_LEN = 16384
EPS = 1e-6
IN_COLS = 4 * RET_WIDTH + 2 * S5_WIDTH + 2 * D_MODEL

LANES = 128
MXU_DIM = 256
BF16_SUBLANES = 16
S5_PAIRS = S5_GROUPS // 2
PAIRS_PER_BLOCK = LANES // (2 * S5_GROUP)
S5_BLOCKS = S5_WIDTH // LANES
S5_PITCH = 36
VMEM_LIMIT = 60 * 1024 * 1024

INPROJ_ROWS = 1024
INPROJ_COLS_PROMPT = 2048
INPROJ_COLS_SAMPLE = 1024
MIXER_ROWS = 512
S5_SUBTILE_ROWS = 128
MERGE_ROWS = 512
RET_SAMPLE_SEQS = 4

_LOG_GAMMA = [float(np.log(np.float32(1.0) - np.float32(2.0) ** np.float32(-5.0 - h)))
              for h in range(RET_HEADS)]


def _params(*sem):
    return pltpu.CompilerParams(dimension_semantics=sem, vmem_limit_bytes=VMEM_LIMIT)


def _resident(shape):
    nd = len(shape)
    return pl.BlockSpec(shape, lambda *_: (0,) * nd, pipeline_mode=pl.Buffered(1))


def _rope(x, cos, sin_signed):
    return x * cos + pltpu.roll(x, HEAD_DIM // 2, 1) * sin_signed


class _Rider(NamedTuple):
    arrays: tuple
    in_specs: tuple
    out_specs: tuple
    out_shapes: tuple
    body: Callable
    n_units: int


def _inproj_kernel(x_ref, g_ref, w_ref, cos_ref, sin_ref, *rest, slab, tn, emit_weights, riders):
    n_in = sum(len(r.in_specs) for r in riders)
    n_out = sum(len(r.out_specs) for r in riders)
    rider_in, rest = rest[:n_in], rest[n_in:]
    o_ref, h_ref = rest[0], rest[-1]
    wcopy_ref = rest[1] if emit_weights else None
    rider_out = rest[len(rest) - 1 - n_out:-1]
    j = pl.program_id(1)

    def rider_units():
        i = o = 0
        for r in riders:
            yield from r.body(rider_in[i:i + len(r.in_specs)], rider_out[o:o + len(r.out_specs)])
            i, o = i + len(r.in_specs), o + len(r.out_specs)

    @pl.when(j == 0)
    def _():
        g = g_ref[...]

        def body(r, _):
            rows = pl.ds(pl.multiple_of(r * slab, slab), slab)
            x = x_ref[rows, :]
            ms = jnp.mean(x * x, axis=-1, keepdims=True)
            h_ref[rows, :] = (x * lax.rsqrt(ms + EPS) * g).astype(BF16)
            return 0

        lax.fori_loop(0, x_ref.shape[0] // slab, body, 0)

    def rotate(s, acc):
        is_k = j * tn + s * MXU_DIM >= RET_WIDTH
        scale = jnp.where(is_k, HEAD_DIM ** -0.5, 1.0).astype(F32)
        cos = cos_ref[...] * scale
        sin = sin_ref[...] * scale
        return jnp.concatenate([_rope(acc[:, hh * HEAD_DIM:(hh + 1) * HEAD_DIM], cos, sin)
                                for hh in range(MXU_DIM // HEAD_DIM)], axis=1)

    epilogues = {"rope": rotate, "plain": lambda s, acc: acc}
    col_kinds = (("rope", 2 * RET_WIDTH), ("plain", IN_COLS - 2 * RET_WIDTH))
    slab_kinds = [kind for kind, width in col_kinds for _ in range(width // MXU_DIM)]
    per_tile = tn // MXU_DIM
    tiles_by_pattern = {}
    for tile in range(len(slab_kinds) // per_tile):
        pattern = tuple(slab_kinds[tile * per_tile:(tile + 1) * per_tile])
        tiles_by_pattern.setdefault(pattern, []).append(tile)

    def slabs(pattern):
        units, n_units, emitted = rider_units(), sum(r.n_units for r in riders), 0
        for s, kind in enumerate(pattern):
            cols = slice(s * MXU_DIM, (s + 1) * MXU_DIM)
            w = w_ref[:, cols]
            if wcopy_ref is not None:
                w = w.astype(BF16)
                wcopy_ref[:, cols] = w
            acc = jnp.dot(h_ref[...], w, preferred_element_type=F32)
            while emitted * len(pattern) < (s + 1) * n_units:
                next(units)
                emitted += 1
            o_ref[:, cols] = epilogues[kind](s, acc).astype(BF16)
        assert next(units, None) is None, "a rider emitted more units than it declared"

    for pattern, tiles in tiles_by_pattern.items():
        is_this_kind = functools.reduce(jnp.logical_or, [j == tile for tile in tiles])
        pl.when(is_this_kind)(functools.partial(slabs, pattern))


def _cast_rider(a, step_of, n_steps):
    rows = -(-a.shape[0] // n_steps)
    rows = -(-rows // BF16_SUBLANES) * BF16_SUBLANES
    assert a.shape[0] % rows == 0
    last = a.shape[0] // rows - 1
    spec = pl.BlockSpec((rows, a.shape[1]), lambda i, j: (jnp.minimum(step_of(i, j), last), 0))

    def body(ins, outs):
        outs[0][...] = ins[0][...].astype(BF16)
        yield

    return _Rider((a,), (spec,), (spec,), (jax.ShapeDtypeStruct(a.shape, BF16),), body, 1)


def _inproj(x2d, g_pre, w_in, cos, sin, *, tm, tn, make_riders=()):
    m, d = x2d.shape
    n = w_in.shape[1]
    pos_tiles = cos.shape[0] // tm
    emit_weights = w_in.dtype != BF16
    assert not emit_weights or m == tm, "each weight tile must be visited once to be copied out"
    n_col_tiles = n // tn
    n_steps = (m // tm) * n_col_tiles
    riders = tuple(make(lambda i, j: i * n_col_tiles + j, n_steps) for make in make_riders)
    in_specs = [
        pl.BlockSpec((tm, d), lambda i, j: (i, 0)),
        pl.BlockSpec((1, d), lambda i, j: (0, 0)),
        pl.BlockSpec((d, tn), lambda i, j: (0, j)),
        pl.BlockSpec((tm, HEAD_DIM), lambda i, j: (i % pos_tiles, 0)),
        pl.BlockSpec((tm, HEAD_DIM), lambda i, j: (i % pos_tiles, 0)),
    ]
    out_specs = [pl.BlockSpec((tm, tn), lambda i, j: (i, j))]
    out_shape = [jax.ShapeDtypeStruct((m, n), BF16)]
    if emit_weights:
        out_specs.append(pl.BlockSpec((d, tn), lambda i, j: (0, j)))
        out_shape.append(jax.ShapeDtypeStruct((d, n), BF16))
    for r in riders:
        in_specs += r.in_specs
        out_specs += r.out_specs
        out_shape += r.out_shapes
    return pl.pallas_call(
        functools.partial(_inproj_kernel, slab=min(tm, 128), tn=tn, emit_weights=emit_weights,
                          riders=riders),
        grid=(m // tm, n_col_tiles),
        in_specs=in_specs,
        out_specs=out_specs,
        out_shape=out_shape,
        scratch_shapes=[pltpu.VMEM((tm, d), BF16)],
        compiler_params=_params("arbitrary", "arbitrary"),
        name="inproj",
    )(x2d, g_pre, w_in, cos, sin, *[a for r in riders for a in r.arrays])


def _group_norm(o):
    mu = jnp.mean(o, axis=-1, keepdims=True)
    d = o - mu
    var = jnp.mean(d * d, axis=-1, keepdims=True)
    return d * lax.rsqrt(var + EPS)


def _retention_tile(q_ref, k_ref, v_ref, z_ref, o_ref, s_ref):
    c = RET_CHUNK
    row = lax.broadcasted_iota(jnp.int32, (c, c), 0).astype(F32)
    col = lax.broadcasted_iota(jnp.int32, (c, c), 1).astype(F32)
    diff = row - col
    for h in range(RET_HEADS):
        lg = _LOG_GAMMA[h]
        mask = jnp.where(diff >= 0, jnp.exp(jnp.maximum(diff, 0.0) * lg), 0.0)
        q_decay = jnp.exp((row + 1.0) * lg)
        k_decay = jnp.exp((c - 1.0 - row) * lg)
        chunk_decay = math.exp(c * lg)
        cols = slice(h * HEAD_DIM, (h + 1) * HEAD_DIM)
        for ci in range(q_ref.shape[0] // c):
            rows = slice(ci * c, (ci + 1) * c)
            q = q_ref[rows, cols]
            k = k_ref[rows, cols]
            v = v_ref[rows, cols]
            s0 = s_ref[0, h]
            scores = lax.dot_general(q, k, (((1,), (1,)), ((), ())),
                                     preferred_element_type=F32) * mask
            inner = jnp.dot(scores.astype(BF16), v, preferred_element_type=F32)
            cross = jnp.dot(q, s0.astype(BF16), preferred_element_type=F32) * q_decay
            kd = (k.astype(F32) * k_decay).astype(BF16)
            s_ref[0, h] = chunk_decay * s0 + lax.dot_general(
                kd, v, (((0,), (0,)), ((), ())), preferred_element_type=F32)
            o = _group_norm(inner + cross)
            o_ref[rows, cols] = (o * jax.nn.silu(z_ref[rows, cols].astype(F32))).astype(BF16)
            yield


def _retention_step_rider(proj, state, step_of, n_steps):
    groups, bb, _ = proj.shape
    assert groups <= n_steps
    group = lambda i, j: jnp.minimum(step_of(i, j), groups - 1)
    col_spec = lambda cb: pl.BlockSpec((1, bb, RET_WIDTH), lambda i, j: (group(i, j), 0, cb))
    state_spec = pl.BlockSpec((bb, RET_HEADS, HEAD_DIM, HEAD_DIM), lambda i, j: (group(i, j), 0, 0, 0))

    def body(ins, outs):
        q_ref, k_ref, v_ref, z_ref, s_ref = ins
        o_ref, sn_ref = outs
        pad = jnp.zeros((HEAD_DIM - bb, HEAD_DIM), F32)
        for h in range(RET_HEADS):
            gamma = math.exp(_LOG_GAMMA[h])
            cols = slice(h * HEAD_DIM, (h + 1) * HEAD_DIM)
            q, k, v = q_ref[0, :, cols], k_ref[0, :, cols], v_ref[0, :, cols]
            qt = jnp.concatenate([q, pad], axis=0).T
            kt = jnp.concatenate([k, pad], axis=0).T
            o = []
            for b in range(bb):
                s_new = gamma * s_ref[b, h] + kt[:, b:b + 1] * v[b:b + 1, :]
                sn_ref[b, h] = s_new
                o.append(jnp.sum(qt[:, b:b + 1] * s_new, axis=0, keepdims=True))
            o_ref[0, :, cols] = _group_norm(jnp.concatenate(o, axis=0)) * jax.nn.silu(z_ref[0, :, cols])
            yield

    return _Rider(
        (proj, proj, proj, proj, state),
        (col_spec(0), col_spec(1), col_spec(2), col_spec(3), state_spec),
        (pl.BlockSpec((1, bb, RET_WIDTH), lambda i, j: (group(i, j), 0, 0)), state_spec),
        (jax.ShapeDtypeStruct((groups, bb, RET_WIDTH), F32), jax.ShapeDtypeStruct(state.shape, F32)),
        body, RET_HEADS)


def _s5_output_gate(y, z, wglu_ref, bglu_ref):
    y = jax.nn.gelu(y)
    g = jnp.dot(y.astype(BF16), wglu_ref[...], preferred_element_type=F32) + bglu_ref[...]
    return y * jax.nn.sigmoid(g) * jax.nn.silu(z.astype(F32))


def _s5_tile(u_ref, z_ref, wb_ref, cw_ref, ar_ref, ai_ref, d_ref, wglu_ref, bglu_ref,
             o_ref, sr_ref, si_ref, y_scr, x_scr, *, sub, filler=(), n_filler=0):
    n_sub = len(x_scr) // 2
    xr_scr, xi_scr = x_scr[:n_sub], x_scr[n_sub:]
    strided = lambda j: pl.ds(j, sub, stride=S5_PITCH)
    filler = iter(filler)
    progress = [0, 0]
    n_slots = n_sub * S5_PAIRS

    def fill():
        progress[0] += 1
        while progress[1] < n_filler and progress[1] * n_slots < progress[0] * n_filler:
            next(filler)
            progress[1] += 1

    for k in range(n_sub):
        rows = slice(k * sub, (k + 1) * sub)
        for blk in range(S5_BLOCKS):
            ub = u_ref[rows, blk * LANES:(blk + 1) * LANES]
            for j in range(blk * PAIRS_PER_BLOCK, (blk + 1) * PAIRS_PER_BLOCK):
                bu = jnp.dot(ub, wb_ref[j], preferred_element_type=F32)
                xr_scr[k][strided(j), :] = bu[:, :LANES]
                xi_scr[k][strided(j), :] = bu[:, LANES:]
                fill()
    ar = ar_ref[...]
    ai = ai_ref[...]
    xr, xi = sr_ref[0], si_ref[0]
    for k in range(n_sub):
        rows = slice(k * sub, (k + 1) * sub)
        for t in range(sub):
            step = slice(t * S5_PITCH, t * S5_PITCH + S5_PAIRS)
            xr, xi = (ar * xr - ai * xi + xr_scr[k][step, :],
                      ar * xi + ai * xr + xi_scr[k][step, :])
            xr_scr[k][step, :] = xr
            xi_scr[k][step, :] = xi
        for blk in range(S5_BLOCKS):
            cols = slice(blk * LANES, (blk + 1) * LANES)
            acc = d_ref[:, cols] * u_ref[rows, cols].astype(F32)
            for j in range(blk * PAIRS_PER_BLOCK, (blk + 1) * PAIRS_PER_BLOCK):
                x = jnp.concatenate([xr_scr[k][strided(j), :], xi_scr[k][strided(j), :]], axis=1)
                acc = acc + lax.dot_general(x.astype(BF16), cw_ref[j], (((1,), (1,)), ((), ())),
                                            preferred_element_type=F32)
            y_scr[rows, cols] = acc
        o_ref[rows, :] = _s5_output_gate(
            y_scr[rows, :], z_ref[rows, :], wglu_ref, bglu_ref).astype(BF16)
    sr_ref[0] = xr
    si_ref[0] = xi


def _mixers_kernel(q_ref, k_ref, v_ref, za_ref, u_ref, zb_ref, wb_ref, cw_ref, ar_ref, ai_ref, d_ref,
                   wglu_ref, bglu_ref, *rest, sub, riders):
    n_in = sum(len(r.in_specs) for r in riders)
    n_out = sum(len(r.out_specs) for r in riders)
    rider_in, rest = rest[:n_in], rest[n_in:]
    ya_ref, ret_ref, yb_ref, sr_ref, si_ref = rest[:5]
    rider_out, (y_scr, *x_scr) = rest[5:5 + n_out], rest[5 + n_out:]

    @pl.when(pl.program_id(1) == 0)
    def _():
        ret_ref[...] = jnp.zeros_like(ret_ref)
        sr_ref[...] = jnp.zeros_like(sr_ref)
        si_ref[...] = jnp.zeros_like(si_ref)

    for r in riders:
        for _ in r.body(rider_in[:len(r.in_specs)], rider_out[:len(r.out_specs)]):
            pass
        rider_in, rider_out = rider_in[len(r.in_specs):], rider_out[len(r.out_specs):]

    retention_units = RET_HEADS * (q_ref.shape[0] // RET_CHUNK)
    _s5_tile(u_ref, zb_ref, wb_ref, cw_ref, ar_ref, ai_ref, d_ref, wglu_ref, bglu_ref,
             yb_ref, sr_ref, si_ref, y_scr, x_scr, sub=sub,
             filler=_retention_tile(q_ref, k_ref, v_ref, za_ref, ya_ref, ret_ref),
             n_filler=retention_units)


def _mixers_prompt(proj, wb, cw, ar, ai, d, wglu, bglu, *, batch, seq, tm, sub, make_riders=()):
    nt = seq // tm
    row_block = lambda b, i: b * nt + i
    col_spec = lambda cb: pl.BlockSpec((tm, RET_WIDTH), lambda b, i: (row_block(b, i), cb))
    s5_state_spec = pl.BlockSpec((1, S5_PAIRS, LANES), lambda b, i: (b, 0, 0))
    assert RET_WIDTH == S5_WIDTH
    riders = tuple(make(row_block, batch * nt) for make in make_riders)
    in_specs = [
        col_spec(0), col_spec(1), col_spec(2), col_spec(3), col_spec(4), col_spec(5),
        _resident(wb.shape), _resident(cw.shape), _resident(ar.shape), _resident(ai.shape),
        _resident(d.shape), _resident(wglu.shape), _resident(bglu.shape),
    ]
    out_specs = [
        pl.BlockSpec((tm, RET_WIDTH), lambda b, i: (row_block(b, i), 0)),
        pl.BlockSpec((1, RET_HEADS, HEAD_DIM, HEAD_DIM), lambda b, i: (b, 0, 0, 0)),
        pl.BlockSpec((tm, S5_WIDTH), lambda b, i: (row_block(b, i), 0)),
        s5_state_spec, s5_state_spec,
    ]
    out_shape = [
        jax.ShapeDtypeStruct((batch * seq, RET_WIDTH), BF16),
        jax.ShapeDtypeStruct((batch, RET_HEADS, HEAD_DIM, HEAD_DIM), F32),
        jax.ShapeDtypeStruct((batch * seq, S5_WIDTH), BF16),
        jax.ShapeDtypeStruct((batch, S5_PAIRS, LANES), F32),
        jax.ShapeDtypeStruct((batch, S5_PAIRS, LANES), F32),
    ]
    for r in riders:
        in_specs += r.in_specs
        out_specs += r.out_specs
        out_shape += r.out_shapes
    return pl.pallas_call(
        functools.partial(_mixers_kernel, sub=sub, riders=riders),
        grid=(batch, nt),
        in_specs=in_specs,
        out_specs=out_specs,
        out_shape=out_shape,
        scratch_shapes=[pltpu.VMEM((tm, S5_WIDTH), F32)]
        + [pltpu.VMEM((sub * S5_PITCH, LANES), F32)] * (2 * (tm // sub)),
        compiler_params=_params("arbitrary", "arbitrary"),
        name="mixers_prompt",
    )(proj, proj, proj, proj, proj, proj, wb, cw, ar, ai, d, wglu, bglu,
      *[a for r in riders for a in r.arrays])


def _s5_step_kernel(u_ref, z_ref, x0r_ref, x0i_ref, wb_ref, cw_ref, ar_ref, ai_ref, d_ref,
                    wglu_ref, bglu_ref, o_ref, sr_ref, si_ref, y_scr):
    for blk in range(S5_BLOCKS):
        cols = slice(blk * LANES, (blk + 1) * LANES)
        ub = u_ref[:, cols]
        acc = d_ref[:, cols] * ub.astype(F32)
        for j in range(blk * PAIRS_PER_BLOCK, (blk + 1) * PAIRS_PER_BLOCK):
            pc = slice(j * LANES, (j + 1) * LANES)
            bu = jnp.dot(ub, wb_ref[j], preferred_element_type=F32)
            ar = ar_ref[j:j + 1, :]
            ai = ai_ref[j:j + 1, :]
            x0r = x0r_ref[:, pc]
            x0i = x0i_ref[:, pc]
            nr = ar * x0r - ai * x0i + bu[:, :LANES]
            ni = ar * x0i + ai * x0r + bu[:, LANES:]
            sr_ref[:, pc] = nr
            si_ref[:, pc] = ni
            x = jnp.concatenate([nr, ni], axis=1).astype(BF16)
            acc = acc + lax.dot_general(x, cw_ref[j], (((1,), (1,)), ((), ())),
                                        preferred_element_type=F32)
        y_scr[:, cols] = acc
    o_ref[...] = _s5_output_gate(y_scr[...], z_ref[...], wglu_ref, bglu_ref).astype(BF16)


def _s5_sample(proj, x0r, x0i, wb, cw, ar, ai, d, wglu, bglu):
    nb = proj.shape[0]
    ub_col = (4 * RET_WIDTH) // S5_WIDTH
    full = lambda a: pl.BlockSpec(a.shape, lambda i: (0,) * a.ndim)
    return pl.pallas_call(
        _s5_step_kernel,
        grid=(1,),
        in_specs=[
            pl.BlockSpec((nb, S5_WIDTH), lambda i: (0, ub_col)),
            pl.BlockSpec((nb, S5_WIDTH), lambda i: (0, ub_col + 1)),
            full(x0r), full(x0i), full(wb), full(cw), full(ar), full(ai), full(d), full(wglu), full(bglu),
        ],
        out_specs=[
            pl.BlockSpec((nb, S5_WIDTH), lambda i: (0, 0)),
            full(x0r), full(x0i),
        ],
        out_shape=[
            jax.ShapeDtypeStruct((nb, S5_WIDTH), BF16),
            jax.ShapeDtypeStruct(x0r.shape, F32),
            jax.ShapeDtypeStruct(x0i.shape, F32),
        ],
        scratch_shapes=[pltpu.VMEM((nb, S5_WIDTH), F32)],
        compiler_params=_params("arbitrary"),
        name="s5_sample",
    )(proj, proj, x0r, x0i, wb, cw, ar, ai, d, wglu, bglu)


def _merge_kernel(ya_ref, yb_ref, ga_ref, gb_ref, x_ref, wpa_ref, wpb_ref, wout_ref, gpost_ref, o_ref,
                  merged_scr):
    slabs = [slice(s * MXU_DIM, (s + 1) * MXU_DIM) for s in range(D_MODEL // MXU_DIM)]
    ya_in = ya_ref[...].astype(BF16)
    yb_in = yb_ref[...].astype(BF16)
    for cols in slabs:
        ya = jnp.dot(ya_in, wpa_ref[:, cols], preferred_element_type=F32)
        yb = jnp.dot(yb_in, wpb_ref[:, cols], preferred_element_type=F32)
        merged_scr[:, cols] = (jax.nn.sigmoid(ga_ref[:, cols].astype(F32)) * ya
                               + jax.nn.sigmoid(gb_ref[:, cols].astype(F32)) * yb).astype(BF16)
    sq = jnp.zeros((o_ref.shape[0], 1), F32)
    for cols in slabs:
        out = jnp.dot(merged_scr[...], wout_ref[:, cols], preferred_element_type=F32)
        sq = sq + jnp.sum(out * out, axis=-1, keepdims=True)
        o_ref[:, cols] = out
    inv_rms = lax.rsqrt(sq * (1.0 / D_MODEL) + EPS)
    for cols in slabs:
        o_ref[:, cols] = x_ref[:, cols] + o_ref[:, cols] * inv_rms * gpost_ref[:, cols]


def _merge(ya, yb, proj, x2d, wpa, wpb, wout, gpost, *, tm):
    m = x2d.shape[0]
    ga_col = (4 * RET_WIDTH + 2 * S5_WIDTH) // D_MODEL
    return pl.pallas_call(
        _merge_kernel,
        grid=(m // tm,),
        in_specs=[
            pl.BlockSpec((tm, RET_WIDTH), lambda i: (i, 0)),
            pl.BlockSpec((tm, S5_WIDTH), lambda i: (i, 0)),
            pl.BlockSpec((tm, D_MODEL), lambda i: (i, ga_col)),
            pl.BlockSpec((tm, D_MODEL), lambda i: (i, ga_col + 1)),
            pl.BlockSpec((tm, D_MODEL), lambda i: (i, 0)),
            _resident(wpa.shape), _resident(wpb.shape), _resident(wout.shape), _resident(gpost.shape),
        ],
        out_specs=pl.BlockSpec((tm, D_MODEL), lambda i: (i, 0)),
        out_shape=jax.ShapeDtypeStruct((m, D_MODEL), F32),
        scratch_shapes=[pltpu.VMEM((tm, D_MODEL), BF16)],
        compiler_params=_params("arbitrary"),
        name="merge_out",
    )(ya, yb, proj, proj, x2d, wpa, wpb, wout, gpost)


def _rope_tables(pos):
    half = HEAD_DIM // 2
    inv = ROPE_BASE ** (-jnp.arange(half, dtype=F32) / half)
    ang = pos.astype(F32)[:, None] * inv[None, :]
    cos, sin = jnp.cos(ang), jnp.sin(ang)
    return jnp.concatenate([cos, cos], axis=-1), jnp.concatenate([-sin, sin], axis=-1)


def _s5_discretize(lam_re, lam_im, log_dt, b_re, b_im):
    dt = jnp.exp(log_dt)[:, None]
    mag = jnp.exp(lam_re * dt)
    abar_r = mag * jnp.cos(lam_im * dt)
    abar_i = mag * jnp.sin(lam_im * dt)
    nr, ni = abar_r - 1.0, abar_i
    den = lam_re * lam_re + lam_im * lam_im
    coef_r = (nr * lam_re + ni * lam_im) / den
    coef_i = (ni * lam_re - nr * lam_im) / den
    bbar_r = coef_r[:, :, None] * b_re - coef_i[:, :, None] * b_im
    bbar_i = coef_r[:, :, None] * b_im + coef_i[:, :, None] * b_re
    return abar_r, abar_i, bbar_r, bbar_i


def _s5_pair_weights(bbar_r, bbar_i, c_re, c_im):
    j = lax.broadcasted_iota(jnp.int32, (S5_PAIRS, LANES, 2 * LANES), 0)
    row = lax.broadcasted_iota(jnp.int32, (S5_PAIRS, LANES, 2 * LANES), 1)
    col = lax.broadcasted_iota(jnp.int32, (S5_PAIRS, LANES, 2 * LANES), 2)
    pair_lanes = 2 * S5_GROUP
    keep = ((row // pair_lanes == j % PAIRS_PER_BLOCK)
            & ((row // S5_GROUP) % 2 == (col // S5_P) % 2))

    def expand(re, im, perm):
        base = jnp.stack([re, im]).reshape((2, S5_PAIRS, 2) + re.shape[1:]).transpose(perm)
        base = base.reshape(S5_PAIRS, 1, S5_GROUP, 2 * LANES)
        tiled = jnp.broadcast_to(base, (S5_PAIRS, LANES // S5_GROUP, S5_GROUP, 2 * LANES))
        return jnp.where(keep, tiled.reshape(S5_PAIRS, LANES, 2 * LANES), 0.0).astype(BF16)

    wb = expand(bbar_r, bbar_i, (1, 4, 0, 2, 3))
    cw = expand(c_re, -c_im, (1, 3, 0, 2, 4))
    return wb, cw


def kernel(x_prompt, x_sample, state_ret, state_s5_re, state_s5_im, g_pre, w_in, w_pa, w_pb, w_out, g_post,
           s5_lam_re, s5_lam_im, s5_log_dt, s5_b_re, s5_b_im, s5_c_re, s5_c_im, s5_d, s5_w_glu, s5_b_glu):
    assert w_in.shape[0] == 1, "single trunk layer"
    bp, lp, _ = x_prompt.shape
    bs, ls, _ = x_sample.shape
    assert ls == 1 and lp % RET_CHUNK == 0

    b_glu = s5_b_glu[0].reshape(1, S5_WIDTH)
    gpre = g_pre[0].reshape(1, D_MODEL)
    gpost = g_post[0].reshape(1, D_MODEL)

    abar_r, abar_i, bbar_r, bbar_i = _s5_discretize(
        s5_lam_re[0], s5_lam_im[0], s5_log_dt[0], s5_b_re[0], s5_b_im[0])
    wb, cw = _s5_pair_weights(bbar_r, bbar_i, s5_c_re[0], s5_c_im[0])
    ar = abar_r.reshape(S5_PAIRS, LANES)
    ai = abar_i.reshape(S5_PAIRS, LANES)
    d_skip = s5_d[0].reshape(1, S5_WIDTH)

    xs = x_sample.reshape(bs, D_MODEL)
    cos_s, sin_s = _rope_tables(jnp.full((bs,), PAST_LEN, jnp.int32))
    proj_s, w_in_b = _inproj(xs, gpre, w_in[0], cos_s, sin_s, tm=bs, tn=INPROJ_COLS_SAMPLE)

    xp = x_prompt.reshape(bp * lp, D_MODEL)
    cos_p, sin_p = _rope_tables(jnp.arange(lp, dtype=jnp.int32))
    proj_s_grouped = proj_s.astype(F32).reshape(bs // RET_SAMPLE_SEQS, RET_SAMPLE_SEQS, IN_COLS)
    proj_p, w_glu_b, ya_s, ret_s = _inproj(
        xp, gpre, w_in_b, cos_p, sin_p, tm=INPROJ_ROWS, tn=INPROJ_COLS_PROMPT,
        make_riders=[functools.partial(_cast_rider, s5_w_glu[0]),
                     functools.partial(_retention_step_rider, proj_s_grouped, state_ret[0])])
    ya_s = ya_s.reshape(bs, RET_WIDTH)
    ya_p, ret_p, yb_p, s5r_p, s5i_p, w_pa_b, w_pb_b, w_out_b = _mixers_prompt(
        proj_p, wb, cw, ar, ai, d_skip, w_glu_b, b_glu, batch=bp, seq=lp, tm=MIXER_ROWS,
        sub=S5_SUBTILE_ROWS,
        make_riders=[functools.partial(_cast_rider, w) for w in (w_pa[0], w_pb[0], w_out[0])])
    y_p = _merge(ya_p, yb_p, proj_p, xp, w_pa_b, w_pb_b, w_out_b, gpost, tm=MERGE_ROWS)

    x0r = state_s5_re[0].reshape(bs, S5_GROUPS * S5_P)
    x0i = state_s5_im[0].reshape(bs, S5_GROUPS * S5_P)
    yb_s, s5r_s, s5i_s = _s5_sample(proj_s, x0r, x0i, wb, cw, ar, ai, d_skip, w_glu_b, b_glu)
    y_s = _merge(ya_s, yb_s, proj_s, xs, w_pa_b, w_pb_b, w_out_b, gpost, tm=bs)

    state_shape = (1, -1, S5_GROUPS, S5_P)
    return (y_p.reshape(bp, lp, D_MODEL), y_s.reshape(bs, 1, D_MODEL),
            ret_p[None], s5r_p.reshape(state_shape), s5i_p.reshape(state_shape),
            ret_s[None], s5r_s.reshape(state_shape), s5i_s.reshape(state_shape))
```

```python
import functools
import math
from typing import Callable, NamedTuple

import jax
import jax.numpy as jnp
import numpy as np
from jax import lax
from jax.experimental import pallas as pl
from jax.experimental.pallas import tpu as pltpu

F32 = jnp.float32
BF16 = jnp.bfloat16

D_MODEL = 2048
RET_WIDTH = 1024
RET_HEADS = 8
HEAD_DIM = RET_WIDTH // RET_HEADS
RET_CHUNK = 128
ROPE_BASE = 10000.0
S5_WIDTH = 1024
S5_GROUP = 16
S5_GROUPS = S5_WIDTH // S5_GROUP
S5_P = 64
PAST_LEN = 16384
EPS = 1e-6
IN_COLS = 4 * RET_WIDTH + 2 * S5_WIDTH + 2 * D_MODEL

LANES = 128
MXU_DIM = 256
BF16_SUBLANES = 16
S5_PAIRS = S5_GROUPS // 2
PAIRS_PER_BLOCK = LANES // (2 * S5_GROUP)
S5_BLOCKS = S5_WIDTH // LANES
S5_PITCH = 36
VMEM_LIMIT = 60 * 1024 * 1024

INPROJ_ROWS = 1024
INPROJ_COLS_PROMPT = 2048
INPROJ_COLS_SAMPLE = 1024
MIXER_ROWS = 512
S5_SUBTILE_ROWS = 128
MERGE_ROWS = 512
RET_SAMPLE_SEQS = 4

_LOG_GAMMA = [float(np.log(np.float32(1.0) - np.float32(2.0) ** np.float32(-5.0 - h)))
              for h in range(RET_HEADS)]


def _params(*sem):
    return pltpu.CompilerParams(dimension_semantics=sem, vmem_limit_bytes=VMEM_LIMIT)


def _resident(shape):
    nd = len(shape)
    return pl.BlockSpec(shape, lambda *_: (0,) * nd, pipeline_mode=pl.Buffered(1))


def _rope(x, cos, sin_signed):
    return x * cos + pltpu.roll(x, HEAD_DIM // 2, 1) * sin_signed


class _Rider(NamedTuple):
    arrays: tuple
    in_specs: tuple
    out_specs: tuple
    out_shapes: tuple
    body: Callable
    n_units: int


def _inproj_kernel(x_ref, g_ref, w_ref, cos_ref, sin_ref, *rest, slab, tn, emit_weights, riders):
    n_in = sum(len(r.in_specs) for r in riders)
    n_out = sum(len(r.out_specs) for r in riders)
    rider_in, rest = rest[:n_in], rest[n_in:]
    o_ref, h_ref = rest[0], rest[-1]
    wcopy_ref = rest[1] if emit_weights else None
    rider_out = rest[len(rest) - 1 - n_out:-1]
    j = pl.program_id(1)

    def rider_units():
        i = o = 0
        for r in riders:
            yield from r.body(rider_in[i:i + len(r.in_specs)], rider_out[o:o + len(r.out_specs)])
            i, o = i + len(r.in_specs), o + len(r.out_specs)

    @pl.when(j == 0)
    def _():
        g = g_ref[...]

        def body(r, _):
            rows = pl.ds(pl.multiple_of(r * slab, slab), slab)
            x = x_ref[rows, :]
            ms = jnp.mean(x * x, axis=-1, keepdims=True)
            h_ref[rows, :] = (x * lax.rsqrt(ms + EPS) * g).astype(BF16)
            return 0

        lax.fori_loop(0, x_ref.shape[0] // slab, body, 0)

    def rotate(s, acc):
        is_k = j * tn + s * MXU_DIM >= RET_WIDTH
        scale = jnp.where(is_k, HEAD_DIM ** -0.5, 1.0).astype(F32)
        cos = cos_ref[...] * scale
        sin = sin_ref[...] * scale
        return jnp.concatenate([_rope(acc[:, hh * HEAD_DIM:(hh + 1) * HEAD_DIM], cos, sin)
                                for hh in range(MXU_DIM // HEAD_DIM)], axis=1)

    epilogues = {"rope": rotate, "plain": lambda s, acc: acc}
    col_kinds = (("rope", 2 * RET_WIDTH), ("plain", IN_COLS - 2 * RET_WIDTH))
    slab_kinds = [kind for kind, width in col_kinds for _ in range(width // MXU_DIM)]
    per_tile = tn // MXU_DIM
    tiles_by_pattern = {}
    for tile in range(len(slab_kinds) // per_tile):
        pattern = tuple(slab_kinds[tile * per_tile:(tile + 1) * per_tile])
        tiles_by_pattern.setdefault(pattern, []).append(tile)

    def slabs(pattern):
        units, n_units, emitted = rider_units(), sum(r.n_units for r in riders), 0
        for s, kind in enumerate(pattern):
            cols = slice(s * MXU_DIM, (s + 1) * MXU_DIM)
            w = w_ref[:, cols]
            if wcopy_ref is not None:
                w = w.astype(BF16)
                wcopy_ref[:, cols] = w
            acc = jnp.dot(h_ref[...], w, preferred_element_type=F32)
            while emitted * len(pattern) < (s + 1) * n_units:
                next(units)
                emitted += 1
            o_ref[:, cols] = epilogues[kind](s, acc).astype(BF16)
        assert next(units, None) is None, "a rider emitted more units than it declared"

    for pattern, tiles in tiles_by_pattern.items():
        is_this_kind = functools.reduce(jnp.logical_or, [j == tile for tile in tiles])
        pl.when(is_this_kind)(functools.partial(slabs, pattern))


def _cast_rider(a, step_of, n_steps):
    rows = -(-a.shape[0] // n_steps)
    rows = -(-rows // BF16_SUBLANES) * BF16_SUBLANES
    assert a.shape[0] % rows == 0
    last = a.shape[0] // rows - 1
    spec = pl.BlockSpec((rows, a.shape[1]), lambda i, j: (jnp.minimum(step_of(i, j), last), 0))

    def body(ins, outs):
        outs[0][...] = ins[0][...].astype(BF16)
        yield

    return _Rider((a,), (spec,), (spec,), (jax.ShapeDtypeStruct(a.shape, BF16),), body, 1)


def _inproj(x2d, g_pre, w_in, cos, sin, *, tm, tn, make_riders=()):
    m, d = x2d.shape
    n = w_in.shape[1]
    pos_tiles = cos.shape[0] // tm
    emit_weights = w_in.dtype != BF16
    assert not emit_weights or m == tm, "each weight tile must be visited once to be copied out"
    n_col_tiles = n // tn
    n_steps = (m // tm) * n_col_tiles
    riders = tuple(make(lambda i, j: i * n_col_tiles + j, n_steps) for make in make_riders)
    in_specs = [
        pl.BlockSpec((tm, d), lambda i, j: (i, 0)),
        pl.BlockSpec((1, d), lambda i, j: (0, 0)),
        pl.BlockSpec((d, tn), lambda i, j: (0, j)),
        pl.BlockSpec((tm, HEAD_DIM), lambda i, j: (i % pos_tiles, 0)),
        pl.BlockSpec((tm, HEAD_DIM), lambda i, j: (i % pos_tiles, 0)),
    ]
    out_specs = [pl.BlockSpec((tm, tn), lambda i, j: (i, j))]
    out_shape = [jax.ShapeDtypeStruct((m, n), BF16)]
    if emit_weights:
        out_specs.append(pl.BlockSpec((d, tn), lambda i, j: (0, j)))
        out_shape.append(jax.ShapeDtypeStruct((d, n), BF16))
    for r in riders:
        in_specs += r.in_specs
        out_specs += r.out_specs
        out_shape += r.out_shapes
    return pl.pallas_call(
        functools.partial(_inproj_kernel, slab=min(tm, 128), tn=tn, emit_weights=emit_weights,
                          riders=riders),
        grid=(m // tm, n_col_tiles),
        in_specs=in_specs,
        out_specs=out_specs,
        out_shape=out_shape,
        scratch_shapes=[pltpu.VMEM((tm, d), BF16)],
        compiler_params=_params("arbitrary", "arbitrary"),
        name="inproj",
    )(x2d, g_pre, w_in, cos, sin, *[a for r in riders for a in r.arrays])


def _group_norm(o):
    mu = jnp.mean(o, axis=-1, keepdims=True)
    d = o - mu
    var = jnp.mean(d * d, axis=-1, keepdims=True)
    return d * lax.rsqrt(var + EPS)


def _retention_tile(q_ref, k_ref, v_ref, z_ref, o_ref, s_ref):
    c = RET_CHUNK
    row = lax.broadcasted_iota(jnp.int32, (c, c), 0).astype(F32)
    col = lax.broadcasted_iota(jnp.int32, (c, c), 1).astype(F32)
    diff = row - col
    for h in range(RET_HEADS):
        lg = _LOG_GAMMA[h]
        mask = jnp.where(diff >= 0, jnp.exp(jnp.maximum(diff, 0.0) * lg), 0.0)
        q_decay = jnp.exp((row + 1.0) * lg)
        k_decay = jnp.exp((c - 1.0 - row) * lg)
        chunk_decay = math.exp(c * lg)
        cols = slice(h * HEAD_DIM, (h + 1) * HEAD_DIM)
        for ci in range(q_ref.shape[0] // c):
            rows = slice(ci * c, (ci + 1) * c)
            q = q_ref[rows, cols]
            k = k_ref[rows, cols]
            v = v_ref[rows, cols]
            s0 = s_ref[0, h]
            scores = lax.dot_general(q, k, (((1,), (1,)), ((), ())),
                                     preferred_element_type=F32) * mask
            inner = jnp.dot(scores.astype(BF16), v, preferred_element_type=F32)
            cross = jnp.dot(q, s0.astype(BF16), preferred_element_type=F32) * q_decay
            kd = (k.astype(F32) * k_decay).astype(BF16)
            s_ref[0, h] = chunk_decay * s0 + lax.dot_general(
                kd, v, (((0,), (0,)), ((), ())), preferred_element_type=F32)
            o = _group_norm(inner + cross)
            o_ref[rows, cols] = (o * jax.nn.silu(z_ref[rows, cols].astype(F32))).astype(BF16)
            yield


def _retention_step_rider(proj, state, step_of, n_steps):
    groups, bb, _ = proj.shape
    assert groups <= n_steps
    group = lambda i, j: jnp.minimum(step_of(i, j), groups - 1)
    col_spec = lambda cb: pl.BlockSpec((1, bb, RET_WIDTH), lambda i, j: (group(i, j), 0, cb))
    state_spec = pl.BlockSpec((bb, RET_HEADS, HEAD_DIM, HEAD_DIM), lambda i, j: (group(i, j), 0, 0, 0))

    def body(ins, outs):
        q_ref, k_ref, v_ref, z_ref, s_ref = ins
        o_ref, sn_ref = outs
        pad = jnp.zeros((HEAD_DIM - bb, HEAD_DIM), F32)
        for h in range(RET_HEADS):
            gamma = math.exp(_LOG_GAMMA[h])
            cols = slice(h * HEAD_DIM, (h + 1) * HEAD_DIM)
            q, k, v = q_ref[0, :, cols], k_ref[0, :, cols], v_ref[0, :, cols]
            qt = jnp.concatenate([q, pad], axis=0).T
            kt = jnp.concatenate([k, pad], axis=0).T
            o = []
            for b in range(bb):
                s_new = gamma * s_ref[b, h] + kt[:, b:b + 1] * v[b:b + 1, :]
                sn_ref[b, h] = s_new
                o.append(jnp.sum(qt[:, b:b + 1] * s_new, axis=0, keepdims=True))
            o_ref[0, :, cols] = _group_norm(jnp.concatenate(o, axis=0)) * jax.nn.silu(z_ref[0, :, cols])
            yield

    return _Rider(
        (proj, proj, proj, proj, state),
        (col_spec(0), col_spec(1), col_spec(2), col_spec(3), state_spec),
        (pl.BlockSpec((1, bb, RET_WIDTH), lambda i, j: (group(i, j), 0, 0)), state_spec),
        (jax.ShapeDtypeStruct((groups, bb, RET_WIDTH), F32), jax.ShapeDtypeStruct(state.shape, F32)),
        body, RET_HEADS)


def _s5_output_gate(y, z, wglu_ref, bglu_ref):
    y = jax.nn.gelu(y)
    g = jnp.dot(y.astype(BF16), wglu_ref[...], preferred_element_type=F32) + bglu_ref[...]
    return y * jax.nn.sigmoid(g) * jax.nn.silu(z.astype(F32))


def _s5_tile(u_ref, z_ref, wb_ref, cw_ref, ar_ref, ai_ref, d_ref, wglu_ref, bglu_ref,
             o_ref, sr_ref, si_ref, y_scr, x_scr, *, sub, filler=(), n_filler=0):
    n_sub = len(x_scr) // 2
    xr_scr, xi_scr = x_scr[:n_sub], x_scr[n_sub:]
    strided = lambda j: pl.ds(j, sub, stride=S5_PITCH)
    filler = iter(filler)
    progress = [0, 0]
    n_slots = n_sub * S5_PAIRS

    def fill():
        progress[0] += 1
        while progress[1] < n_filler and progress[1] * n_slots < progress[0] * n_filler:
            next(filler)
            progress[1] += 1

    for k in range(n_sub):
        rows = slice(k * sub, (k + 1) * sub)
        for blk in range(S5_BLOCKS):
            ub = u_ref[rows, blk * LANES:(blk + 1) * LANES]
            for j in range(blk * PAIRS_PER_BLOCK, (blk + 1) * PAIRS_PER_BLOCK):
                bu = jnp.dot(ub, wb_ref[j], preferred_element_type=F32)
                xr_scr[k][strided(j), :] = bu[:, :LANES]
                xi_scr[k][strided(j), :] = bu[:, LANES:]
                fill()
    ar = ar_ref[...]
    ai = ai_ref[...]
    xr, xi = sr_ref[0], si_ref[0]
    for k in range(n_sub):
        rows = slice(k * sub, (k + 1) * sub)
        for t in range(sub):
            step = slice(t * S5_PITCH, t * S5_PITCH + S5_PAIRS)
            xr, xi = (ar * xr - ai * xi + xr_scr[k][step, :],
                      ar * xi + ai * xr + xi_scr[k][step, :])
            xr_scr[k][step, :] = xr
            xi_scr[k][step, :] = xi
        for blk in range(S5_BLOCKS):
            cols = slice(blk * LANES, (blk + 1) * LANES)
            acc = d_ref[:, cols] * u_ref[rows, cols].astype(F32)
            for j in range(blk * PAIRS_PER_BLOCK, (blk + 1) * PAIRS_PER_BLOCK):
                x = jnp.concatenate([xr_scr[k][strided(j), :], xi_scr[k][strided(j), :]], axis=1)
                acc = acc + lax.dot_general(x.astype(BF16), cw_ref[j], (((1,), (1,)), ((), ())),
                                            preferred_element_type=F32)
            y_scr[rows, cols] = acc
        o_ref[rows, :] = _s5_output_gate(
            y_scr[rows, :], z_ref[rows, :], wglu_ref, bglu_ref).astype(BF16)
    sr_ref[0] = xr
    si_ref[0] = xi


def _mixers_kernel(q_ref, k_ref, v_ref, za_ref, u_ref, zb_ref, wb_ref, cw_ref, ar_ref, ai_ref, d_ref,
                   wglu_ref, bglu_ref, ya_ref, ret_ref, yb_ref, sr_ref, si_ref, y_scr, *x_scr, sub):
    @pl.when(pl.program_id(1) == 0)
    def _():
        ret_ref[...] = jnp.zeros_like(ret_ref)
        sr_ref[...] = jnp.zeros_like(sr_ref)
        si_ref[...] = jnp.zeros_like(si_ref)

    retention_units = RET_HEADS * (q_ref.shape[0] // RET_CHUNK)
    _s5_tile(u_ref, zb_ref, wb_ref, cw_ref, ar_ref, ai_ref, d_ref, wglu_ref, bglu_ref,
             yb_ref, sr_ref, si_ref, y_scr, x_scr, sub=sub,
             filler=_retention_tile(q_ref, k_ref, v_ref, za_ref, ya_ref, ret_ref),
             n_filler=retention_units)


def _mixers_prompt(proj, wb, cw, ar, ai, d, wglu, bglu, *, batch, seq, tm, sub):
    nt = seq // tm
    row_block = lambda b, i: b * nt + i
    col_spec = lambda cb: pl.BlockSpec((tm, RET_WIDTH), lambda b, i: (row_block(b, i), cb))
    s5_state_spec = pl.BlockSpec((1, S5_PAIRS, LANES), lambda b, i: (b, 0, 0))
    assert RET_WIDTH == S5_WIDTH
    return pl.pallas_call(
        functools.partial(_mixers_kernel, sub=sub),
        grid=(batch, nt),
        in_specs=[
            col_spec(0), col_spec(1), col_spec(2), col_spec(3), col_spec(4), col_spec(5),
            _resident(wb.shape), _resident(cw.shape), _resident(ar.shape), _resident(ai.shape),
            _resident(d.shape), _resident(wglu.shape), _resident(bglu.shape),
        ],
        out_specs=[
            pl.BlockSpec((tm, RET_WIDTH), lambda b, i: (row_block(b, i), 0)),
            pl.BlockSpec((1, RET_HEADS, HEAD_DIM, HEAD_DIM), lambda b, i: (b, 0, 0, 0)),
            pl.BlockSpec((tm, S5_WIDTH), lambda b, i: (row_block(b, i), 0)),
            s5_state_spec, s5_state_spec,
        ],
        out_shape=[
            jax.ShapeDtypeStruct((batch * seq, RET_WIDTH), BF16),
            jax.ShapeDtypeStruct((batch, RET_HEADS, HEAD_DIM, HEAD_DIM), F32),
            jax.ShapeDtypeStruct((batch * seq, S5_WIDTH), BF16),
            jax.ShapeDtypeStruct((batch, S5_PAIRS, LANES), F32),
            jax.ShapeDtypeStruct((batch, S5_PAIRS, LANES), F32),
        ],
        scratch_shapes=[pltpu.VMEM((tm, S5_WIDTH), F32)]
        + [pltpu.VMEM((sub * S5_PITCH, LANES), F32)] * (2 * (tm // sub)),
        compiler_params=_params("arbitrary", "arbitrary"),
        name="mixers_prompt",
    )(proj, proj, proj, proj, proj, proj, wb, cw, ar, ai, d, wglu, bglu)


def _s5_step_kernel(u_ref, z_ref, x0r_ref, x0i_ref, wb_ref, cw_ref, ar_ref, ai_ref, d_ref,
                    wglu_ref, bglu_ref, o_ref, sr_ref, si_ref, y_scr):
    for blk in range(S5_BLOCKS):
        cols = slice(blk * LANES, (blk + 1) * LANES)
        ub = u_ref[:, cols]
        acc = d_ref[:, cols] * ub.astype(F32)
        for j in range(blk * PAIRS_PER_BLOCK, (blk + 1) * PAIRS_PER_BLOCK):
            pc = slice(j * LANES, (j + 1) * LANES)
            bu = jnp.dot(ub, wb_ref[j], preferred_element_type=F32)
            ar = ar_ref[j:j + 1, :]
            ai = ai_ref[j:j + 1, :]
            x0r = x0r_ref[:, pc]
            x0i = x0i_ref[:, pc]
            nr = ar * x0r - ai * x0i + bu[:, :LANES]
            ni = ar * x0i + ai * x0r + bu[:, LANES:]
            sr_ref[:, pc] = nr
            si_ref[:, pc] = ni
            x = jnp.concatenate([nr, ni], axis=1).astype(BF16)
            acc = acc + lax.dot_general(x, cw_ref[j], (((1,), (1,)), ((), ())),
                                        preferred_element_type=F32)
        y_scr[:, cols] = acc
    o_ref[...] = _s5_output_gate(y_scr[...], z_ref[...], wglu_ref, bglu_ref).astype(BF16)


def _s5_sample(proj, x0r, x0i, wb, cw, ar, ai, d, wglu, bglu):
    nb = proj.shape[0]
    ub_col = (4 * RET_WIDTH) // S5_WIDTH
    full = lambda a: pl.BlockSpec(a.shape, lambda i: (0,) * a.ndim)
    return pl.pallas_call(
        _s5_step_kernel,
        grid=(1,),
        in_specs=[
            pl.BlockSpec((nb, S5_WIDTH), lambda i: (0, ub_col)),
            pl.BlockSpec((nb, S5_WIDTH), lambda i: (0, ub_col + 1)),
            full(x0r), full(x0i), full(wb), full(cw), full(ar), full(ai), full(d), full(wglu), full(bglu),
        ],
        out_specs=[
            pl.BlockSpec((nb, S5_WIDTH), lambda i: (0, 0)),
            full(x0r), full(x0i),
        ],
        out_shape=[
            jax.ShapeDtypeStruct((nb, S5_WIDTH), BF16),
            jax.ShapeDtypeStruct(x0r.shape, F32),
            jax.ShapeDtypeStruct(x0i.shape, F32),
        ],
        scratch_shapes=[pltpu.VMEM((nb, S5_WIDTH), F32)],
        compiler_params=_params("arbitrary"),
        name="s5_sample",
    )(proj, proj, x0r, x0i, wb, cw, ar, ai, d, wglu, bglu)


def _merge_kernel(ya_ref, yb_ref, ga_ref, gb_ref, x_ref, wpa_ref, wpb_ref, wout_ref, gpost_ref, o_ref,
                  merged_scr):
    slabs = [slice(s * MXU_DIM, (s + 1) * MXU_DIM) for s in range(D_MODEL // MXU_DIM)]
    ya_in = ya_ref[...].astype(BF16)
    yb_in = yb_ref[...].astype(BF16)
    for cols in slabs:
        ya = jnp.dot(ya_in, wpa_ref[:, cols], preferred_element_type=F32)
        yb = jnp.dot(yb_in, wpb_ref[:, cols], preferred_element_type=F32)
        merged_scr[:, cols] = (jax.nn.sigmoid(ga_ref[:, cols].astype(F32)) * ya
                               + jax.nn.sigmoid(gb_ref[:, cols].astype(F32)) * yb).astype(BF16)
    sq = jnp.zeros((o_ref.shape[0], 1), F32)
    for cols in slabs:
        out = jnp.dot(merged_scr[...], wout_ref[:, cols], preferred_element_type=F32)
        sq = sq + jnp.sum(out * out, axis=-1, keepdims=True)
        o_ref[:, cols] = out
    inv_rms = lax.rsqrt(sq * (1.0 / D_MODEL) + EPS)
    for cols in slabs:
        o_ref[:, cols] = x_ref[:, cols] + o_ref[:, cols] * inv_rms * gpost_ref[:, cols]


def _merge(ya, yb, proj, x2d, wpa, wpb, wout, gpost, *, tm):
    m = x2d.shape[0]
    ga_col = (4 * RET_WIDTH + 2 * S5_WIDTH) // D_MODEL
    return pl.pallas_call(
        _merge_kernel,
        grid=(m // tm,),
        in_specs=[
            pl.BlockSpec((tm, RET_WIDTH), lambda i: (i, 0)),
            pl.BlockSpec((tm, S5_WIDTH), lambda i: (i, 0)),
            pl.BlockSpec((tm, D_MODEL), lambda i: (i, ga_col)),
            pl.BlockSpec((tm, D_MODEL), lambda i: (i, ga_col + 1)),
            pl.BlockSpec((tm, D_MODEL), lambda i: (i, 0)),
            _resident(wpa.shape), _resident(wpb.shape), _resident(wout.shape), _resident(gpost.shape),
        ],
        out_specs=pl.BlockSpec((tm, D_MODEL), lambda i: (i, 0)),
        out_shape=jax.ShapeDtypeStruct((m, D_MODEL), F32),
        scratch_shapes=[pltpu.VMEM((tm, D_MODEL), BF16)],
        compiler_params=_params("arbitrary"),
        name="merge_out",
    )(ya, yb, proj, proj, x2d, wpa, wpb, wout, gpost)


def _rope_tables(pos):
    half = HEAD_DIM // 2
    inv = ROPE_BASE ** (-jnp.arange(half, dtype=F32) / half)
    ang = pos.astype(F32)[:, None] * inv[None, :]
    cos, sin = jnp.cos(ang), jnp.sin(ang)
    return jnp.concatenate([cos, cos], axis=-1), jnp.concatenate([-sin, sin], axis=-1)


def _s5_discretize(lam_re, lam_im, log_dt, b_re, b_im):
    dt = jnp.exp(log_dt)[:, None]
    mag = jnp.exp(lam_re * dt)
    abar_r = mag * jnp.cos(lam_im * dt)
    abar_i = mag * jnp.sin(lam_im * dt)
    nr, ni = abar_r - 1.0, abar_i
    den = lam_re * lam_re + lam_im * lam_im
    coef_r = (nr * lam_re + ni * lam_im) / den
    coef_i = (ni * lam_re - nr * lam_im) / den
    bbar_r = coef_r[:, :, None] * b_re - coef_i[:, :, None] * b_im
    bbar_i = coef_r[:, :, None] * b_im + coef_i[:, :, None] * b_re
    return abar_r, abar_i, bbar_r, bbar_i


def _s5_pair_weights(bbar_r, bbar_i, c_re, c_im):
    j = lax.broadcasted_iota(jnp.int32, (S5_PAIRS, LANES, 2 * LANES), 0)
    row = lax.broadcasted_iota(jnp.int32, (S5_PAIRS, LANES, 2 * LANES), 1)
    col = lax.broadcasted_iota(jnp.int32, (S5_PAIRS, LANES, 2 * LANES), 2)
    pair_lanes = 2 * S5_GROUP
    keep = ((row // pair_lanes == j % PAIRS_PER_BLOCK)
            & ((row // S5_GROUP) % 2 == (col // S5_P) % 2))

    def expand(re, im, perm):
        base = jnp.stack([re, im]).reshape((2, S5_PAIRS, 2) + re.shape[1:]).transpose(perm)
        base = base.reshape(S5_PAIRS, 1, S5_GROUP, 2 * LANES)
        tiled = jnp.broadcast_to(base, (S5_PAIRS, LANES // S5_GROUP, S5_GROUP, 2 * LANES))
        return jnp.where(keep, tiled.reshape(S5_PAIRS, LANES, 2 * LANES), 0.0).astype(BF16)

    wb = expand(bbar_r, bbar_i, (1, 4, 0, 2, 3))
    cw = expand(c_re, -c_im, (1, 3, 0, 2, 4))
    return wb, cw


def kernel(x_prompt, x_sample, state_ret, state_s5_re, state_s5_im, g_pre, w_in, w_pa, w_pb, w_out, g_post,
           s5_lam_re, s5_lam_im, s5_log_dt, s5_b_re, s5_b_im, s5_c_re, s5_c_im, s5_d, s5_w_glu, s5_b_glu):
    assert w_in.shape[0] == 1, "single trunk layer"
    bp, lp, _ = x_prompt.shape
    bs, ls, _ = x_sample.shape
    assert ls == 1 and lp % RET_CHUNK == 0

    b_glu = s5_b_glu[0].reshape(1, S5_WIDTH)
    gpre = g_pre[0].reshape(1, D_MODEL)
    gpost = g_post[0].reshape(1, D_MODEL)

    abar_r, abar_i, bbar_r, bbar_i = _s5_discretize(
        s5_lam_re[0], s5_lam_im[0], s5_log_dt[0], s5_b_re[0], s5_b_im[0])
    wb, cw = _s5_pair_weights(bbar_r, bbar_i, s5_c_re[0], s5_c_im[0])
    ar = abar_r.reshape(S5_PAIRS, LANES)
    ai = abar_i.reshape(S5_PAIRS, LANES)
    d_skip = s5_d[0].reshape(1, S5_WIDTH)

    xs = x_sample.reshape(bs, D_MODEL)
    cos_s, sin_s = _rope_tables(jnp.full((bs,), PAST_LEN, jnp.int32))
    proj_s, w_in_b = _inproj(xs, gpre, w_in[0], cos_s, sin_s, tm=bs, tn=INPROJ_COLS_SAMPLE)

    xp = x_prompt.reshape(bp * lp, D_MODEL)
    cos_p, sin_p = _rope_tables(jnp.arange(lp, dtype=jnp.int32))
    proj_s_grouped = proj_s[:, :4 * RET_WIDTH].astype(F32).reshape(
        bs // RET_SAMPLE_SEQS, RET_SAMPLE_SEQS, 4 * RET_WIDTH)
    proj_p, w_pa_b, w_pb_b, w_out_b, w_glu_b, ya_s, ret_s = _inproj(
        xp, gpre, w_in_b, cos_p, sin_p, tm=INPROJ_ROWS, tn=INPROJ_COLS_PROMPT,
        make_riders=[functools.partial(_cast_rider, w) for w in (w_pa[0], w_pb[0], w_out[0], s5_w_glu[0])]
        + [functools.partial(_retention_step_rider, proj_s_grouped, state_ret[0])])
    ya_s = ya_s.reshape(bs, RET_WIDTH)
    ya_p, ret_p, yb_p, s5r_p, s5i_p = _mixers_prompt(
        proj_p, wb, cw, ar, ai, d_skip, w_glu_b, b_glu, batch=bp, seq=lp, tm=MIXER_ROWS,
        sub=S5_SUBTILE_ROWS)
    y_p = _merge(ya_p, yb_p, proj_p, xp, w_pa_b, w_pb_b, w_out_b, gpost, tm=MERGE_ROWS)

    x0r = state_s5_re[0].reshape(bs, S5_GROUPS * S5_P)
    x0i = state_s5_im[0].reshape(bs, S5_GROUPS * S5_P)
    yb_s, s5r_s, s5i_s = _s5_sample(proj_s, x0r, x0i, wb, cw, ar, ai, d_skip, w_glu_b, b_glu)
    y_s = _merge(ya_s, yb_s, proj_s, xs, w_pa_b, w_pb_b, w_out_b, gpost, tm=bs)

    state_shape = (1, -1, S5_GROUPS, S5_P)
    return (y_p.reshape(bp, lp, D_MODEL), y_s.reshape(bs, 1, D_MODEL),
            ret_p[None], s5r_p.reshape(state_shape), s5i_p.reshape(state_shape),
            ret_s[None], s5r_s.reshape(state_shape), s5i_s.reshape(state_shape))
```

```python
import functools
import math
from typing import Callable, NamedTuple

import jax
import jax.numpy as jnp
import numpy as np
from jax import lax
from jax.experimental import pallas as pl
from jax.experimental.pallas import tpu as pltpu

F32 = jnp.float32
BF16 = jnp.bfloat16

D_MODEL = 2048
RET_WIDTH = 1024
RET_HEADS = 8
HEAD_DIM = RET_WIDTH // RET_HEADS
RET_CHUNK = 128
ROPE_BASE = 10000.0
S5_WIDTH = 1024
S5_GROUP = 16
S5_GROUPS = S5_WIDTH // S5_GROUP
S5_P = 64
PAST_LEN = 16384
EPS = 1e-6
IN_COLS = 4 * RET_WIDTH + 2 * S5_WIDTH + 2 * D_MODEL

LANES = 128
MXU_DIM = 256
BF16_SUBLANES = 16
S5_PAIRS = S5_GROUPS // 2
PAIRS_PER_BLOCK = LANES // (2 * S5_GROUP)
S5_BLOCKS = S5_WIDTH // LANES
S5_PITCH = 36
VMEM_LIMIT = 60 * 1024 * 1024

INPROJ_ROWS = 1024
INPROJ_COLS_PROMPT = 2048
INPROJ_COLS_SAMPLE = 1024
MIXER_ROWS = 512
S5_SUBTILE_ROWS = 128
MERGE_ROWS = 512
RET_SAMPLE_SEQS = 4

_LOG_GAMMA = [float(np.log(np.float32(1.0) - np.float32(2.0) ** np.float32(-5.0 - h)))
              for h in range(RET_HEADS)]


def _params(*sem):
    return pltpu.CompilerParams(dimension_semantics=sem, vmem_limit_bytes=VMEM_LIMIT)


def _resident(shape):
    nd = len(shape)
    return pl.BlockSpec(shape, lambda *_: (0,) * nd, pipeline_mode=pl.Buffered(1))


def _rope(x, cos, sin_signed):
    return x * cos + pltpu.roll(x, HEAD_DIM // 2, 1) * sin_signed


class _Rider(NamedTuple):
    arrays: tuple
    in_specs: tuple
    out_specs: tuple
    out_shapes: tuple
    body: Callable
    n_units: int
    skips_first_tile: bool = False


class _HostSteps(NamedTuple):
    step_of: Callable
    n_steps: int
    later_step_of: Callable
    n_later_steps: int


def _inproj_kernel(x_ref, g_ref, w_ref, cos_ref, sin_ref, *rest, slab, tn, emit_weights, riders):
    n_in = sum(len(r.in_specs) for r in riders)
    n_out = sum(len(r.out_specs) for r in riders)
    rider_in, rest = rest[:n_in], rest[n_in:]
    o_ref, h_ref = rest[0], rest[-1]
    wcopy_ref = rest[1] if emit_weights else None
    rider_out = rest[len(rest) - 1 - n_out:-1]
    j = pl.program_id(1)

    def rider_units(active):
        i = o = 0
        for r, is_active in zip(riders, active):
            if is_active:
                yield from r.body(rider_in[i:i + len(r.in_specs)], rider_out[o:o + len(r.out_specs)])
            i, o = i + len(r.in_specs), o + len(r.out_specs)

    @pl.when(j == 0)
    def _():
        g = g_ref[...]

        def body(r, _):
            rows = pl.ds(pl.multiple_of(r * slab, slab), slab)
            x = x_ref[rows, :]
            ms = jnp.mean(x * x, axis=-1, keepdims=True)
            h_ref[rows, :] = (x * lax.rsqrt(ms + EPS) * g).astype(BF16)
            return 0

        lax.fori_loop(0, x_ref.shape[0] // slab, body, 0)

    def rotate(s, acc):
        is_k = j * tn + s * MXU_DIM >= RET_WIDTH
        scale = jnp.where(is_k, HEAD_DIM ** -0.5, 1.0).astype(F32)
        cos = cos_ref[...] * scale
        sin = sin_ref[...] * scale
        return jnp.concatenate([_rope(acc[:, hh * HEAD_DIM:(hh + 1) * HEAD_DIM], cos, sin)
                                for hh in range(MXU_DIM // HEAD_DIM)], axis=1)

    epilogues = {"rope": rotate, "plain": lambda s, acc: acc}
    col_kinds = (("rope", 2 * RET_WIDTH), ("plain", IN_COLS - 2 * RET_WIDTH))
    slab_kinds = [kind for kind, width in col_kinds for _ in range(width // MXU_DIM)]
    per_tile = tn // MXU_DIM
    tiles_by_pattern = {}
    for tile in range(len(slab_kinds) // per_tile):
        pattern = tuple(slab_kinds[tile * per_tile:(tile + 1) * per_tile])
        tiles_by_pattern.setdefault(pattern, []).append(tile)

    def slabs(pattern, tiles):
        some_skip = any(r.skips_first_tile for r in riders)
        assert not (some_skip and 0 in tiles and len(tiles) > 1), "tile 0 needs its own branch"
        active = [not (r.skips_first_tile and tiles == [0]) for r in riders]
        n_units = sum(r.n_units for r, is_active in zip(riders, active) if is_active)
        units, emitted = rider_units(active), 0
        for s, kind in enumerate(pattern):
            cols = slice(s * MXU_DIM, (s + 1) * MXU_DIM)
            w = w_ref[:, cols]
            if wcopy_ref is not None:
                w = w.astype(BF16)
                wcopy_ref[:, cols] = w
            acc = jnp.dot(h_ref[...], w, preferred_element_type=F32)
            while emitted * len(pattern) < (s + 1) * n_units:
                next(units)
                emitted += 1
            o_ref[:, cols] = epilogues[kind](s, acc).astype(BF16)
        assert next(units, None) is None, "a rider emitted more units than it declared"

    for pattern, tiles in tiles_by_pattern.items():
        is_this_kind = functools.reduce(jnp.logical_or, [j == tile for tile in tiles])
        pl.when(is_this_kind)(functools.partial(slabs, pattern, tiles))


def _cast_rider(a, host):
    rows = -(-a.shape[0] // host.n_steps)
    rows = -(-rows // BF16_SUBLANES) * BF16_SUBLANES
    assert a.shape[0] % rows == 0
    last = a.shape[0] // rows - 1
    spec = pl.BlockSpec((rows, a.shape[1]), lambda i, j: (jnp.minimum(host.step_of(i, j), last), 0))

    def body(ins, outs):
        outs[0][...] = ins[0][...].astype(BF16)
        yield

    return _Rider((a,), (spec,), (spec,), (jax.ShapeDtypeStruct(a.shape, BF16),), body, 1)


def _inproj(x2d, g_pre, w_in, cos, sin, *, tm, tn, make_riders=()):
    m, d = x2d.shape
    n = w_in.shape[1]
    pos_tiles = cos.shape[0] // tm
    emit_weights = w_in.dtype != BF16
    assert not emit_weights or m == tm, "each weight tile must be visited once to be copied out"
    n_col_tiles = n // tn
    host = _HostSteps(
        lambda i, j: i * n_col_tiles + j, (m // tm) * n_col_tiles,
        lambda i, j: i * (n_col_tiles - 1) + jnp.maximum(j - 1, 0), (m // tm) * (n_col_tiles - 1))
    riders = tuple(make(host) for make in make_riders)
    in_specs = [
        pl.BlockSpec((tm, d), lambda i, j: (i, 0)),
        pl.BlockSpec((1, d), lambda i, j: (0, 0)),
        pl.BlockSpec((d, tn), lambda i, j: (0, j)),
        pl.BlockSpec((tm, HEAD_DIM), lambda i, j: (i % pos_tiles, 0)),
        pl.BlockSpec((tm, HEAD_DIM), lambda i, j: (i % pos_tiles, 0)),
    ]
    out_specs = [pl.BlockSpec((tm, tn), lambda i, j: (i, j))]
    out_shape = [jax.ShapeDtypeStruct((m, n), BF16)]
    if emit_weights:
        out_specs.append(pl.BlockSpec((d, tn), lambda i, j: (0, j)))
        out_shape.append(jax.ShapeDtypeStruct((d, n), BF16))
    for r in riders:
        in_specs += r.in_specs
        out_specs += r.out_specs
        out_shape += r.out_shapes
    return pl.pallas_call(
        functools.partial(_inproj_kernel, slab=min(tm, 128), tn=tn, emit_weights=emit_weights,
                          riders=riders),
        grid=(m // tm, n_col_tiles),
        in_specs=in_specs,
        out_specs=out_specs,
        out_shape=out_shape,
        scratch_shapes=[pltpu.VMEM((tm, d), BF16)],
        compiler_params=_params("arbitrary", "arbitrary"),
        name="inproj",
    )(x2d, g_pre, w_in, cos, sin, *[a for r in riders for a in r.arrays])


def _group_norm(o):
    mu = jnp.mean(o, axis=-1, keepdims=True)
    d = o - mu
    var = jnp.mean(d * d, axis=-1, keepdims=True)
    return d * lax.rsqrt(var + EPS)


def _retention_tile(q_ref, k_ref, v_ref, z_ref, o_ref, s_ref):
    c = RET_CHUNK
    row = lax.broadcasted_iota(jnp.int32, (c, c), 0).astype(F32)
    col = lax.broadcasted_iota(jnp.int32, (c, c), 1).astype(F32)
    diff = row - col
    for h in range(RET_HEADS):
        lg = _LOG_GAMMA[h]
        mask = jnp.where(diff >= 0, jnp.exp(jnp.maximum(diff, 0.0) * lg), 0.0)
        q_decay = jnp.exp((row + 1.0) * lg)
        k_decay = jnp.exp((c - 1.0 - row) * lg)
        chunk_decay = math.exp(c * lg)
        cols = slice(h * HEAD_DIM, (h + 1) * HEAD_DIM)
        for ci in range(q_ref.shape[0] // c):
            rows = slice(ci * c, (ci + 1) * c)
            q = q_ref[rows, cols]
            k = k_ref[rows, cols]
            v = v_ref[rows, cols]
            s0 = s_ref[0, h]
            scores = lax.dot_general(q, k, (((1,), (1,)), ((), ())),
                                     preferred_element_type=F32) * mask
            inner = jnp.dot(scores.astype(BF16), v, preferred_element_type=F32)
            cross = jnp.dot(q, s0.astype(BF16), preferred_element_type=F32) * q_decay
            kd = (k.astype(F32) * k_decay).astype(BF16)
            s_ref[0, h] = chunk_decay * s0 + lax.dot_general(
                kd, v, (((0,), (0,)), ((), ())), preferred_element_type=F32)
            o = _group_norm(inner + cross)
            o_ref[rows, cols] = (o * jax.nn.silu(z_ref[rows, cols].astype(F32))).astype(BF16)
            yield


def _retention_step_rider(proj, state, host):
    groups, bb, _ = proj.shape
    assert groups <= host.n_later_steps
    group = lambda i, j: jnp.minimum(host.later_step_of(i, j), groups - 1)
    col_spec = lambda cb: pl.BlockSpec((1, bb, RET_WIDTH), lambda i, j: (group(i, j), 0, cb))
    state_spec = pl.BlockSpec((bb, RET_HEADS, HEAD_DIM, HEAD_DIM), lambda i, j: (group(i, j), 0, 0, 0))

    def body(ins, outs):
        q_ref, k_ref, v_ref, z_ref, s_ref = ins
        o_ref, sn_ref = outs
        pad = jnp.zeros((HEAD_DIM - bb, HEAD_DIM), F32)
        for h in range(RET_HEADS):
            gamma = math.exp(_LOG_GAMMA[h])
            cols = slice(h * HEAD_DIM, (h + 1) * HEAD_DIM)
            q, k, v = q_ref[0, :, cols], k_ref[0, :, cols], v_ref[0, :, cols]
            qt = jnp.concatenate([q, pad], axis=0).T
            kt = jnp.concatenate([k, pad], axis=0).T
            o = []
            for b in range(bb):
                s_new = gamma * s_ref[b, h] + kt[:, b:b + 1] * v[b:b + 1, :]
                sn_ref[b, h] = s_new
                o.append(jnp.sum(qt[:, b:b + 1] * s_new, axis=0, keepdims=True))
            o_ref[0, :, cols] = _group_norm(jnp.concatenate(o, axis=0)) * jax.nn.silu(z_ref[0, :, cols])
            yield

    return _Rider(
        (proj, proj, proj, proj, state),
        (col_spec(0), col_spec(1), col_spec(2), col_spec(3), state_spec),
        (pl.BlockSpec((1, bb, RET_WIDTH), lambda i, j: (group(i, j), 0, 0)), state_spec),
        (jax.ShapeDtypeStruct((groups, bb, RET_WIDTH), F32), jax.ShapeDtypeStruct(state.shape, F32)),
        body, RET_HEADS, skips_first_tile=True)


def _s5_output_gate(y, z, wglu_ref, bglu_ref):
    y = jax.nn.gelu(y)
    g = jnp.dot(y.astype(BF16), wglu_ref[...], preferred_element_type=F32) + bglu_ref[...]
    return y * jax.nn.sigmoid(g) * jax.nn.silu(z.astype(F32))


def _s5_tile(u_ref, z_ref, wb_ref, cw_ref, ar_ref, ai_ref, d_ref, wglu_ref, bglu_ref,
             o_ref, sr_ref, si_ref, y_scr, x_scr, *, sub, filler=(), n_filler=0):
    n_sub = len(x_scr) // 2
    xr_scr, xi_scr = x_scr[:n_sub], x_scr[n_sub:]
    strided = lambda j: pl.ds(j, sub, stride=S5_PITCH)
    filler = iter(filler)
    progress = [0, 0]
    n_slots = n_sub * S5_PAIRS

    def fill():
        progress[0] += 1
        while progress[1] < n_filler and progress[1] * n_slots < progress[0] * n_filler:
            next(filler)
            progress[1] += 1

    for k in range(n_sub):
        rows = slice(k * sub, (k + 1) * sub)
        for blk in range(S5_BLOCKS):
            ub = u_ref[rows, blk * LANES:(blk + 1) * LANES]
            for j in range(blk * PAIRS_PER_BLOCK, (blk + 1) * PAIRS_PER_BLOCK):
                bu = jnp.dot(ub, wb_ref[j], preferred_element_type=F32)
                xr_scr[k][strided(j), :] = bu[:, :LANES]
                xi_scr[k][strided(j), :] = bu[:, LANES:]
                fill()
    ar = ar_ref[...]
    ai = ai_ref[...]
    xr, xi = sr_ref[0], si_ref[0]
    for k in range(n_sub):
        rows = slice(k * sub, (k + 1) * sub)
        for t in range(sub):
            step = slice(t * S5_PITCH, t * S5_PITCH + S5_PAIRS)
            xr, xi = (ar * xr - ai * xi + xr_scr[k][step, :],
                      ar * xi + ai * xr + xi_scr[k][step, :])
            xr_scr[k][step, :] = xr
            xi_scr[k][step, :] = xi
        for blk in range(S5_BLOCKS):
            cols = slice(blk * LANES, (blk + 1) * LANES)
            acc = d_ref[:, cols] * u_ref[rows, cols].astype(F32)
            for j in range(blk * PAIRS_PER_BLOCK, (blk + 1) * PAIRS_PER_BLOCK):
                x = jnp.concatenate([xr_scr[k][strided(j), :], xi_scr[k][strided(j), :]], axis=1)
                acc = acc + lax.dot_general(x.astype(BF16), cw_ref[j], (((1,), (1,)), ((), ())),
                                            preferred_element_type=F32)
            y_scr[rows, cols] = acc
        o_ref[rows, :] = _s5_output_gate(
            y_scr[rows, :], z_ref[rows, :], wglu_ref, bglu_ref).astype(BF16)
    sr_ref[0] = xr
    si_ref[0] = xi


def _mixers_kernel(q_ref, k_ref, v_ref, za_ref, u_ref, zb_ref, wb_ref, cw_ref, ar_ref, ai_ref, d_ref,
                   wglu_ref, bglu_ref, ya_ref, ret_ref, yb_ref, sr_ref, si_ref, y_scr, *x_scr, sub):
    @pl.when(pl.program_id(1) == 0)
    def _():
        ret_ref[...] = jnp.zeros_like(ret_ref)
        sr_ref[...] = jnp.zeros_like(sr_ref)
        si_ref[...] = jnp.zeros_like(si_ref)

    retention_units = RET_HEADS * (q_ref.shape[0] // RET_CHUNK)
    _s5_tile(u_ref, zb_ref, wb_ref, cw_ref, ar_ref, ai_ref, d_ref, wglu_ref, bglu_ref,
             yb_ref, sr_ref, si_ref, y_scr, x_scr, sub=sub,
             filler=_retention_tile(q_ref, k_ref, v_ref, za_ref, ya_ref, ret_ref),
             n_filler=retention_units)


def _mixers_prompt(proj, wb, cw, ar, ai, d, wglu, bglu, *, batch, seq, tm, sub):
    nt = seq // tm
    row_block = lambda b, i: b * nt + i
    col_spec = lambda cb: pl.BlockSpec((tm, RET_WIDTH), lambda b, i: (row_block(b, i), cb))
    s5_state_spec = pl.BlockSpec((1, S5_PAIRS, LANES), lambda b, i: (b, 0, 0))
    assert RET_WIDTH == S5_WIDTH
    return pl.pallas_call(
        functools.partial(_mixers_kernel, sub=sub),
        grid=(batch, nt),
        in_specs=[
            col_spec(0), col_spec(1), col_spec(2), col_spec(3), col_spec(4), col_spec(5),
            _resident(wb.shape), _resident(cw.shape), _resident(ar.shape), _resident(ai.shape),
            _resident(d.shape), _resident(wglu.shape), _resident(bglu.shape),
        ],
        out_specs=[
            pl.BlockSpec((tm, RET_WIDTH), lambda b, i: (row_block(b, i), 0)),
            pl.BlockSpec((1, RET_HEADS, HEAD_DIM, HEAD_DIM), lambda b, i: (b, 0, 0, 0)),
            pl.BlockSpec((tm, S5_WIDTH), lambda b, i: (row_block(b, i), 0)),
            s5_state_spec, s5_state_spec,
        ],
        out_shape=[
            jax.ShapeDtypeStruct((batch * seq, RET_WIDTH), BF16),
            jax.ShapeDtypeStruct((batch, RET_HEADS, HEAD_DIM, HEAD_DIM), F32),
            jax.ShapeDtypeStruct((batch * seq, S5_WIDTH), BF16),
            jax.ShapeDtypeStruct((batch, S5_PAIRS, LANES), F32),
            jax.ShapeDtypeStruct((batch, S5_PAIRS, LANES), F32),
        ],
        scratch_shapes=[pltpu.VMEM((tm, S5_WIDTH), F32)]
        + [pltpu.VMEM((sub * S5_PITCH, LANES), F32)] * (2 * (tm // sub)),
        compiler_params=_params("arbitrary", "arbitrary"),
        name="mixers_prompt",
    )(proj, proj, proj, proj, proj, proj, wb, cw, ar, ai, d, wglu, bglu)


def _s5_step_kernel(u_ref, z_ref, x0r_ref, x0i_ref, wb_ref, cw_ref, ar_ref, ai_ref, d_ref,
                    wglu_ref, bglu_ref, o_ref, sr_ref, si_ref, y_scr):
    for blk in range(S5_BLOCKS):
        cols = slice(blk * LANES, (blk + 1) * LANES)
        ub = u_ref[:, cols]
        acc = d_ref[:, cols] * ub.astype(F32)
        for j in range(blk * PAIRS_PER_BLOCK, (blk + 1) * PAIRS_PER_BLOCK):
            pc = slice(j * LANES, (j + 1) * LANES)
            bu = jnp.dot(ub, wb_ref[j], preferred_element_type=F32)
            ar = ar_ref[j:j + 1, :]
            ai = ai_ref[j:j + 1, :]
            x0r = x0r_ref[:, pc]
            x0i = x0i_ref[:, pc]
            nr = ar * x0r - ai * x0i + bu[:, :LANES]
            ni = ar * x0i + ai * x0r + bu[:, LANES:]
            sr_ref[:, pc] = nr
            si_ref[:, pc] = ni
            x = jnp.concatenate([nr, ni], axis=1).astype(BF16)
            acc = acc + lax.dot_general(x, cw_ref[j], (((1,), (1,)), ((), ())),
                                        preferred_element_type=F32)
        y_scr[:, cols] = acc
    o_ref[...] = _s5_output_gate(y_scr[...], z_ref[...], wglu_ref, bglu_ref).astype(BF16)


def _s5_sample(proj, x0r, x0i, wb, cw, ar, ai, d, wglu, bglu):
    nb = proj.shape[0]
    ub_col = (4 * RET_WIDTH) // S5_WIDTH
    full = lambda a: pl.BlockSpec(a.shape, lambda i: (0,) * a.ndim)
    return pl.pallas_call(
        _s5_step_kernel,
        grid=(1,),
        in_specs=[
            pl.BlockSpec((nb, S5_WIDTH), lambda i: (0, ub_col)),
            pl.BlockSpec((nb, S5_WIDTH), lambda i: (0, ub_col + 1)),
            full(x0r), full(x0i), full(wb), full(cw), full(ar), full(ai), full(d), full(wglu), full(bglu),
        ],
        out_specs=[
            pl.BlockSpec((nb, S5_WIDTH), lambda i: (0, 0)),
            full(x0r), full(x0i),
        ],
        out_shape=[
            jax.ShapeDtypeStruct((nb, S5_WIDTH), BF16),
            jax.ShapeDtypeStruct(x0r.shape, F32),
            jax.ShapeDtypeStruct(x0i.shape, F32),
        ],
        scratch_shapes=[pltpu.VMEM((nb, S5_WIDTH), F32)],
        compiler_params=_params("arbitrary"),
        name="s5_sample",
    )(proj, proj, x0r, x0i, wb, cw, ar, ai, d, wglu, bglu)


def _merge_kernel(ya_ref, yb_ref, ga_ref, gb_ref, x_ref, wpa_ref, wpb_ref, wout_ref, gpost_ref, o_ref,
                  merged_scr):
    slabs = [slice(s * MXU_DIM, (s + 1) * MXU_DIM) for s in range(D_MODEL // MXU_DIM)]
    ya_in = ya_ref[...].astype(BF16)
    yb_in = yb_ref[...].astype(BF16)
    for cols in slabs:
        ya = jnp.dot(ya_in, wpa_ref[:, cols], preferred_element_type=F32)
        yb = jnp.dot(yb_in, wpb_ref[:, cols], preferred_element_type=F32)
        merged_scr[:, cols] = (jax.nn.sigmoid(ga_ref[:, cols].astype(F32)) * ya
                               + jax.nn.sigmoid(gb_ref[:, cols].astype(F32)) * yb).astype(BF16)
    sq = jnp.zeros((o_ref.shape[0], 1), F32)
    for cols in slabs:
        out = jnp.dot(merged_scr[...], wout_ref[:, cols], preferred_element_type=F32)
        sq = sq + jnp.sum(out * out, axis=-1, keepdims=True)
        o_ref[:, cols] = out
    inv_rms = lax.rsqrt(sq * (1.0 / D_MODEL) + EPS)
    for cols in slabs:
        o_ref[:, cols] = x_ref[:, cols] + o_ref[:, cols] * inv_rms * gpost_ref[:, cols]


def _merge(ya, yb, proj, x2d, wpa, wpb, wout, gpost, *, tm):
    m = x2d.shape[0]
    ga_col = (4 * RET_WIDTH + 2 * S5_WIDTH) // D_MODEL
    return pl.pallas_call(
        _merge_kernel,
        grid=(m // tm,),
        in_specs=[
            pl.BlockSpec((tm, RET_WIDTH), lambda i: (i, 0)),
            pl.BlockSpec((tm, S5_WIDTH), lambda i: (i, 0)),
            pl.BlockSpec((tm, D_MODEL), lambda i: (i, ga_col)),
            pl.BlockSpec((tm, D_MODEL), lambda i: (i, ga_col + 1)),
            pl.BlockSpec((tm, D_MODEL), lambda i: (i, 0)),
            _resident(wpa.shape), _resident(wpb.shape), _resident(wout.shape), _resident(gpost.shape),
        ],
        out_specs=pl.BlockSpec((tm, D_MODEL), lambda i: (i, 0)),
        out_shape=jax.ShapeDtypeStruct((m, D_MODEL), F32),
        scratch_shapes=[pltpu.VMEM((tm, D_MODEL), BF16)],
        compiler_params=_params("arbitrary"),
        name="merge_out",
    )(ya, yb, proj, proj, x2d, wpa, wpb, wout, gpost)


def _rope_tables(pos):
    half = HEAD_DIM // 2
    inv = ROPE_BASE ** (-jnp.arange(half, dtype=F32) / half)
    ang = pos.astype(F32)[:, None] * inv[None, :]
    cos, sin = jnp.cos(ang), jnp.sin(ang)
    return jnp.concatenate([cos, cos], axis=-1), jnp.concatenate([-sin, sin], axis=-1)


def _s5_discretize(lam_re, lam_im, log_dt, b_re, b_im):
    dt = jnp.exp(log_dt)[:, None]
    mag = jnp.exp(lam_re * dt)
    abar_r = mag * jnp.cos(lam_im * dt)
    abar_i = mag * jnp.sin(lam_im * dt)
    nr, ni = abar_r - 1.0, abar_i
    den = lam_re * lam_re + lam_im * lam_im
    coef_r = (nr * lam_re + ni * lam_im) / den
    coef_i = (ni * lam_re - nr * lam_im) / den
    bbar_r = coef_r[:, :, None] * b_re - coef_i[:, :, None] * b_im
    bbar_i = coef_r[:, :, None] * b_im + coef_i[:, :, None] * b_re
    return abar_r, abar_i, bbar_r, bbar_i


def _s5_pair_weights(bbar_r, bbar_i, c_re, c_im):
    j = lax.broadcasted_iota(jnp.int32, (S5_PAIRS, LANES, 2 * LANES), 0)
    row = lax.broadcasted_iota(jnp.int32, (S5_PAIRS, LANES, 2 * LANES), 1)
    col = lax.broadcasted_iota(jnp.int32, (S5_PAIRS, LANES, 2 * LANES), 2)
    pair_lanes = 2 * S5_GROUP
    keep = ((row // pair_lanes == j % PAIRS_PER_BLOCK)
            & ((row // S5_GROUP) % 2 == (col // S5_P) % 2))

    def expand(re, im, perm):
        base = jnp.stack([re, im]).reshape((2, S5_PAIRS, 2) + re.shape[1:]).transpose(perm)
        base = base.reshape(S5_PAIRS, 1, S5_GROUP, 2 * LANES)
        tiled = jnp.broadcast_to(base, (S5_PAIRS, LANES // S5_GROUP, S5_GROUP, 2 * LANES))
        return jnp.where(keep, tiled.reshape(S5_PAIRS, LANES, 2 * LANES), 0.0).astype(BF16)

    wb = expand(bbar_r, bbar_i, (1, 4, 0, 2, 3))
    cw = expand(c_re, -c_im, (1, 3, 0, 2, 4))
    return wb, cw


def kernel(x_prompt, x_sample, state_ret, state_s5_re, state_s5_im, g_pre, w_in, w_pa, w_pb, w_out, g_post,
           s5_lam_re, s5_lam_im, s5_log_dt, s5_b_re, s5_b_im, s5_c_re, s5_c_im, s5_d, s5_w_glu, s5_b_glu):
    assert w_in.shape[0] == 1, "single trunk layer"
    bp, lp, _ = x_prompt.shape
    bs, ls, _ = x_sample.shape
    assert ls == 1 and lp % RET_CHUNK == 0

    b_glu = s5_b_glu[0].reshape(1, S5_WIDTH)
    gpre = g_pre[0].reshape(1, D_MODEL)
    gpost = g_post[0].reshape(1, D_MODEL)

    abar_r, abar_i, bbar_r, bbar_i = _s5_discretize(
        s5_lam_re[0], s5_lam_im[0], s5_log_dt[0], s5_b_re[0], s5_b_im[0])
    wb, cw = _s5_pair_weights(bbar_r, bbar_i, s5_c_re[0], s5_c_im[0])
    ar = abar_r.reshape(S5_PAIRS, LANES)
    ai = abar_i.reshape(S5_PAIRS, LANES)
    d_skip = s5_d[0].reshape(1, S5_WIDTH)

    xs = x_sample.reshape(bs, D_MODEL)
    cos_s, sin_s = _rope_tables(jnp.full((bs,), PAST_LEN, jnp.int32))
    proj_s, w_in_b = _inproj(xs, gpre, w_in[0], cos_s, sin_s, tm=bs, tn=INPROJ_COLS_SAMPLE)

    xp = x_prompt.reshape(bp * lp, D_MODEL)
    cos_p, sin_p = _rope_tables(jnp.arange(lp, dtype=jnp.int32))
    proj_s_grouped = proj_s[:, :4 * RET_WIDTH].astype(F32).reshape(
        bs // RET_SAMPLE_SEQS, RET_SAMPLE_SEQS, 4 * RET_WIDTH)
    proj_p, w_pa_b, w_pb_b, w_out_b, w_glu_b, ya_s, ret_s = _inproj(
        xp, gpre, w_in_b, cos_p, sin_p, tm=INPROJ_ROWS, tn=INPROJ_COLS_PROMPT,
        make_riders=[functools.partial(_cast_rider, w) for w in (w_pa[0], w_pb[0], w_out[0], s5_w_glu[0])]
        + [functools.partial(_retention_step_rider, proj_s_grouped, state_ret[0])])
    ya_s = ya_s.reshape(bs, RET_WIDTH)
    ya_p, ret_p, yb_p, s5r_p, s5i_p = _mixers_prompt(
        proj_p, wb, cw, ar, ai, d_skip, w_glu_b, b_glu, batch=bp, seq=lp, tm=MIXER_ROWS,
        sub=S5_SUBTILE_ROWS)
    y_p = _merge(ya_p, yb_p, proj_p, xp, w_pa_b, w_pb_b, w_out_b, gpost, tm=MERGE_ROWS)

    x0r = state_s5_re[0].reshape(bs, S5_GROUPS * S5_P)
    x0i = state_s5_im[0].reshape(bs, S5_GROUPS * S5_P)
    yb_s, s5r_s, s5i_s = _s5_sample(proj_s, x0r, x0i, wb, cw, ar, ai, d_skip, w_glu_b, b_glu)
    y_s = _merge(ya_s, yb_s, proj_s, xs, w_pa_b, w_pb_b, w_out_b, gpost, tm=bs)

    state_shape = (1, -1, S5_GROUPS, S5_P)
    return (y_p.reshape(bp, lp, D_MODEL), y_s.reshape(bs, 1, D_MODEL),
            ret_p[None], s5r_p.reshape(state_shape), s5i_p.reshape(state_shape),
            ret_s[None], s5r_s.reshape(state_shape), s5i_s.reshape(state_shape))
```

```python
import functools
import math
from typing import Callable, NamedTuple

import jax
import jax.numpy as jnp
import numpy as np
from jax import lax
from jax.experimental import pallas as pl
from jax.experimental.pallas import tpu as pltpu

F32 = jnp.float32
BF16 = jnp.bfloat16

D_MODEL = 2048
RET_WIDTH = 1024
RET_HEADS = 8
HEAD_DIM = RET_WIDTH // RET_HEADS
RET_CHUNK = 128
ROPE_BASE = 10000.0
S5_WIDTH = 1024
S5_GROUP = 16
S5_GROUPS = S5_WIDTH // S5_GROUP
S5_P = 64
PAST_LEN = 16384
EPS = 1e-6
IN_COLS = 4 * RET_WIDTH + 2 * S5_WIDTH + 2 * D_MODEL

LANES = 128
MXU_DIM = 256
BF16_SUBLANES = 16
S5_PAIRS = S5_GROUPS // 2
PAIRS_PER_BLOCK = LANES // (2 * S5_GROUP)
S5_BLOCKS = S5_WIDTH // LANES
S5_PITCH = 36
VMEM_LIMIT = 60 * 1024 * 1024

INPROJ_ROWS = 1024
INPROJ_COLS_PROMPT = 2048
INPROJ_COLS_SAMPLE = 1024
MIXER_ROWS = 512
S5_SUBTILE_ROWS = 128
MERGE_ROWS = 512
RET_SAMPLE_SEQS = 4

_LOG_GAMMA = [float(np.log(np.float32(1.0) - np.float32(2.0) ** np.float32(-5.0 - h)))
              for h in range(RET_HEADS)]


def _params(*sem):
    return pltpu.CompilerParams(dimension_semantics=sem, vmem_limit_bytes=VMEM_LIMIT)


def _resident(shape):
    nd = len(shape)
    return pl.BlockSpec(shape, lambda *_: (0,) * nd, pipeline_mode=pl.Buffered(1))


def _rope(x, cos, sin_signed):
    return x * cos + pltpu.roll(x, HEAD_DIM // 2, 1) * sin_signed


class _Rider(NamedTuple):
    arrays: tuple
    in_specs: tuple
    out_specs: tuple
    out_shapes: tuple
    body: Callable
    n_units: int
    skips_first_tile: bool = False


class _HostSteps(NamedTuple):
    step_of: Callable
    n_steps: int
    later_step_of: Callable
    n_later_steps: int


def _inproj_kernel(x_ref, g_ref, w_ref, cos_ref, sin_ref, *rest, slab, tn, emit_weights, riders):
    n_in = sum(len(r.in_specs) for r in riders)
    n_out = sum(len(r.out_specs) for r in riders)
    rider_in, rest = rest[:n_in], rest[n_in:]
    o_ref, h_ref = rest[0], rest[-1]
    wcopy_ref = rest[1] if emit_weights else None
    rider_out = rest[len(rest) - 1 - n_out:-1]
    j = pl.program_id(1)

    def rider_units(active):
        i = o = 0
        for r, is_active in zip(riders, active):
            if is_active:
                yield from r.body(rider_in[i:i + len(r.in_specs)], rider_out[o:o + len(r.out_specs)])
            i, o = i + len(r.in_specs), o + len(r.out_specs)

    @pl.when(j == 0)
    def _():
        g = g_ref[...]

        def body(r, _):
            rows = pl.ds(pl.multiple_of(r * slab, slab), slab)
            x = x_ref[rows, :]
            ms = jnp.mean(x * x, axis=-1, keepdims=True)
            h_ref[rows, :] = (x * lax.rsqrt(ms + EPS) * g).astype(BF16)
            return 0

        lax.fori_loop(0, x_ref.shape[0] // slab, body, 0)

    def rotate(s, acc):
        is_k = j * tn + s * MXU_DIM >= RET_WIDTH
        scale = jnp.where(is_k, HEAD_DIM ** -0.5, 1.0).astype(F32)
        cos = cos_ref[...] * scale
        sin = sin_ref[...] * scale
        return jnp.concatenate([_rope(acc[:, hh * HEAD_DIM:(hh + 1) * HEAD_DIM], cos, sin)
                                for hh in range(MXU_DIM // HEAD_DIM)], axis=1)

    epilogues = {"rope": rotate, "plain": lambda s, acc: acc}
    col_kinds = (("rope", 2 * RET_WIDTH), ("plain", IN_COLS - 2 * RET_WIDTH))
    slab_kinds = [kind for kind, width in col_kinds for _ in range(width // MXU_DIM)]
    per_tile = tn // MXU_DIM
    tiles_by_pattern = {}
    for tile in range(len(slab_kinds) // per_tile):
        pattern = tuple(slab_kinds[tile * per_tile:(tile + 1) * per_tile])
        tiles_by_pattern.setdefault(pattern, []).append(tile)

    def slabs(pattern, tiles):
        some_skip = any(r.skips_first_tile for r in riders)
        assert not (some_skip and 0 in tiles and len(tiles) > 1), "tile 0 needs its own branch"
        active = [not (r.skips_first_tile and tiles == [0]) for r in riders]
        n_units = sum(r.n_units for r, is_active in zip(riders, active) if is_active)
        units, emitted = rider_units(active), 0
        for s, kind in enumerate(pattern):
            cols = slice(s * MXU_DIM, (s + 1) * MXU_DIM)
            w = w_ref[:, cols]
            if wcopy_ref is not None:
                w = w.astype(BF16)
                wcopy_ref[:, cols] = w
            acc = jnp.dot(h_ref[...], w, preferred_element_type=F32)
            while emitted * len(pattern) < (s + 1) * n_units:
                next(units)
                emitted += 1
            o_ref[:, cols] = epilogues[kind](s, acc).astype(BF16)
        assert next(units, None) is None, "a rider emitted more units than it declared"

    for pattern, tiles in tiles_by_pattern.items():
        is_this_kind = functools.reduce(jnp.logical_or, [j == tile for tile in tiles])
        pl.when(is_this_kind)(functools.partial(slabs, pattern, tiles))


def _cast_rider(a, host):
    rows = -(-a.shape[0] // host.n_later_steps)
    rows = -(-rows // BF16_SUBLANES) * BF16_SUBLANES
    assert a.shape[0] % rows == 0
    last = a.shape[0] // rows - 1
    spec = pl.BlockSpec((rows, a.shape[1]),
                        lambda i, j: (jnp.minimum(host.later_step_of(i, j), last), 0))

    def body(ins, outs):
        outs[0][...] = ins[0][...].astype(BF16)
        yield

    return _Rider((a,), (spec,), (spec,), (jax.ShapeDtypeStruct(a.shape, BF16),), body, 1,
                  skips_first_tile=True)


def _inproj(x2d, g_pre, w_in, cos, sin, *, tm, tn, make_riders=()):
    m, d = x2d.shape
    n = w_in.shape[1]
    pos_tiles = cos.shape[0] // tm
    emit_weights = w_in.dtype != BF16
    assert not emit_weights or m == tm, "each weight tile must be visited once to be copied out"
    n_col_tiles = n // tn
    host = _HostSteps(
        lambda i, j: i * n_col_tiles + j, (m // tm) * n_col_tiles,
        lambda i, j: i * (n_col_tiles - 1) + jnp.maximum(j - 1, 0), (m // tm) * (n_col_tiles - 1))
    riders = tuple(make(host) for make in make_riders)
    in_specs = [
        pl.BlockSpec((tm, d), lambda i, j: (i, 0)),
        pl.BlockSpec((1, d), lambda i, j: (0, 0)),
        pl.BlockSpec((d, tn), lambda i, j: (0, j)),
        pl.BlockSpec((tm, HEAD_DIM), lambda i, j: (i % pos_tiles, 0)),
        pl.BlockSpec((tm, HEAD_DIM), lambda i, j: (i % pos_tiles, 0)),
    ]
    out_specs = [pl.BlockSpec((tm, tn), lambda i, j: (i, j))]
    out_shape = [jax.ShapeDtypeStruct((m, n), BF16)]
    if emit_weights:
        out_specs.append(pl.BlockSpec((d, tn), lambda i, j: (0, j)))
        out_shape.append(jax.ShapeDtypeStruct((d, n), BF16))
    for r in riders:
        in_specs += r.in_specs
        out_specs += r.out_specs
        out_shape += r.out_shapes
    return pl.pallas_call(
        functools.partial(_inproj_kernel, slab=min(tm, 128), tn=tn, emit_weights=emit_weights,
                          riders=riders),
        grid=(m // tm, n_col_tiles),
        in_specs=in_specs,
        out_specs=out_specs,
        out_shape=out_shape,
        scratch_shapes=[pltpu.VMEM((tm, d), BF16)],
        compiler_params=_params("arbitrary", "arbitrary"),
        name="inproj",
    )(x2d, g_pre, w_in, cos, sin, *[a for r in riders for a in r.arrays])


def _group_norm(o):
    mu = jnp.mean(o, axis=-1, keepdims=True)
    d = o - mu
    var = jnp.mean(d * d, axis=-1, keepdims=True)
    return d * lax.rsqrt(var + EPS)


def _retention_tile(q_ref, k_ref, v_ref, z_ref, o_ref, s_ref):
    c = RET_CHUNK
    row = lax.broadcasted_iota(jnp.int32, (c, c), 0).astype(F32)
    col = lax.broadcasted_iota(jnp.int32, (c, c), 1).astype(F32)
    diff = row - col
    for h in range(RET_HEADS):
        lg = _LOG_GAMMA[h]
        mask = jnp.where(diff >= 0, jnp.exp(jnp.maximum(diff, 0.0) * lg), 0.0)
        q_decay = jnp.exp((row + 1.0) * lg)
        k_decay = jnp.exp((c - 1.0 - row) * lg)
        chunk_decay = math.exp(c * lg)
        cols = slice(h * HEAD_DIM, (h + 1) * HEAD_DIM)
        for ci in range(q_ref.shape[0] // c):
            rows = slice(ci * c, (ci + 1) * c)
            q = q_ref[rows, cols]
            k = k_ref[rows, cols]
            v = v_ref[rows, cols]
            s0 = s_ref[0, h]
            scores = lax.dot_general(q, k, (((1,), (1,)), ((), ())),
                                     preferred_element_type=F32) * mask
            inner = jnp.dot(scores.astype(BF16), v, preferred_element_type=F32)
            cross = jnp.dot(q, s0.astype(BF16), preferred_element_type=F32) * q_decay
            kd = (k.astype(F32) * k_decay).astype(BF16)
            s_ref[0, h] = chunk_decay * s0 + lax.dot_general(
                kd, v, (((0,), (0,)), ((), ())), preferred_element_type=F32)
            o = _group_norm(inner + cross)
            o_ref[rows, cols] = (o * jax.nn.silu(z_ref[rows, cols].astype(F32))).astype(BF16)
            yield


def _retention_step_rider(proj, state, host):
    groups, bb, _ = proj.shape
    assert groups <= host.n_later_steps
    group = lambda i, j: jnp.minimum(host.later_step_of(i, j), groups - 1)
    col_spec = lambda cb: pl.BlockSpec((1, bb, RET_WIDTH), lambda i, j: (group(i, j), 0, cb))
    state_spec = pl.BlockSpec((bb, RET_HEADS, HEAD_DIM, HEAD_DIM), lambda i, j: (group(i, j), 0, 0, 0))

    def body(ins, outs):
        q_ref, k_ref, v_ref, z_ref, s_ref = ins
        o_ref, sn_ref = outs
        pad = jnp.zeros((HEAD_DIM - bb, HEAD_DIM), F32)
        for h in range(RET_HEADS):
            gamma = math.exp(_LOG_GAMMA[h])
            cols = slice(h * HEAD_DIM, (h + 1) * HEAD_DIM)
            q, k, v = q_ref[0, :, cols], k_ref[0, :, cols], v_ref[0, :, cols]
            qt = jnp.concatenate([q, pad], axis=0).T
            kt = jnp.concatenate([k, pad], axis=0).T
            o = []
            for b in range(bb):
                s_new = gamma * s_ref[b, h] + kt[:, b:b + 1] * v[b:b + 1, :]
                sn_ref[b, h] = s_new
                o.append(jnp.sum(qt[:, b:b + 1] * s_new, axis=0, keepdims=True))
            o_ref[0, :, cols] = _group_norm(jnp.concatenate(o, axis=0)) * jax.nn.silu(z_ref[0, :, cols])
            yield

    return _Rider(
        (proj, proj, proj, proj, state),
        (col_spec(0), col_spec(1), col_spec(2), col_spec(3), state_spec),
        (pl.BlockSpec((1, bb, RET_WIDTH), lambda i, j: (group(i, j), 0, 0)), state_spec),
        (jax.ShapeDtypeStruct((groups, bb, RET_WIDTH), F32), jax.ShapeDtypeStruct(state.shape, F32)),
        body, RET_HEADS, skips_first_tile=True)


def _s5_output_gate(y, z, wglu_ref, bglu_ref):
    y = jax.nn.gelu(y)
    g = jnp.dot(y.astype(BF16), wglu_ref[...], preferred_element_type=F32) + bglu_ref[...]
    return y * jax.nn.sigmoid(g) * jax.nn.silu(z.astype(F32))


def _s5_tile(u_ref, z_ref, wb_ref, cw_ref, ar_ref, ai_ref, d_ref, wglu_ref, bglu_ref,
             o_ref, sr_ref, si_ref, y_scr, x_scr, *, sub, filler=(), n_filler=0):
    n_sub = len(x_scr) // 2
    xr_scr, xi_scr = x_scr[:n_sub], x_scr[n_sub:]
    strided = lambda j: pl.ds(j, sub, stride=S5_PITCH)
    filler = iter(filler)
    progress = [0, 0]
    n_slots = n_sub * S5_PAIRS

    def fill():
        progress[0] += 1
        while progress[1] < n_filler and progress[1] * n_slots < progress[0] * n_filler:
            next(filler)
            progress[1] += 1

    for k in range(n_sub):
        rows = slice(k * sub, (k + 1) * sub)
        for blk in range(S5_BLOCKS):
            ub = u_ref[rows, blk * LANES:(blk + 1) * LANES]
            for j in range(blk * PAIRS_PER_BLOCK, (blk + 1) * PAIRS_PER_BLOCK):
                bu = jnp.dot(ub, wb_ref[j], preferred_element_type=F32)
                xr_scr[k][strided(j), :] = bu[:, :LANES]
                xi_scr[k][strided(j), :] = bu[:, LANES:]
                fill()
    ar = ar_ref[...]
    ai = ai_ref[...]
    xr, xi = sr_ref[0], si_ref[0]
    for k in range(n_sub):
        rows = slice(k * sub, (k + 1) * sub)
        for t in range(sub):
            step = slice(t * S5_PITCH, t * S5_PITCH + S5_PAIRS)
            xr, xi = (ar * xr - ai * xi + xr_scr[k][step, :],
                      ar * xi + ai * xr + xi_scr[k][step, :])
            xr_scr[k][step, :] = xr
            xi_scr[k][step, :] = xi
        for blk in range(S5_BLOCKS):
            cols = slice(blk * LANES, (blk + 1) * LANES)
            acc = d_ref[:, cols] * u_ref[rows, cols].astype(F32)
            for j in range(blk * PAIRS_PER_BLOCK, (blk + 1) * PAIRS_PER_BLOCK):
                x = jnp.concatenate([xr_scr[k][strided(j), :], xi_scr[k][strided(j), :]], axis=1)
                acc = acc + lax.dot_general(x.astype(BF16), cw_ref[j], (((1,), (1,)), ((), ())),
                                            preferred_element_type=F32)
            y_scr[rows, cols] = acc
        o_ref[rows, :] = _s5_output_gate(
            y_scr[rows, :], z_ref[rows, :], wglu_ref, bglu_ref).astype(BF16)
    sr_ref[0] = xr
    si_ref[0] = xi


def _mixers_kernel(q_ref, k_ref, v_ref, za_ref, u_ref, zb_ref, wb_ref, cw_ref, ar_ref, ai_ref, d_ref,
                   wglu_ref, bglu_ref, ya_ref, ret_ref, yb_ref, sr_ref, si_ref, y_scr, *x_scr, sub):
    @pl.when(pl.program_id(1) == 0)
    def _():
        ret_ref[...] = jnp.zeros_like(ret_ref)
        sr_ref[...] = jnp.zeros_like(sr_ref)
        si_ref[...] = jnp.zeros_like(si_ref)

    retention_units = RET_HEADS * (q_ref.shape[0] // RET_CHUNK)
    _s5_tile(u_ref, zb_ref, wb_ref, cw_ref, ar_ref, ai_ref, d_ref, wglu_ref, bglu_ref,
             yb_ref, sr_ref, si_ref, y_scr, x_scr, sub=sub,
             filler=_retention_tile(q_ref, k_ref, v_ref, za_ref, ya_ref, ret_ref),
             n_filler=retention_units)


def _mixers_prompt(proj, wb, cw, ar, ai, d, wglu, bglu, *, batch, seq, tm, sub):
    nt = seq // tm
    row_block = lambda b, i: b * nt + i
    col_spec = lambda cb: pl.BlockSpec((tm, RET_WIDTH), lambda b, i: (row_block(b, i), cb))
    s5_state_spec = pl.BlockSpec((1, S5_PAIRS, LANES), lambda b, i: (b, 0, 0))
    assert RET_WIDTH == S5_WIDTH
    return pl.pallas_call(
        functools.partial(_mixers_kernel, sub=sub),
        grid=(batch, nt),
        in_specs=[
            col_spec(0), col_spec(1), col_spec(2), col_spec(3), col_spec(4), col_spec(5),
            _resident(wb.shape), _resident(cw.shape), _resident(ar.shape), _resident(ai.shape),
            _resident(d.shape), _resident(wglu.shape), _resident(bglu.shape),
        ],
        out_specs=[
            pl.BlockSpec((tm, RET_WIDTH), lambda b, i: (row_block(b, i), 0)),
            pl.BlockSpec((1, RET_HEADS, HEAD_DIM, HEAD_DIM), lambda b, i: (b, 0, 0, 0)),
            pl.BlockSpec((tm, S5_WIDTH), lambda b, i: (row_block(b, i), 0)),
            s5_state_spec, s5_state_spec,
        ],
        out_shape=[
            jax.ShapeDtypeStruct((batch * seq, RET_WIDTH), BF16),
            jax.ShapeDtypeStruct((batch, RET_HEADS, HEAD_DIM, HEAD_DIM), F32),
            jax.ShapeDtypeStruct((batch * seq, S5_WIDTH), BF16),
            jax.ShapeDtypeStruct((batch, S5_PAIRS, LANES), F32),
            jax.ShapeDtypeStruct((batch, S5_PAIRS, LANES), F32),
        ],
        scratch_shapes=[pltpu.VMEM((tm, S5_WIDTH), F32)]
        + [pltpu.VMEM((sub * S5_PITCH, LANES), F32)] * (2 * (tm // sub)),
        compiler_params=_params("arbitrary", "arbitrary"),
        name="mixers_prompt",
    )(proj, proj, proj, proj, proj, proj, wb, cw, ar, ai, d, wglu, bglu)


def _s5_step_kernel(u_ref, z_ref, x0r_ref, x0i_ref, wb_ref, cw_ref, ar_ref, ai_ref, d_ref,
                    wglu_ref, bglu_ref, o_ref, sr_ref, si_ref, y_scr):
    for blk in range(S5_BLOCKS):
        cols = slice(blk * LANES, (blk + 1) * LANES)
        ub = u_ref[:, cols]
        acc = d_ref[:, cols] * ub.astype(F32)
        for j in range(blk * PAIRS_PER_BLOCK, (blk + 1) * PAIRS_PER_BLOCK):
            pc = slice(j * LANES, (j + 1) * LANES)
            bu = jnp.dot(ub, wb_ref[j], preferred_element_type=F32)
            ar = ar_ref[j:j + 1, :]
            ai = ai_ref[j:j + 1, :]
            x0r = x0r_ref[:, pc]
            x0i = x0i_ref[:, pc]
            nr = ar * x0r - ai * x0i + bu[:, :LANES]
            ni = ar * x0i + ai * x0r + bu[:, LANES:]
            sr_ref[:, pc] = nr
            si_ref[:, pc] = ni
            x = jnp.concatenate([nr, ni], axis=1).astype(BF16)
            acc = acc + lax.dot_general(x, cw_ref[j], (((1,), (1,)), ((), ())),
                                        preferred_element_type=F32)
        y_scr[:, cols] = acc
    o_ref[...] = _s5_output_gate(y_scr[...], z_ref[...], wglu_ref, bglu_ref).astype(BF16)


def _s5_sample(proj, x0r, x0i, wb, cw, ar, ai, d, wglu, bglu):
    nb = proj.shape[0]
    ub_col = (4 * RET_WIDTH) // S5_WIDTH
    full = lambda a: pl.BlockSpec(a.shape, lambda i: (0,) * a.ndim)
    return pl.pallas_call(
        _s5_step_kernel,
        grid=(1,),
        in_specs=[
            pl.BlockSpec((nb, S5_WIDTH), lambda i: (0, ub_col)),
            pl.BlockSpec((nb, S5_WIDTH), lambda i: (0, ub_col + 1)),
            full(x0r), full(x0i), full(wb), full(cw), full(ar), full(ai), full(d), full(wglu), full(bglu),
        ],
        out_specs=[
            pl.BlockSpec((nb, S5_WIDTH), lambda i: (0, 0)),
            full(x0r), full(x0i),
        ],
        out_shape=[
            jax.ShapeDtypeStruct((nb, S5_WIDTH), BF16),
            jax.ShapeDtypeStruct(x0r.shape, F32),
            jax.ShapeDtypeStruct(x0i.shape, F32),
        ],
        scratch_shapes=[pltpu.VMEM((nb, S5_WIDTH), F32)],
        compiler_params=_params("arbitrary"),
        name="s5_sample",
    )(proj, proj, x0r, x0i, wb, cw, ar, ai, d, wglu, bglu)


def _merge_kernel(ya_ref, yb_ref, ga_ref, gb_ref, x_ref, wpa_ref, wpb_ref, wout_ref, gpost_ref, o_ref,
                  merged_scr):
    slabs = [slice(s * MXU_DIM, (s + 1) * MXU_DIM) for s in range(D_MODEL // MXU_DIM)]
    ya_in = ya_ref[...].astype(BF16)
    yb_in = yb_ref[...].astype(BF16)
    for cols in slabs:
        ya = jnp.dot(ya_in, wpa_ref[:, cols], preferred_element_type=F32)
        yb = jnp.dot(yb_in, wpb_ref[:, cols], preferred_element_type=F32)
        merged_scr[:, cols] = (jax.nn.sigmoid(ga_ref[:, cols].astype(F32)) * ya
                               + jax.nn.sigmoid(gb_ref[:, cols].astype(F32)) * yb).astype(BF16)
    sq = jnp.zeros((o_ref.shape[0], 1), F32)
    for cols in slabs:
        out = jnp.dot(merged_scr[...], wout_ref[:, cols], preferred_element_type=F32)
        sq = sq + jnp.sum(out * out, axis=-1, keepdims=True)
        o_ref[:, cols] = out
    inv_rms = lax.rsqrt(sq * (1.0 / D_MODEL) + EPS)
    for cols in slabs:
        o_ref[:, cols] = x_ref[:, cols] + o_ref[:, cols] * inv_rms * gpost_ref[:, cols]


def _merge(ya, yb, proj, x2d, wpa, wpb, wout, gpost, *, tm):
    m = x2d.shape[0]
    ga_col = (4 * RET_WIDTH + 2 * S5_WIDTH) // D_MODEL
    return pl.pallas_call(
        _merge_kernel,
        grid=(m // tm,),
        in_specs=[
            pl.BlockSpec((tm, RET_WIDTH), lambda i: (i, 0)),
            pl.BlockSpec((tm, S5_WIDTH), lambda i: (i, 0)),
            pl.BlockSpec((tm, D_MODEL), lambda i: (i, ga_col)),
            pl.BlockSpec((tm, D_MODEL), lambda i: (i, ga_col + 1)),
            pl.BlockSpec((tm, D_MODEL), lambda i: (i, 0)),
            _resident(wpa.shape), _resident(wpb.shape), _resident(wout.shape), _resident(gpost.shape),
        ],
        out_specs=pl.BlockSpec((tm, D_MODEL), lambda i: (i, 0)),
        out_shape=jax.ShapeDtypeStruct((m, D_MODEL), F32),
        scratch_shapes=[pltpu.VMEM((tm, D_MODEL), BF16)],
        compiler_params=_params("arbitrary"),
        name="merge_out",
    )(ya, yb, proj, proj, x2d, wpa, wpb, wout, gpost)


def _rope_tables(pos):
    half = HEAD_DIM // 2
    inv = ROPE_BASE ** (-jnp.arange(half, dtype=F32) / half)
    ang = pos.astype(F32)[:, None] * inv[None, :]
    cos, sin = jnp.cos(ang), jnp.sin(ang)
    return jnp.concatenate([cos, cos], axis=-1), jnp.concatenate([-sin, sin], axis=-1)


def _s5_discretize(lam_re, lam_im, log_dt, b_re, b_im):
    dt = jnp.exp(log_dt)[:, None]
    mag = jnp.exp(lam_re * dt)
    abar_r = mag * jnp.cos(lam_im * dt)
    abar_i = mag * jnp.sin(lam_im * dt)
    nr, ni = abar_r - 1.0, abar_i
    den = lam_re * lam_re + lam_im * lam_im
    coef_r = (nr * lam_re + ni * lam_im) / den
    coef_i = (ni * lam_re - nr * lam_im) / den
    bbar_r = coef_r[:, :, None] * b_re - coef_i[:, :, None] * b_im
    bbar_i = coef_r[:, :, None] * b_im + coef_i[:, :, None] * b_re
    return abar_r, abar_i, bbar_r, bbar_i


def _s5_pair_weights(bbar_r, bbar_i, c_re, c_im):
    j = lax.broadcasted_iota(jnp.int32, (S5_PAIRS, LANES, 2 * LANES), 0)
    row = lax.broadcasted_iota(jnp.int32, (S5_PAIRS, LANES, 2 * LANES), 1)
    col = lax.broadcasted_iota(jnp.int32, (S5_PAIRS, LANES, 2 * LANES), 2)
    pair_lanes = 2 * S5_GROUP
    keep = ((row // pair_lanes == j % PAIRS_PER_BLOCK)
            & ((row // S5_GROUP) % 2 == (col // S5_P) % 2))

    def expand(re, im, perm):
        base = jnp.stack([re, im]).reshape((2, S5_PAIRS, 2) + re.shape[1:]).transpose(perm)
        base = base.reshape(S5_PAIRS, 1, S5_GROUP, 2 * LANES)
        tiled = jnp.broadcast_to(base, (S5_PAIRS, LANES // S5_GROUP, S5_GROUP, 2 * LANES))
        return jnp.where(keep, tiled.reshape(S5_PAIRS, LANES, 2 * LANES), 0.0).astype(BF16)

    wb = expand(bbar_r, bbar_i, (1, 4, 0, 2, 3))
    cw = expand(c_re, -c_im, (1, 3, 0, 2, 4))
    return wb, cw


def kernel(x_prompt, x_sample, state_ret, state_s5_re, state_s5_im, g_pre, w_in, w_pa, w_pb, w_out, g_post,
           s5_lam_re, s5_lam_im, s5_log_dt, s5_b_re, s5_b_im, s5_c_re, s5_c_im, s5_d, s5_w_glu, s5_b_glu):
    assert w_in.shape[0] == 1, "single trunk layer"
    bp, lp, _ = x_prompt.shape
    bs, ls, _ = x_sample.shape
    assert ls == 1 and lp % RET_CHUNK == 0

    b_glu = s5_b_glu[0].reshape(1, S5_WIDTH)
    gpre = g_pre[0].reshape(1, D_MODEL)
    gpost = g_post[0].reshape(1, D_MODEL)

    abar_r, abar_i, bbar_r, bbar_i = _s5_discretize(
        s5_lam_re[0], s5_lam_im[0], s5_log_dt[0], s5_b_re[0], s5_b_im[0])
    wb, cw = _s5_pair_weights(bbar_r, bbar_i, s5_c_re[0], s5_c_im[0])
    ar = abar_r.reshape(S5_PAIRS, LANES)
    ai = abar_i.reshape(S5_PAIRS, LANES)
    d_skip = s5_d[0].reshape(1, S5_WIDTH)

    xs = x_sample.reshape(bs, D_MODEL)
    cos_s, sin_s = _rope_tables(jnp.full((bs,), PAST_LEN, jnp.int32))
    proj_s, w_in_b = _inproj(xs, gpre, w_in[0], cos_s, sin_s, tm=bs, tn=INPROJ_COLS_SAMPLE)

    xp = x_prompt.reshape(bp * lp, D_MODEL)
    cos_p, sin_p = _rope_tables(jnp.arange(lp, dtype=jnp.int32))
    proj_s_grouped = proj_s[:, :4 * RET_WIDTH].astype(F32).reshape(
        bs // RET_SAMPLE_SEQS, RET_SAMPLE_SEQS, 4 * RET_WIDTH)
    proj_p, w_pa_b, w_pb_b, w_out_b, w_glu_b, ya_s, ret_s = _inproj(
        xp, gpre, w_in_b, cos_p, sin_p, tm=INPROJ_ROWS, tn=INPROJ_COLS_PROMPT,
        make_riders=[functools.partial(_cast_rider, w) for w in (w_pa[0], w_pb[0], w_out[0], s5_w_glu[0])]
        + [functools.partial(_retention_step_rider, proj_s_grouped, state_ret[0])])
    ya_s = ya_s.reshape(bs, RET_WIDTH)
    ya_p, ret_p, yb_p, s5r_p, s5i_p = _mixers_prompt(
        proj_p, wb, cw, ar, ai, d_skip, w_glu_b, b_glu, batch=bp, seq=lp, tm=MIXER_ROWS,
        sub=S5_SUBTILE_ROWS)
    y_p = _merge(ya_p, yb_p, proj_p, xp, w_pa_b, w_pb_b, w_out_b, gpost, tm=MERGE_ROWS)

    x0r = state_s5_re[0].reshape(bs, S5_GROUPS * S5_P)
    x0i = state_s5_im[0].reshape(bs, S5_GROUPS * S5_P)
    yb_s, s5r_s, s5i_s = _s5_sample(proj_s, x0r, x0i, wb, cw, ar, ai, d_skip, w_glu_b, b_glu)
    y_s = _merge(ya_s, yb_s, proj_s, xs, w_pa_b, w_pb_b, w_out_b, gpost, tm=bs)

    state_shape = (1, -1, S5_GROUPS, S5_P)
    return (y_p.reshape(bp, lp, D_MODEL), y_s.reshape(bs, 1, D_MODEL),
            ret_p[None], s5r_p.reshape(state_shape), s5i_p.reshape(state_shape),
            ret_s[None], s5r_s.reshape(state_shape), s5i_s.reshape(state_shape))
```

```python
import functools
import math
from typing import Callable, NamedTuple

import jax
import jax.numpy as jnp
import numpy as np
from jax import lax
from jax.experimental import pallas as pl
from jax.experimental.pallas import tpu as pltpu

F32 = jnp.float32
BF16 = jnp.bfloat16

D_MODEL = 2048
RET_WIDTH = 1024
RET_HEADS = 8
HEAD_DIM = RET_WIDTH // RET_HEADS
RET_CHUNK = 128
ROPE_BASE = 10000.0
S5_WIDTH = 1024
S5_GROUP = 16
S5_GROUPS = S5_WIDTH // S5_GROUP
S5_P = 64
PAST_LEN = 16384
EPS = 1e-6
IN_COLS = 4 * RET_WIDTH + 2 * S5_WIDTH + 2 * D_MODEL

LANES = 128
MXU_DIM = 256
BF16_SUBLANES = 16
S5_PAIRS = S5_GROUPS // 2
PAIRS_PER_BLOCK = LANES // (2 * S5_GROUP)
S5_BLOCKS = S5_WIDTH // LANES
S5_PITCH = 36
VMEM_LIMIT = 60 * 1024 * 1024

INPROJ_ROWS = 1024
INPROJ_COLS_PROMPT = 2048
INPROJ_COLS_SAMPLE = 1024
MIXER_ROWS = 512
S5_SUBTILE_ROWS = 128
MERGE_ROWS = 512
RET_SAMPLE_SEQS = 4

_LOG_GAMMA = [float(np.log(np.float32(1.0) - np.float32(2.0) ** np.float32(-5.0 - h)))
              for h in range(RET_HEADS)]


def _params(*sem):
    return pltpu.CompilerParams(dimension_semantics=sem, vmem_limit_bytes=VMEM_LIMIT)


def _resident(shape):
    nd = len(shape)
    return pl.BlockSpec(shape, lambda *_: (0,) * nd, pipeline_mode=pl.Buffered(1))


def _rope(x, cos, sin_signed):
    return x * cos + pltpu.roll(x, HEAD_DIM // 2, 1) * sin_signed


class _Rider(NamedTuple):
    arrays: tuple
    in_specs: tuple
    out_specs: tuple
    out_shapes: tuple
    body: Callable
    n_units: int


def _inproj_kernel(x_ref, g_ref, w_ref, cos_ref, sin_ref, *rest, slab, tn, emit_weights, riders):
    n_in = sum(len(r.in_specs) for r in riders)
    n_out = sum(len(r.out_specs) for r in riders)
    rider_in, rest = rest[:n_in], rest[n_in:]
    o_ref, h_ref = rest[0], rest[-1]
    wcopy_ref = rest[1] if emit_weights else None
    rider_out = rest[len(rest) - 1 - n_out:-1]
    j = pl.program_id(1)

    def rider_units():
        i = o = 0
        for r in riders:
            yield from r.body(rider_in[i:i + len(r.in_specs)], rider_out[o:o + len(r.out_specs)])
            i, o = i + len(r.in_specs), o + len(r.out_specs)

    @pl.when(j == 0)
    def _():
        g = g_ref[...]

        def body(r, _):
            rows = pl.ds(pl.multiple_of(r * slab, slab), slab)
            x = x_ref[rows, :]
            ms = jnp.mean(x * x, axis=-1, keepdims=True)
            h_ref[rows, :] = (x * lax.rsqrt(ms + EPS) * g).astype(BF16)
            return 0

        lax.fori_loop(0, x_ref.shape[0] // slab, body, 0)

    def rotate(s, acc):
        is_k = j * tn + s * MXU_DIM >= RET_WIDTH
        scale = jnp.where(is_k, HEAD_DIM ** -0.5, 1.0).astype(F32)
        cos = cos_ref[...] * scale
        sin = sin_ref[...] * scale
        return jnp.concatenate([_rope(acc[:, hh * HEAD_DIM:(hh + 1) * HEAD_DIM], cos, sin)
                                for hh in range(MXU_DIM // HEAD_DIM)], axis=1)

    epilogues = {"rope": rotate, "plain": lambda s, acc: acc}
    col_kinds = (("rope", 2 * RET_WIDTH), ("plain", IN_COLS - 2 * RET_WIDTH))
    slab_kinds = [kind for kind, width in col_kinds for _ in range(width // MXU_DIM)]
    per_tile = tn // MXU_DIM
    tiles_by_pattern = {}
    for tile in range(len(slab_kinds) // per_tile):
        pattern = tuple(slab_kinds[tile * per_tile:(tile + 1) * per_tile])
        tiles_by_pattern.setdefault(pattern, []).append(tile)

    def slabs(pattern, tiles):
        assert not (riders and 0 in tiles and len(tiles) > 1), "tile 0 needs its own branch"
        riders_active = tiles != [0]
        n_units = sum(r.n_units for r in riders) if riders_active else 0
        units, emitted = rider_units() if riders_active else iter(()), 0
        for s, kind in enumerate(pattern):
            cols = slice(s * MXU_DIM, (s + 1) * MXU_DIM)
            w = w_ref[:, cols]
            if wcopy_ref is not None:
                w = w.astype(BF16)
                wcopy_ref[:, cols] = w
            acc = jnp.dot(h_ref[...], w, preferred_element_type=F32)
            while emitted * len(pattern) < (s + 1) * n_units:
                next(units)
                emitted += 1
            o_ref[:, cols] = epilogues[kind](s, acc).astype(BF16)
        assert next(units, None) is None, "a rider emitted more units than it declared"

    for pattern, tiles in tiles_by_pattern.items():
        is_this_kind = functools.reduce(jnp.logical_or, [j == tile for tile in tiles])
        pl.when(is_this_kind)(functools.partial(slabs, pattern, tiles))


def _cast_rider(a, step_of, n_steps):
    rows = -(-a.shape[0] // n_steps)
    rows = -(-rows // BF16_SUBLANES) * BF16_SUBLANES
    assert a.shape[0] % rows == 0
    last = a.shape[0] // rows - 1
    spec = pl.BlockSpec((rows, a.shape[1]), lambda i, j: (jnp.minimum(step_of(i, j), last), 0))

    def body(ins, outs):
        outs[0][...] = ins[0][...].astype(BF16)
        yield

    return _Rider((a,), (spec,), (spec,), (jax.ShapeDtypeStruct(a.shape, BF16),), body, 1)


def _inproj(x2d, g_pre, w_in, cos, sin, *, tm, tn, make_riders=()):
    m, d = x2d.shape
    n = w_in.shape[1]
    pos_tiles = cos.shape[0] // tm
    emit_weights = w_in.dtype != BF16
    assert not emit_weights or m == tm, "each weight tile must be visited once to be copied out"
    n_col_tiles = n // tn
    rider_step = lambda i, j: i * (n_col_tiles - 1) + jnp.maximum(j - 1, 0)
    riders = tuple(make(rider_step, (m // tm) * (n_col_tiles - 1)) for make in make_riders)
    in_specs = [
        pl.BlockSpec((tm, d), lambda i, j: (i, 0)),
        pl.BlockSpec((1, d), lambda i, j: (0, 0)),
        pl.BlockSpec((d, tn), lambda i, j: (0, j)),
        pl.BlockSpec((tm, HEAD_DIM), lambda i, j: (i % pos_tiles, 0)),
        pl.BlockSpec((tm, HEAD_DIM), lambda i, j: (i % pos_tiles, 0)),
    ]
    out_specs = [pl.BlockSpec((tm, tn), lambda i, j: (i, j))]
    out_shape = [jax.ShapeDtypeStruct((m, n), BF16)]
    if emit_weights:
        out_specs.append(pl.BlockSpec((d, tn), lambda i, j: (0, j)))
        out_shape.append(jax.ShapeDtypeStruct((d, n), BF16))
    for r in riders:
        in_specs += r.in_specs
        out_specs += r.out_specs
        out_shape += r.out_shapes
    return pl.pallas_call(
        functools.partial(_inproj_kernel, slab=min(tm, 128), tn=tn, emit_weights=emit_weights,
                          riders=riders),
        grid=(m // tm, n_col_tiles),
        in_specs=in_specs,
        out_specs=out_specs,
        out_shape=out_shape,
        scratch_shapes=[pltpu.VMEM((tm, d), BF16)],
        compiler_params=_params("arbitrary", "arbitrary"),
        name="inproj",
    )(x2d, g_pre, w_in, cos, sin, *[a for r in riders for a in r.arrays])


def _group_norm(o):
    mu = jnp.mean(o, axis=-1, keepdims=True)
    d = o - mu
    var = jnp.mean(d * d, axis=-1, keepdims=True)
    return d * lax.rsqrt(var + EPS)


def _retention_tile(q_ref, k_ref, v_ref, z_ref, o_ref, s_ref):
    c = RET_CHUNK
    row = lax.broadcasted_iota(jnp.int32, (c, c), 0).astype(F32)
    col = lax.broadcasted_iota(jnp.int32, (c, c), 1).astype(F32)
    diff = row - col
    for h in range(RET_HEADS):
        lg = _LOG_GAMMA[h]
        mask = jnp.where(diff >= 0, jnp.exp(jnp.maximum(diff, 0.0) * lg), 0.0)
        q_decay = jnp.exp((row + 1.0) * lg)
        k_decay = jnp.exp((c - 1.0 - row) * lg)
        chunk_decay = math.exp(c * lg)
        cols = slice(h * HEAD_DIM, (h + 1) * HEAD_DIM)
        for ci in range(q_ref.shape[0] // c):
            rows = slice(ci * c, (ci + 1) * c)
            q = q_ref[rows, cols]
            k = k_ref[rows, cols]
            v = v_ref[rows, cols]
            s0 = s_ref[0, h]
            scores = lax.dot_general(q, k, (((1,), (1,)), ((), ())),
                                     preferred_element_type=F32) * mask
            inner = jnp.dot(scores.astype(BF16), v, preferred_element_type=F32)
            cross = jnp.dot(q, s0.astype(BF16), preferred_element_type=F32) * q_decay
            kd = (k.astype(F32) * k_decay).astype(BF16)
            s_ref[0, h] = chunk_decay * s0 + lax.dot_general(
                kd, v, (((0,), (0,)), ((), ())), preferred_element_type=F32)
            o = _group_norm(inner + cross)
            o_ref[rows, cols] = (o * jax.nn.silu(z_ref[rows, cols].astype(F32))).astype(BF16)
            yield


def _retention_step_rider(proj, state, step_of, n_steps):
    groups, bb, _ = proj.shape
    assert groups <= n_steps
    group = lambda i, j: jnp.minimum(step_of(i, j), groups - 1)
    col_spec = lambda cb: pl.BlockSpec((1, bb, RET_WIDTH), lambda i, j: (group(i, j), 0, cb))
    state_spec = pl.BlockSpec((bb, RET_HEADS, HEAD_DIM, HEAD_DIM), lambda i, j: (group(i, j), 0, 0, 0))

    def body(ins, outs):
        q_ref, k_ref, v_ref, z_ref, s_ref = ins
        o_ref, sn_ref = outs
        pad = jnp.zeros((HEAD_DIM - bb, HEAD_DIM), F32)
        for h in range(RET_HEADS):
            gamma = math.exp(_LOG_GAMMA[h])
            cols = slice(h * HEAD_DIM, (h + 1) * HEAD_DIM)
            q, k, v = q_ref[0, :, cols], k_ref[0, :, cols], v_ref[0, :, cols]
            qt = jnp.concatenate([q, pad], axis=0).T
            kt = jnp.concatenate([k, pad], axis=0).T
            o = []
            for b in range(bb):
                s_new = gamma * s_ref[b, h] + kt[:, b:b + 1] * v[b:b + 1, :]
                sn_ref[b, h] = s_new
                o.append(jnp.sum(qt[:, b:b + 1] * s_new, axis=0, keepdims=True))
            o_ref[0, :, cols] = _group_norm(jnp.concatenate(o, axis=0)) * jax.nn.silu(z_ref[0, :, cols])
            yield

    return _Rider(
        (proj, proj, proj, proj, state),
        (col_spec(0), col_spec(1), col_spec(2), col_spec(3), state_spec),
        (pl.BlockSpec((1, bb, RET_WIDTH), lambda i, j: (group(i, j), 0, 0)), state_spec),
        (jax.ShapeDtypeStruct((groups, bb, RET_WIDTH), F32), jax.ShapeDtypeStruct(state.shape, F32)),
        body, RET_HEADS)


def _s5_output_gate(y, z, wglu_ref, bglu_ref):
    y = jax.nn.gelu(y)
    g = jnp.dot(y.astype(BF16), wglu_ref[...], preferred_element_type=F32) + bglu_ref[...]
    return y * jax.nn.sigmoid(g) * jax.nn.silu(z.astype(F32))


def _s5_tile(u_ref, z_ref, wb_ref, cw_ref, ar_ref, ai_ref, d_ref, wglu_ref, bglu_ref,
             o_ref, sr_ref, si_ref, y_scr, x_scr, *, sub, filler=(), n_filler=0):
    n_sub = len(x_scr) // 2
    xr_scr, xi_scr = x_scr[:n_sub], x_scr[n_sub:]
    strided = lambda j: pl.ds(j, sub, stride=S5_PITCH)
    filler = iter(filler)
    progress = [0, 0]
    n_slots = n_sub * S5_PAIRS

    def fill():
        progress[0] += 1
        while progress[1] < n_filler and progress[1] * n_slots < progress[0] * n_filler:
            next(filler)
            progress[1] += 1

    for k in range(n_sub):
        rows = slice(k * sub, (k + 1) * sub)
        for blk in range(S5_BLOCKS):
            ub = u_ref[rows, blk * LANES:(blk + 1) * LANES]
            for j in range(blk * PAIRS_PER_BLOCK, (blk + 1) * PAIRS_PER_BLOCK):
                bu = jnp.dot(ub, wb_ref[j], preferred_element_type=F32)
                xr_scr[k][strided(j), :] = bu[:, :LANES]
                xi_scr[k][strided(j), :] = bu[:, LANES:]
                fill()
    ar = ar_ref[...]
    ai = ai_ref[...]
    xr, xi = sr_ref[0], si_ref[0]
    for k in range(n_sub):
        rows = slice(k * sub, (k + 1) * sub)
        for t in range(sub):
            step = slice(t * S5_PITCH, t * S5_PITCH + S5_PAIRS)
            xr, xi = (ar * xr - ai * xi + xr_scr[k][step, :],
                      ar * xi + ai * xr + xi_scr[k][step, :])
            xr_scr[k][step, :] = xr
            xi_scr[k][step, :] = xi
        for blk in range(S5_BLOCKS):
            cols = slice(blk * LANES, (blk + 1) * LANES)
            acc = d_ref[:, cols] * u_ref[rows, cols].astype(F32)
            for j in range(blk * PAIRS_PER_BLOCK, (blk + 1) * PAIRS_PER_BLOCK):
                x = jnp.concatenate([xr_scr[k][strided(j), :], xi_scr[k][strided(j), :]], axis=1)
                acc = acc + lax.dot_general(x.astype(BF16), cw_ref[j], (((1,), (1,)), ((), ())),
                                            preferred_element_type=F32)
            y_scr[rows, cols] = acc
        o_ref[rows, :] = _s5_output_gate(
            y_scr[rows, :], z_ref[rows, :], wglu_ref, bglu_ref).astype(BF16)
    sr_ref[0] = xr
    si_ref[0] = xi


def _mixers_kernel(q_ref, k_ref, v_ref, za_ref, u_ref, zb_ref, wb_ref, cw_ref, ar_ref, ai_ref, d_ref,
                   wglu_ref, bglu_ref, ya_ref, ret_ref, yb_ref, sr_ref, si_ref, y_scr, *x_scr, sub):
    @pl.when(pl.program_id(1) == 0)
    def _():
        ret_ref[...] = jnp.zeros_like(ret_ref)
        sr_ref[...] = jnp.zeros_like(sr_ref)
        si_ref[...] = jnp.zeros_like(si_ref)

    retention_units = RET_HEADS * (q_ref.shape[0] // RET_CHUNK)
    _s5_tile(u_ref, zb_ref, wb_ref, cw_ref, ar_ref, ai_ref, d_ref, wglu_ref, bglu_ref,
             yb_ref, sr_ref, si_ref, y_scr, x_scr, sub=sub,
             filler=_retention_tile(q_ref, k_ref, v_ref, za_ref, ya_ref, ret_ref),
             n_filler=retention_units)


def _mixers_prompt(proj, wb, cw, ar, ai, d, wglu, bglu, *, batch, seq, tm, sub):
    nt = seq // tm
    row_block = lambda b, i: b * nt + i
    col_spec = lambda cb: pl.BlockSpec((tm, RET_WIDTH), lambda b, i: (row_block(b, i), cb))
    s5_state_spec = pl.BlockSpec((1, S5_PAIRS, LANES), lambda b, i: (b, 0, 0))
    assert RET_WIDTH == S5_WIDTH
    return pl.pallas_call(
        functools.partial(_mixers_kernel, sub=sub),
        grid=(batch, nt),
        in_specs=[
            col_spec(0), col_spec(1), col_spec(2), col_spec(3), col_spec(4), col_spec(5),
            _resident(wb.shape), _resident(cw.shape), _resident(ar.shape), _resident(ai.shape),
            _resident(d.shape), _resident(wglu.shape), _resident(bglu.shape),
        ],
        out_specs=[
            pl.BlockSpec((tm, RET_WIDTH), lambda b, i: (row_block(b, i), 0)),
            pl.BlockSpec((1, RET_HEADS, HEAD_DIM, HEAD_DIM), lambda b, i: (b, 0, 0, 0)),
            pl.BlockSpec((tm, S5_WIDTH), lambda b, i: (row_block(b, i), 0)),
            s5_state_spec, s5_state_spec,
        ],
        out_shape=[
            jax.ShapeDtypeStruct((batch * seq, RET_WIDTH), BF16),
            jax.ShapeDtypeStruct((batch, RET_HEADS, HEAD_DIM, HEAD_DIM), F32),
            jax.ShapeDtypeStruct((batch * seq, S5_WIDTH), BF16),
            jax.ShapeDtypeStruct((batch, S5_PAIRS, LANES), F32),
            jax.ShapeDtypeStruct((batch, S5_PAIRS, LANES), F32),
        ],
        scratch_shapes=[pltpu.VMEM((tm, S5_WIDTH), F32)]
        + [pltpu.VMEM((sub * S5_PITCH, LANES), F32)] * (2 * (tm // sub)),
        compiler_params=_params("arbitrary", "arbitrary"),
        name="mixers_prompt",
    )(proj, proj, proj, proj, proj, proj, wb, cw, ar, ai, d, wglu, bglu)


def _s5_step_kernel(u_ref, z_ref, x0r_ref, x0i_ref, wb_ref, cw_ref, ar_ref, ai_ref, d_ref,
                    wglu_ref, bglu_ref, o_ref, sr_ref, si_ref, y_scr):
    for blk in range(S5_BLOCKS):
        cols = slice(blk * LANES, (blk + 1) * LANES)
        ub = u_ref[:, cols]
        acc = d_ref[:, cols] * ub.astype(F32)
        for j in range(blk * PAIRS_PER_BLOCK, (blk + 1) * PAIRS_PER_BLOCK):
            pc = slice(j * LANES, (j + 1) * LANES)
            bu = jnp.dot(ub, wb_ref[j], preferred_element_type=F32)
            ar = ar_ref[j:j + 1, :]
            ai = ai_ref[j:j + 1, :]
            x0r = x0r_ref[:, pc]
            x0i = x0i_ref[:, pc]
            nr = ar * x0r - ai * x0i + bu[:, :LANES]
            ni = ar * x0i + ai * x0r + bu[:, LANES:]
            sr_ref[:, pc] = nr
            si_ref[:, pc] = ni
            x = jnp.concatenate([nr, ni], axis=1).astype(BF16)
            acc = acc + lax.dot_general(x, cw_ref[j], (((1,), (1,)), ((), ())),
                                        preferred_element_type=F32)
        y_scr[:, cols] = acc
    o_ref[...] = _s5_output_gate(y_scr[...], z_ref[...], wglu_ref, bglu_ref).astype(BF16)


def _s5_sample(proj, x0r, x0i, wb, cw, ar, ai, d, wglu, bglu):
    nb = proj.shape[0]
    ub_col = (4 * RET_WIDTH) // S5_WIDTH
    full = lambda a: pl.BlockSpec(a.shape, lambda i: (0,) * a.ndim)
    return pl.pallas_call(
        _s5_step_kernel,
        grid=(1,),
        in_specs=[
            pl.BlockSpec((nb, S5_WIDTH), lambda i: (0, ub_col)),
            pl.BlockSpec((nb, S5_WIDTH), lambda i: (0, ub_col + 1)),
            full(x0r), full(x0i), full(wb), full(cw), full(ar), full(ai), full(d), full(wglu), full(bglu),
        ],
        out_specs=[
            pl.BlockSpec((nb, S5_WIDTH), lambda i: (0, 0)),
            full(x0r), full(x0i),
        ],
        out_shape=[
            jax.ShapeDtypeStruct((nb, S5_WIDTH), BF16),
            jax.ShapeDtypeStruct(x0r.shape, F32),
            jax.ShapeDtypeStruct(x0i.shape, F32),
        ],
        scratch_shapes=[pltpu.VMEM((nb, S5_WIDTH), F32)],
        compiler_params=_params("arbitrary"),
        name="s5_sample",
    )(proj, proj, x0r, x0i, wb, cw, ar, ai, d, wglu, bglu)


def _merge_kernel(ya_ref, yb_ref, ga_ref, gb_ref, x_ref, wpa_ref, wpb_ref, wout_ref, gpost_ref, o_ref,
                  merged_scr):
    slabs = [slice(s * MXU_DIM, (s + 1) * MXU_DIM) for s in range(D_MODEL // MXU_DIM)]
    ya_in = ya_ref[...].astype(BF16)
    yb_in = yb_ref[...].astype(BF16)
    for cols in slabs:
        ya = jnp.dot(ya_in, wpa_ref[:, cols], preferred_element_type=F32)
        yb = jnp.dot(yb_in, wpb_ref[:, cols], preferred_element_type=F32)
        merged_scr[:, cols] = (jax.nn.sigmoid(ga_ref[:, cols].astype(F32)) * ya
                               + jax.nn.sigmoid(gb_ref[:, cols].astype(F32)) * yb).astype(BF16)
    sq = jnp.zeros((o_ref.shape[0], 1), F32)
    for cols in slabs:
        out = jnp.dot(merged_scr[...], wout_ref[:, cols], preferred_element_type=F32)
        sq = sq + jnp.sum(out * out, axis=-1, keepdims=True)
        o_ref[:, cols] = out
    inv_rms = lax.rsqrt(sq * (1.0 / D_MODEL) + EPS)
    for cols in slabs:
        o_ref[:, cols] = x_ref[:, cols] + o_ref[:, cols] * inv_rms * gpost_ref[:, cols]


def _merge(ya, yb, proj, x2d, wpa, wpb, wout, gpost, *, tm):
    m = x2d.shape[0]
    ga_col = (4 * RET_WIDTH + 2 * S5_WIDTH) // D_MODEL
    return pl.pallas_call(
        _merge_kernel,
        grid=(m // tm,),
        in_specs=[
            pl.BlockSpec((tm, RET_WIDTH), lambda i: (i, 0)),
            pl.BlockSpec((tm, S5_WIDTH), lambda i: (i, 0)),
            pl.BlockSpec((tm, D_MODEL), lambda i: (i, ga_col)),
            pl.BlockSpec((tm, D_MODEL), lambda i: (i, ga_col + 1)),
            pl.BlockSpec((tm, D_MODEL), lambda i: (i, 0)),
            _resident(wpa.shape), _resident(wpb.shape), _resident(wout.shape), _resident(gpost.shape),
        ],
        out_specs=pl.BlockSpec((tm, D_MODEL), lambda i: (i, 0)),
        out_shape=jax.ShapeDtypeStruct((m, D_MODEL), F32),
        scratch_shapes=[pltpu.VMEM((tm, D_MODEL), BF16)],
        compiler_params=_params("arbitrary"),
        name="merge_out",
    )(ya, yb, proj, proj, x2d, wpa, wpb, wout, gpost)


def _rope_tables(pos):
    half = HEAD_DIM // 2
    inv = ROPE_BASE ** (-jnp.arange(half, dtype=F32) / half)
    ang = pos.astype(F32)[:, None] * inv[None, :]
    cos, sin = jnp.cos(ang), jnp.sin(ang)
    return jnp.concatenate([cos, cos], axis=-1), jnp.concatenate([-sin, sin], axis=-1)


def _s5_discretize(lam_re, lam_im, log_dt, b_re, b_im):
    dt = jnp.exp(log_dt)[:, None]
    mag = jnp.exp(lam_re * dt)
    abar_r = mag * jnp.cos(lam_im * dt)
    abar_i = mag * jnp.sin(lam_im * dt)
    nr, ni = abar_r - 1.0, abar_i
    den = lam_re * lam_re + lam_im * lam_im
    coef_r = (nr * lam_re + ni * lam_im) / den
    coef_i = (ni * lam_re - nr * lam_im) / den
    bbar_r = coef_r[:, :, None] * b_re - coef_i[:, :, None] * b_im
    bbar_i = coef_r[:, :, None] * b_im + coef_i[:, :, None] * b_re
    return abar_r, abar_i, bbar_r, bbar_i


def _s5_pair_weights(bbar_r, bbar_i, c_re, c_im):
    j = lax.broadcasted_iota(jnp.int32, (S5_PAIRS, LANES, 2 * LANES), 0)
    row = lax.broadcasted_iota(jnp.int32, (S5_PAIRS, LANES, 2 * LANES), 1)
    col = lax.broadcasted_iota(jnp.int32, (S5_PAIRS, LANES, 2 * LANES), 2)
    pair_lanes = 2 * S5_GROUP
    keep = ((row // pair_lanes == j % PAIRS_PER_BLOCK)
            & ((row // S5_GROUP) % 2 == (col // S5_P) % 2))

    def expand(re, im, perm):
        base = jnp.stack([re, im]).reshape((2, S5_PAIRS, 2) + re.shape[1:]).transpose(perm)
        base = base.reshape(S5_PAIRS, 1, S5_GROUP, 2 * LANES)
        tiled = jnp.broadcast_to(base, (S5_PAIRS, LANES // S5_GROUP, S5_GROUP, 2 * LANES))
        return jnp.where(keep, tiled.reshape(S5_PAIRS, LANES, 2 * LANES), 0.0).astype(BF16)

    wb = expand(bbar_r, bbar_i, (1, 4, 0, 2, 3))
    cw = expand(c_re, -c_im, (1, 3, 0, 2, 4))
    return wb, cw


def kernel(x_prompt, x_sample, state_ret, state_s5_re, state_s5_im, g_pre, w_in, w_pa, w_pb, w_out, g_post,
           s5_lam_re, s5_lam_im, s5_log_dt, s5_b_re, s5_b_im, s5_c_re, s5_c_im, s5_d, s5_w_glu, s5_b_glu):
    assert w_in.shape[0] == 1, "single trunk layer"
    bp, lp, _ = x_prompt.shape
    bs, ls, _ = x_sample.shape
    assert ls == 1 and lp % RET_CHUNK == 0

    b_glu = s5_b_glu[0].reshape(1, S5_WIDTH)
    gpre = g_pre[0].reshape(1, D_MODEL)
    gpost = g_post[0].reshape(1, D_MODEL)

    abar_r, abar_i, bbar_r, bbar_i = _s5_discretize(
        s5_lam_re[0], s5_lam_im[0], s5_log_dt[0], s5_b_re[0], s5_b_im[0])
    wb, cw = _s5_pair_weights(bbar_r, bbar_i, s5_c_re[0], s5_c_im[0])
    ar = abar_r.reshape(S5_PAIRS, LANES)
    ai = abar_i.reshape(S5_PAIRS, LANES)
    d_skip = s5_d[0].reshape(1, S5_WIDTH)

    xs = x_sample.reshape(bs, D_MODEL)
    cos_s, sin_s = _rope_tables(jnp.full((bs,), PAST_LEN, jnp.int32))
    proj_s, w_in_b = _inproj(xs, gpre, w_in[0], cos_s, sin_s, tm=bs, tn=INPROJ_COLS_SAMPLE)

    xp = x_prompt.reshape(bp * lp, D_MODEL)
    cos_p, sin_p = _rope_tables(jnp.arange(lp, dtype=jnp.int32))
    proj_s_grouped = proj_s[:, :4 * RET_WIDTH].astype(F32).reshape(
        bs // RET_SAMPLE_SEQS, RET_SAMPLE_SEQS, 4 * RET_WIDTH)
    proj_p, w_pa_b, w_pb_b, w_out_b, w_glu_b, ya_s, ret_s = _inproj(
        xp, gpre, w_in_b, cos_p, sin_p, tm=INPROJ_ROWS, tn=INPROJ_COLS_PROMPT,
        make_riders=[functools.partial(_cast_rider, w) for w in (w_pa[0], w_pb[0], w_out[0], s5_w_glu[0])]
        + [functools.partial(_retention_step_rider, proj_s_grouped, state_ret[0])])
    ya_s = ya_s.reshape(bs, RET_WIDTH)
    ya_p, ret_p, yb_p, s5r_p, s5i_p = _mixers_prompt(
        proj_p, wb, cw, ar, ai, d_skip, w_glu_b, b_glu, batch=bp, seq=lp, tm=MIXER_ROWS,
        sub=S5_SUBTILE_ROWS)
    y_p = _merge(ya_p, yb_p, proj_p, xp, w_pa_b, w_pb_b, w_out_b, gpost, tm=MERGE_ROWS)

    x0r = state_s5_re[0].reshape(bs, S5_GROUPS * S5_P)
    x0i = state_s5_im[0].reshape(bs, S5_GROUPS * S5_P)
    yb_s, s5r_s, s5i_s = _s5_sample(proj_s, x0r, x0i, wb, cw, ar, ai, d_skip, w_glu_b, b_glu)
    y_s = _merge(ya_s, yb_s, proj_s, xs, w_pa_b, w_pb_b, w_out_b, gpost, tm=bs)

    state_shape = (1, -1, S5_GROUPS, S5_P)
    return (y_p.reshape(bp, lp, D_MODEL), y_s.reshape(bs, 1, D_MODEL),
            ret_p[None], s5r_p.reshape(state_shape), s5i_p.reshape(state_shape),
            ret_s[None], s5r_s.reshape(state_shape), s5i_s.reshape(state_shape))
```

```python
import functools
import math
from typing import Callable, NamedTuple

import jax
import jax.numpy as jnp
import numpy as np
from jax import lax
from jax.experimental import pallas as pl
from jax.experimental.pallas import tpu as pltpu

F32 = jnp.float32
BF16 = jnp.bfloat16

D_MODEL = 2048
RET_WIDTH = 1024
RET_HEADS = 8
HEAD_DIM = RET_WIDTH // RET_HEADS
RET_CHUNK = 256
ROPE_BASE = 10000.0
S5_WIDTH = 1024
S5_GROUP = 16
S5_GROUPS = S5_WIDTH // S5_GROUP
S5_P = 64
PAST_LEN = 16384
EPS = 1e-6
IN_COLS = 4 * RET_WIDTH + 2 * S5_WIDTH + 2 * D_MODEL

LANES = 128
MXU_DIM = 256
BF16_SUBLANES = 16
S5_PAIRS = S5_GROUPS // 2
PAIRS_PER_BLOCK = LANES // (2 * S5_GROUP)
S5_BLOCKS = S5_WIDTH // LANES
S5_PITCH = 36
VMEM_LIMIT = 60 * 1024 * 1024

INPROJ_ROWS = 1024
INPROJ_COLS_PROMPT = 2048
INPROJ_COLS_SAMPLE = 1024
MIXER_ROWS = 512
S5_SUBTILE_ROWS = 128
MERGE_ROWS = 512
RET_SAMPLE_SEQS = 4

_LOG_GAMMA = [float(np.log(np.float32(1.0) - np.float32(2.0) ** np.float32(-5.0 - h)))
              for h in range(RET_HEADS)]


def _params(*sem):
    return pltpu.CompilerParams(dimension_semantics=sem, vmem_limit_bytes=VMEM_LIMIT)


def _resident(shape):
    nd = len(shape)
    return pl.BlockSpec(shape, lambda *_: (0,) * nd, pipeline_mode=pl.Buffered(1))


def _rope(x, cos, sin_signed):
    return x * cos + pltpu.roll(x, HEAD_DIM // 2, 1) * sin_signed


class _Rider(NamedTuple):
    arrays: tuple
    in_specs: tuple
    out_specs: tuple
    out_shapes: tuple
    body: Callable
    n_units: int


def _inproj_kernel(x_ref, g_ref, w_ref, cos_ref, sin_ref, *rest, slab, tn, emit_weights, riders):
    n_in = sum(len(r.in_specs) for r in riders)
    n_out = sum(len(r.out_specs) for r in riders)
    rider_in, rest = rest[:n_in], rest[n_in:]
    o_ref, h_ref = rest[0], rest[-1]
    wcopy_ref = rest[1] if emit_weights else None
    rider_out = rest[len(rest) - 1 - n_out:-1]
    j = pl.program_id(1)

    def rider_units():
        i = o = 0
        for r in riders:
            yield from r.body(rider_in[i:i + len(r.in_specs)], rider_out[o:o + len(r.out_specs)])
            i, o = i + len(r.in_specs), o + len(r.out_specs)

    @pl.when(j == 0)
    def _():
        g = g_ref[...]

        def body(r, _):
            rows = pl.ds(pl.multiple_of(r * slab, slab), slab)
            x = x_ref[rows, :]
            ms = jnp.mean(x * x, axis=-1, keepdims=True)
            h_ref[rows, :] = (x * lax.rsqrt(ms + EPS) * g).astype(BF16)
            return 0

        lax.fori_loop(0, x_ref.shape[0] // slab, body, 0)

    def rotate(s, acc):
        is_k = j * tn + s * MXU_DIM >= RET_WIDTH
        scale = jnp.where(is_k, HEAD_DIM ** -0.5, 1.0).astype(F32)
        cos = cos_ref[...] * scale
        sin = sin_ref[...] * scale
        return jnp.concatenate([_rope(acc[:, hh * HEAD_DIM:(hh + 1) * HEAD_DIM], cos, sin)
                                for hh in range(MXU_DIM // HEAD_DIM)], axis=1)

    epilogues = {"rope": rotate, "plain": lambda s, acc: acc}
    col_kinds = (("rope", 2 * RET_WIDTH), ("plain", IN_COLS - 2 * RET_WIDTH))
    slab_kinds = [kind for kind, width in col_kinds for _ in range(width // MXU_DIM)]
    per_tile = tn // MXU_DIM
    tiles_by_pattern = {}
    for tile in range(len(slab_kinds) // per_tile):
        pattern = tuple(slab_kinds[tile * per_tile:(tile + 1) * per_tile])
        tiles_by_pattern.setdefault(pattern, []).append(tile)

    def slabs(pattern, tiles):
        assert not (riders and 0 in tiles and len(tiles) > 1), "tile 0 needs its own branch"
        riders_active = tiles != [0]
        n_units = sum(r.n_units for r in riders) if riders_active else 0
        units, emitted = rider_units() if riders_active else iter(()), 0
        for s, kind in enumerate(pattern):
            cols = slice(s * MXU_DIM, (s + 1) * MXU_DIM)
            w = w_ref[:, cols]
            if wcopy_ref is not None:
                w = w.astype(BF16)
                wcopy_ref[:, cols] = w
            acc = jnp.dot(h_ref[...], w, preferred_element_type=F32)
            while emitted * len(pattern) < (s + 1) * n_units:
                next(units)
                emitted += 1
            o_ref[:, cols] = epilogues[kind](s, acc).astype(BF16)
        assert next(units, None) is None, "a rider emitted more units than it declared"

    for pattern, tiles in tiles_by_pattern.items():
        is_this_kind = functools.reduce(jnp.logical_or, [j == tile for tile in tiles])
        pl.when(is_this_kind)(functools.partial(slabs, pattern, tiles))


def _cast_rider(a, step_of, n_steps):
    rows = -(-a.shape[0] // n_steps)
    rows = -(-rows // BF16_SUBLANES) * BF16_SUBLANES
    assert a.shape[0] % rows == 0
    last = a.shape[0] // rows - 1
    spec = pl.BlockSpec((rows, a.shape[1]), lambda i, j: (jnp.minimum(step_of(i, j), last), 0))

    def body(ins, outs):
        outs[0][...] = ins[0][...].astype(BF16)
        yield

    return _Rider((a,), (spec,), (spec,), (jax.ShapeDtypeStruct(a.shape, BF16),), body, 1)


def _inproj(x2d, g_pre, w_in, cos, sin, *, tm, tn, make_riders=()):
    m, d = x2d.shape
    n = w_in.shape[1]
    pos_tiles = cos.shape[0] // tm
    emit_weights = w_in.dtype != BF16
    assert not emit_weights or m == tm, "each weight tile must be visited once to be copied out"
    n_col_tiles = n // tn
    rider_step = lambda i, j: i * (n_col_tiles - 1) + jnp.maximum(j - 1, 0)
    riders = tuple(make(rider_step, (m // tm) * (n_col_tiles - 1)) for make in make_riders)
    in_specs = [
        pl.BlockSpec((tm, d), lambda i, j: (i, 0)),
        pl.BlockSpec((1, d), lambda i, j: (0, 0)),
        pl.BlockSpec((d, tn), lambda i, j: (0, j)),
        pl.BlockSpec((tm, HEAD_DIM), lambda i, j: (i % pos_tiles, 0)),
        pl.BlockSpec((tm, HEAD_DIM), lambda i, j: (i % pos_tiles, 0)),
    ]
    out_specs = [pl.BlockSpec((tm, tn), lambda i, j: (i, j))]
    out_shape = [jax.ShapeDtypeStruct((m, n), BF16)]
    if emit_weights:
        out_specs.append(pl.BlockSpec((d, tn), lambda i, j: (0, j)))
        out_shape.append(jax.ShapeDtypeStruct((d, n), BF16))
    for r in riders:
        in_specs += r.in_specs
        out_specs += r.out_specs
        out_shape += r.out_shapes
    return pl.pallas_call(
        functools.partial(_inproj_kernel, slab=min(tm, 128), tn=tn, emit_weights=emit_weights,
                          riders=riders),
        grid=(m // tm, n_col_tiles),
        in_specs=in_specs,
        out_specs=out_specs,
        out_shape=out_shape,
        scratch_shapes=[pltpu.VMEM((tm, d), BF16)],
        compiler_params=_params("arbitrary", "arbitrary"),
        name="inproj",
    )(x2d, g_pre, w_in, cos, sin, *[a for r in riders for a in r.arrays])


def _group_norm(o):
    mu = jnp.mean(o, axis=-1, keepdims=True)
    d = o - mu
    var = jnp.mean(d * d, axis=-1, keepdims=True)
    return d * lax.rsqrt(var + EPS)


def _retention_tile(q_ref, k_ref, v_ref, z_ref, o_ref, s_ref):
    c = RET_CHUNK
    diff = (lax.broadcasted_iota(jnp.int32, (c, c), 0)
            - lax.broadcasted_iota(jnp.int32, (c, c), 1)).astype(F32)
    row = lax.broadcasted_iota(jnp.int32, (c, HEAD_DIM), 0).astype(F32)
    for h in range(RET_HEADS):
        lg = _LOG_GAMMA[h]
        mask = jnp.where(diff >= 0, jnp.exp(jnp.maximum(diff, 0.0) * lg), 0.0)
        q_decay = jnp.exp((row + 1.0) * lg)
        k_decay = jnp.exp((c - 1.0 - row) * lg)
        chunk_decay = math.exp(c * lg)
        cols = slice(h * HEAD_DIM, (h + 1) * HEAD_DIM)
        for ci in range(q_ref.shape[0] // c):
            rows = slice(ci * c, (ci + 1) * c)
            q = q_ref[rows, cols]
            k = k_ref[rows, cols]
            v = v_ref[rows, cols]
            s0 = s_ref[0, h]
            scores = lax.dot_general(q, k, (((1,), (1,)), ((), ())),
                                     preferred_element_type=F32) * mask
            inner = jnp.dot(scores.astype(BF16), v, preferred_element_type=F32)
            cross = jnp.dot(q, s0.astype(BF16), preferred_element_type=F32) * q_decay
            kd = (k.astype(F32) * k_decay).astype(BF16)
            s_ref[0, h] = chunk_decay * s0 + lax.dot_general(
                kd, v, (((0,), (0,)), ((), ())), preferred_element_type=F32)
            o = _group_norm(inner + cross)
            o_ref[rows, cols] = (o * jax.nn.silu(z_ref[rows, cols].astype(F32))).astype(BF16)
            yield


def _retention_step_rider(proj, state, step_of, n_steps):
    groups, bb, _ = proj.shape
    assert groups <= n_steps
    group = lambda i, j: jnp.minimum(step_of(i, j), groups - 1)
    col_spec = lambda cb: pl.BlockSpec((1, bb, RET_WIDTH), lambda i, j: (group(i, j), 0, cb))
    state_spec = pl.BlockSpec((bb, RET_HEADS, HEAD_DIM, HEAD_DIM), lambda i, j: (group(i, j), 0, 0, 0))

    def body(ins, outs):
        q_ref, k_ref, v_ref, z_ref, s_ref = ins
        o_ref, sn_ref = outs
        pad = jnp.zeros((HEAD_DIM - bb, HEAD_DIM), F32)
        for h in range(RET_HEADS):
            gamma = math.exp(_LOG_GAMMA[h])
            cols = slice(h * HEAD_DIM, (h + 1) * HEAD_DIM)
            q, k, v = q_ref[0, :, cols], k_ref[0, :, cols], v_ref[0, :, cols]
            qt = jnp.concatenate([q, pad], axis=0).T
            kt = jnp.concatenate([k, pad], axis=0).T
            o = []
            for b in range(bb):
                s_new = gamma * s_ref[b, h] + kt[:, b:b + 1] * v[b:b + 1, :]
                sn_ref[b, h] = s_new
                o.append(jnp.sum(qt[:, b:b + 1] * s_new, axis=0, keepdims=True))
            o_ref[0, :, cols] = _group_norm(jnp.concatenate(o, axis=0)) * jax.nn.silu(z_ref[0, :, cols])
            yield

    return _Rider(
        (proj, proj, proj, proj, state),
        (col_spec(0), col_spec(1), col_spec(2), col_spec(3), state_spec),
        (pl.BlockSpec((1, bb, RET_WIDTH), lambda i, j: (group(i, j), 0, 0)), state_spec),
        (jax.ShapeDtypeStruct((groups, bb, RET_WIDTH), F32), jax.ShapeDtypeStruct(state.shape, F32)),
        body, RET_HEADS)


def _s5_output_gate(y, z, wglu_ref, bglu_ref):
    y = jax.nn.gelu(y)
    g = jnp.dot(y.astype(BF16), wglu_ref[...], preferred_element_type=F32) + bglu_ref[...]
    return y * jax.nn.sigmoid(g) * jax.nn.silu(z.astype(F32))


def _s5_tile(u_ref, z_ref, wb_ref, cw_ref, ar_ref, ai_ref, d_ref, wglu_ref, bglu_ref,
             o_ref, sr_ref, si_ref, y_scr, x_scr, *, sub, filler=(), n_filler=0):
    n_sub = len(x_scr) // 2
    xr_scr, xi_scr = x_scr[:n_sub], x_scr[n_sub:]
    strided = lambda j: pl.ds(j, sub, stride=S5_PITCH)
    filler = iter(filler)
    progress = [0, 0]
    n_slots = n_sub * S5_PAIRS

    def fill():
        progress[0] += 1
        while progress[1] < n_filler and progress[1] * n_slots < progress[0] * n_filler:
            next(filler)
            progress[1] += 1

    for k in range(n_sub):
        rows = slice(k * sub, (k + 1) * sub)
        for blk in range(S5_BLOCKS):
            ub = u_ref[rows, blk * LANES:(blk + 1) * LANES]
            for j in range(blk * PAIRS_PER_BLOCK, (blk + 1) * PAIRS_PER_BLOCK):
                bu = jnp.dot(ub, wb_ref[j], preferred_element_type=F32)
                xr_scr[k][strided(j), :] = bu[:, :LANES]
                xi_scr[k][strided(j), :] = bu[:, LANES:]
                fill()
    ar = ar_ref[...]
    ai = ai_ref[...]
    xr, xi = sr_ref[0], si_ref[0]
    for k in range(n_sub):
        rows = slice(k * sub, (k + 1) * sub)
        for t in range(sub):
            step = slice(t * S5_PITCH, t * S5_PITCH + S5_PAIRS)
            xr, xi = (ar * xr - ai * xi + xr_scr[k][step, :],
                      ar * xi + ai * xr + xi_scr[k][step, :])
            xr_scr[k][step, :] = xr
            xi_scr[k][step, :] = xi
        for blk in range(S5_BLOCKS):
            cols = slice(blk * LANES, (blk + 1) * LANES)
            acc = d_ref[:, cols] * u_ref[rows, cols].astype(F32)
            for j in range(blk * PAIRS_PER_BLOCK, (blk + 1) * PAIRS_PER_BLOCK):
                x = jnp.concatenate([xr_scr[k][strided(j), :], xi_scr[k][strided(j), :]], axis=1)
                acc = acc + lax.dot_general(x.astype(BF16), cw_ref[j], (((1,), (1,)), ((), ())),
                                            preferred_element_type=F32)
            y_scr[rows, cols] = acc
        o_ref[rows, :] = _s5_output_gate(
            y_scr[rows, :], z_ref[rows, :], wglu_ref, bglu_ref).astype(BF16)
    sr_ref[0] = xr
    si_ref[0] = xi


def _mixers_kernel(q_ref, k_ref, v_ref, za_ref, u_ref, zb_ref, wb_ref, cw_ref, ar_ref, ai_ref, d_ref,
                   wglu_ref, bglu_ref, ya_ref, ret_ref, yb_ref, sr_ref, si_ref, y_scr, *x_scr, sub):
    @pl.when(pl.program_id(1) == 0)
    def _():
        ret_ref[...] = jnp.zeros_like(ret_ref)
        sr_ref[...] = jnp.zeros_like(sr_ref)
        si_ref[...] = jnp.zeros_like(si_ref)

    retention_units = RET_HEADS * (q_ref.shape[0] // RET_CHUNK)
    _s5_tile(u_ref, zb_ref, wb_ref, cw_ref, ar_ref, ai_ref, d_ref, wglu_ref, bglu_ref,
             yb_ref, sr_ref, si_ref, y_scr, x_scr, sub=sub,
             filler=_retention_tile(q_ref, k_ref, v_ref, za_ref, ya_ref, ret_ref),
             n_filler=retention_units)


def _mixers_prompt(proj, wb, cw, ar, ai, d, wglu, bglu, *, batch, seq, tm, sub):
    nt = seq // tm
    row_block = lambda b, i: b * nt + i
    col_spec = lambda cb: pl.BlockSpec((tm, RET_WIDTH), lambda b, i: (row_block(b, i), cb))
    s5_state_spec = pl.BlockSpec((1, S5_PAIRS, LANES), lambda b, i: (b, 0, 0))
    assert RET_WIDTH == S5_WIDTH
    return pl.pallas_call(
        functools.partial(_mixers_kernel, sub=sub),
        grid=(batch, nt),
        in_specs=[
            col_spec(0), col_spec(1), col_spec(2), col_spec(3), col_spec(4), col_spec(5),
            _resident(wb.shape), _resident(cw.shape), _resident(ar.shape), _resident(ai.shape),
            _resident(d.shape), _resident(wglu.shape), _resident(bglu.shape),
        ],
        out_specs=[
            pl.BlockSpec((tm, RET_WIDTH), lambda b, i: (row_block(b, i), 0)),
            pl.BlockSpec((1, RET_HEADS, HEAD_DIM, HEAD_DIM), lambda b, i: (b, 0, 0, 0)),
            pl.BlockSpec((tm, S5_WIDTH), lambda b, i: (row_block(b, i), 0)),
            s5_state_spec, s5_state_spec,
        ],
        out_shape=[
            jax.ShapeDtypeStruct((batch * seq, RET_WIDTH), BF16),
            jax.ShapeDtypeStruct((batch, RET_HEADS, HEAD_DIM, HEAD_DIM), F32),
            jax.ShapeDtypeStruct((batch * seq, S5_WIDTH), BF16),
            jax.ShapeDtypeStruct((batch, S5_PAIRS, LANES), F32),
            jax.ShapeDtypeStruct((batch, S5_PAIRS, LANES), F32),
        ],
        scratch_shapes=[pltpu.VMEM((tm, S5_WIDTH), F32)]
        + [pltpu.VMEM((sub * S5_PITCH, LANES), F32)] * (2 * (tm // sub)),
        compiler_params=_params("arbitrary", "arbitrary"),
        name="mixers_prompt",
    )(proj, proj, proj, proj, proj, proj, wb, cw, ar, ai, d, wglu, bglu)


def _s5_step_kernel(u_ref, z_ref, x0r_ref, x0i_ref, wb_ref, cw_ref, ar_ref, ai_ref, d_ref,
                    wglu_ref, bglu_ref, o_ref, sr_ref, si_ref, y_scr):
    for blk in range(S5_BLOCKS):
        cols = slice(blk * LANES, (blk + 1) * LANES)
        ub = u_ref[:, cols]
        acc = d_ref[:, cols] * ub.astype(F32)
        for j in range(blk * PAIRS_PER_BLOCK, (blk + 1) * PAIRS_PER_BLOCK):
            pc = slice(j * LANES, (j + 1) * LANES)
            bu = jnp.dot(ub, wb_ref[j], preferred_element_type=F32)
            ar = ar_ref[j:j + 1, :]
            ai = ai_ref[j:j + 1, :]
            x0r = x0r_ref[:, pc]
            x0i = x0i_ref[:, pc]
            nr = ar * x0r - ai * x0i + bu[:, :LANES]
            ni = ar * x0i + ai * x0r + bu[:, LANES:]
            sr_ref[:, pc] = nr
            si_ref[:, pc] = ni
            x = jnp.concatenate([nr, ni], axis=1).astype(BF16)
            acc = acc + lax.dot_general(x, cw_ref[j], (((1,), (1,)), ((), ())),
                                        preferred_element_type=F32)
        y_scr[:, cols] = acc
    o_ref[...] = _s5_output_gate(y_scr[...], z_ref[...], wglu_ref, bglu_ref).astype(BF16)


def _s5_sample(proj, x0r, x0i, wb, cw, ar, ai, d, wglu, bglu):
    nb = proj.shape[0]
    ub_col = (4 * RET_WIDTH) // S5_WIDTH
    full = lambda a: pl.BlockSpec(a.shape, lambda i: (0,) * a.ndim)
    return pl.pallas_call(
        _s5_step_kernel,
        grid=(1,),
        in_specs=[
            pl.BlockSpec((nb, S5_WIDTH), lambda i: (0, ub_col)),
            pl.BlockSpec((nb, S5_WIDTH), lambda i: (0, ub_col + 1)),
            full(x0r), full(x0i), full(wb), full(cw), full(ar), full(ai), full(d), full(wglu), full(bglu),
        ],
        out_specs=[
            pl.BlockSpec((nb, S5_WIDTH), lambda i: (0, 0)),
            full(x0r), full(x0i),
        ],
        out_shape=[
            jax.ShapeDtypeStruct((nb, S5_WIDTH), BF16),
            jax.ShapeDtypeStruct(x0r.shape, F32),
            jax.ShapeDtypeStruct(x0i.shape, F32),
        ],
        scratch_shapes=[pltpu.VMEM((nb, S5_WIDTH), F32)],
        compiler_params=_params("arbitrary"),
        name="s5_sample",
    )(proj, proj, x0r, x0i, wb, cw, ar, ai, d, wglu, bglu)


def _merge_kernel(ya_ref, yb_ref, ga_ref, gb_ref, x_ref, wpa_ref, wpb_ref, wout_ref, gpost_ref, o_ref,
                  merged_scr):
    slabs = [slice(s * MXU_DIM, (s + 1) * MXU_DIM) for s in range(D_MODEL // MXU_DIM)]
    ya_in = ya_ref[...].astype(BF16)
    yb_in = yb_ref[...].astype(BF16)
    for cols in slabs:
        ya = jnp.dot(ya_in, wpa_ref[:, cols], preferred_element_type=F32)
        yb = jnp.dot(yb_in, wpb_ref[:, cols], preferred_element_type=F32)
        merged_scr[:, cols] = (jax.nn.sigmoid(ga_ref[:, cols].astype(F32)) * ya
                               + jax.nn.sigmoid(gb_ref[:, cols].astype(F32)) * yb).astype(BF16)
    sq = jnp.zeros((o_ref.shape[0], 1), F32)
    for cols in slabs:
        out = jnp.dot(merged_scr[...], wout_ref[:, cols], preferred_element_type=F32)
        sq = sq + jnp.sum(out * out, axis=-1, keepdims=True)
        o_ref[:, cols] = out
    inv_rms = lax.rsqrt(sq * (1.0 / D_MODEL) + EPS)
    for cols in slabs:
        o_ref[:, cols] = x_ref[:, cols] + o_ref[:, cols] * inv_rms * gpost_ref[:, cols]


def _merge(ya, yb, proj, x2d, wpa, wpb, wout, gpost, *, tm):
    m = x2d.shape[0]
    ga_col = (4 * RET_WIDTH + 2 * S5_WIDTH) // D_MODEL
    return pl.pallas_call(
        _merge_kernel,
        grid=(m // tm,),
        in_specs=[
            pl.BlockSpec((tm, RET_WIDTH), lambda i: (i, 0)),
            pl.BlockSpec((tm, S5_WIDTH), lambda i: (i, 0)),
            pl.BlockSpec((tm, D_MODEL), lambda i: (i, ga_col)),
            pl.BlockSpec((tm, D_MODEL), lambda i: (i, ga_col + 1)),
            pl.BlockSpec((tm, D_MODEL), lambda i: (i, 0)),
            _resident(wpa.shape), _resident(wpb.shape), _resident(wout.shape), _resident(gpost.shape),
        ],
        out_specs=pl.BlockSpec((tm, D_MODEL), lambda i: (i, 0)),
        out_shape=jax.ShapeDtypeStruct((m, D_MODEL), F32),
        scratch_shapes=[pltpu.VMEM((tm, D_MODEL), BF16)],
        compiler_params=_params("arbitrary"),
        name="merge_out",
    )(ya, yb, proj, proj, x2d, wpa, wpb, wout, gpost)


def _rope_tables(pos):
    half = HEAD_DIM // 2
    inv = ROPE_BASE ** (-jnp.arange(half, dtype=F32) / half)
    ang = pos.astype(F32)[:, None] * inv[None, :]
    cos, sin = jnp.cos(ang), jnp.sin(ang)
    return jnp.concatenate([cos, cos], axis=-1), jnp.concatenate([-sin, sin], axis=-1)


def _s5_discretize(lam_re, lam_im, log_dt, b_re, b_im):
    dt = jnp.exp(log_dt)[:, None]
    mag = jnp.exp(lam_re * dt)
    abar_r = mag * jnp.cos(lam_im * dt)
    abar_i = mag * jnp.sin(lam_im * dt)
    nr, ni = abar_r - 1.0, abar_i
    den = lam_re * lam_re + lam_im * lam_im
    coef_r = (nr * lam_re + ni * lam_im) / den
    coef_i = (ni * lam_re - nr * lam_im) / den
    bbar_r = coef_r[:, :, None] * b_re - coef_i[:, :, None] * b_im
    bbar_i = coef_r[:, :, None] * b_im + coef_i[:, :, None] * b_re
    return abar_r, abar_i, bbar_r, bbar_i


def _s5_pair_weights(bbar_r, bbar_i, c_re, c_im):
    j = lax.broadcasted_iota(jnp.int32, (S5_PAIRS, LANES, 2 * LANES), 0)
    row = lax.broadcasted_iota(jnp.int32, (S5_PAIRS, LANES, 2 * LANES), 1)
    col = lax.broadcasted_iota(jnp.int32, (S5_PAIRS, LANES, 2 * LANES), 2)
    pair_lanes = 2 * S5_GROUP
    keep = ((row // pair_lanes == j % PAIRS_PER_BLOCK)
            & ((row // S5_GROUP) % 2 == (col // S5_P) % 2))

    def expand(re, im, perm):
        base = jnp.stack([re, im]).reshape((2, S5_PAIRS, 2) + re.shape[1:]).transpose(perm)
        base = base.reshape(S5_PAIRS, 1, S5_GROUP, 2 * LANES)
        tiled = jnp.broadcast_to(base, (S5_PAIRS, LANES // S5_GROUP, S5_GROUP, 2 * LANES))
        return jnp.where(keep, tiled.reshape(S5_PAIRS, LANES, 2 * LANES), 0.0).astype(BF16)

    wb = expand(bbar_r, bbar_i, (1, 4, 0, 2, 3))
    cw = expand(c_re, -c_im, (1, 3, 0, 2, 4))
    return wb, cw


def kernel(x_prompt, x_sample, state_ret, state_s5_re, state_s5_im, g_pre, w_in, w_pa, w_pb, w_out, g_post,
           s5_lam_re, s5_lam_im, s5_log_dt, s5_b_re, s5_b_im, s5_c_re, s5_c_im, s5_d, s5_w_glu, s5_b_glu):
    assert w_in.shape[0] == 1, "single trunk layer"
    bp, lp, _ = x_prompt.shape
    bs, ls, _ = x_sample.shape
    assert ls == 1 and lp % RET_CHUNK == 0

    b_glu = s5_b_glu[0].reshape(1, S5_WIDTH)
    gpre = g_pre[0].reshape(1, D_MODEL)
    gpost = g_post[0].reshape(1, D_MODEL)

    abar_r, abar_i, bbar_r, bbar_i = _s5_discretize(
        s5_lam_re[0], s5_lam_im[0], s5_log_dt[0], s5_b_re[0], s5_b_im[0])
    wb, cw = _s5_pair_weights(bbar_r, bbar_i, s5_c_re[0], s5_c_im[0])
    ar = abar_r.reshape(S5_PAIRS, LANES)
    ai = abar_i.reshape(S5_PAIRS, LANES)
    d_skip = s5_d[0].reshape(1, S5_WIDTH)

    xs = x_sample.reshape(bs, D_MODEL)
    cos_s, sin_s = _rope_tables(jnp.full((bs,), PAST_LEN, jnp.int32))
    proj_s, w_in_b = _inproj(xs, gpre, w_in[0], cos_s, sin_s, tm=bs, tn=INPROJ_COLS_SAMPLE)

    xp = x_prompt.reshape(bp * lp, D_MODEL)
    cos_p, sin_p = _rope_tables(jnp.arange(lp, dtype=jnp.int32))
    proj_s_grouped = proj_s[:, :4 * RET_WIDTH].astype(F32).reshape(
        bs // RET_SAMPLE_SEQS, RET_SAMPLE_SEQS, 4 * RET_WIDTH)
    proj_p, w_pa_b, w_pb_b, w_out_b, w_glu_b, ya_s, ret_s = _inproj(
        xp, gpre, w_in_b, cos_p, sin_p, tm=INPROJ_ROWS, tn=INPROJ_COLS_PROMPT,
        make_riders=[functools.partial(_cast_rider, w) for w in (w_pa[0], w_pb[0], w_out[0], s5_w_glu[0])]
        + [functools.partial(_retention_step_rider, proj_s_grouped, state_ret[0])])
    ya_s = ya_s.reshape(bs, RET_WIDTH)
    ya_p, ret_p, yb_p, s5r_p, s5i_p = _mixers_prompt(
        proj_p, wb, cw, ar, ai, d_skip, w_glu_b, b_glu, batch=bp, seq=lp, tm=MIXER_ROWS,
        sub=S5_SUBTILE_ROWS)
    y_p = _merge(ya_p, yb_p, proj_p, xp, w_pa_b, w_pb_b, w_out_b, gpost, tm=MERGE_ROWS)

    x0r = state_s5_re[0].reshape(bs, S5_GROUPS * S5_P)
    x0i = state_s5_im[0].reshape(bs, S5_GROUPS * S5_P)
    yb_s, s5r_s, s5i_s = _s5_sample(proj_s, x0r, x0i, wb, cw, ar, ai, d_skip, w_glu_b, b_glu)
    y_s = _merge(ya_s, yb_s, proj_s, xs, w_pa_b, w_pb_b, w_out_b, gpost, tm=bs)

    state_shape = (1, -1, S5_GROUPS, S5_P)
    return (y_p.reshape(bp, lp, D_MODEL), y_s.reshape(bs, 1, D_MODEL),
            ret_p[None], s5r_p.reshape(state_shape), s5i_p.reshape(state_shape),
            ret_s[None], s5r_s.reshape(state_shape), s5i_s.reshape(state_shape))
```

```python
import functools
import math
from typing import Callable, NamedTuple

import jax
import jax.numpy as jnp
import numpy as np
from jax import lax
from jax.experimental import pallas as pl
from jax.experimental.pallas import tpu as pltpu

F32 = jnp.float32
BF16 = jnp.bfloat16

D_MODEL = 2048
RET_WIDTH = 1024
RET_HEADS = 8
HEAD_DIM = RET_WIDTH // RET_HEADS
RET_CHUNK = 128
ROPE_BASE = 10000.0
S5_WIDTH = 1024
S5_GROUP = 16
S5_GROUPS = S5_WIDTH // S5_GROUP
S5_P = 64
PAST_LEN = 16384
EPS = 1e-6
IN_COLS = 4 * RET_WIDTH + 2 * S5_WIDTH + 2 * D_MODEL

LANES = 128
MXU_DIM = 256
BF16_SUBLANES = 16
S5_PAIRS = S5_GROUPS // 2
PAIRS_PER_BLOCK = LANES // (2 * S5_GROUP)
S5_BLOCKS = S5_WIDTH // LANES
S5_PITCH = 36
VMEM_LIMIT = 60 * 1024 * 1024

INPROJ_ROWS = 1024
INPROJ_COLS_PROMPT = 2048
INPROJ_COLS_SAMPLE = 1024
MIXER_ROWS = 512
S5_SUBTILE_ROWS = 128
MERGE_ROWS = 512
RET_SAMPLE_SEQS = 4

_LOG_GAMMA = [float(np.log(np.float32(1.0) - np.float32(2.0) ** np.float32(-5.0 - h)))
              for h in range(RET_HEADS)]


def _params(*sem):
    return pltpu.CompilerParams(dimension_semantics=sem, vmem_limit_bytes=VMEM_LIMIT)


def _resident(shape):
    nd = len(shape)
    return pl.BlockSpec(shape, lambda *_: (0,) * nd, pipeline_mode=pl.Buffered(1))


def _rope(x, cos, sin_signed):
    return x * cos + pltpu.roll(x, HEAD_DIM // 2, 1) * sin_signed


class _Rider(NamedTuple):
    arrays: tuple
    in_specs: tuple
    out_specs: tuple
    out_shapes: tuple
    body: Callable
    n_units: int


def _inproj_kernel(x_ref, g_ref, w_ref, cos_ref, sin_ref, *rest, slab, tn, emit_weights, riders):
    n_in = sum(len(r.in_specs) for r in riders)
    n_out = sum(len(r.out_specs) for r in riders)
    rider_in, rest = rest[:n_in], rest[n_in:]
    o_ref, h_ref = rest[0], rest[-1]
    wcopy_ref = rest[1] if emit_weights else None
    rider_out = rest[len(rest) - 1 - n_out:-1]
    j = pl.program_id(1)

    def rider_units():
        i = o = 0
        for r in riders:
            yield from r.body(rider_in[i:i + len(r.in_specs)], rider_out[o:o + len(r.out_specs)])
            i, o = i + len(r.in_specs), o + len(r.out_specs)

    @pl.when(j == 0)
    def _():
        g = g_ref[...]

        def body(r, _):
            rows = pl.ds(pl.multiple_of(r * slab, slab), slab)
            x = x_ref[rows, :]
            ms = jnp.mean(x * x, axis=-1, keepdims=True)
            h_ref[rows, :] = (x * lax.rsqrt(ms + EPS) * g).astype(BF16)
            return 0

        lax.fori_loop(0, x_ref.shape[0] // slab, body, 0)

    def rotate(s, acc):
        is_k = j * tn + s * MXU_DIM >= RET_WIDTH
        scale = jnp.where(is_k, HEAD_DIM ** -0.5, 1.0).astype(F32)
        cos = cos_ref[...] * scale
        sin = sin_ref[...] * scale
        return jnp.concatenate([_rope(acc[:, hh * HEAD_DIM:(hh + 1) * HEAD_DIM], cos, sin)
                                for hh in range(MXU_DIM // HEAD_DIM)], axis=1)

    epilogues = {"rope": rotate, "plain": lambda s, acc: acc}
    col_kinds = (("rope", 2 * RET_WIDTH), ("plain", IN_COLS - 2 * RET_WIDTH))
    slab_kinds = [kind for kind, width in col_kinds for _ in range(width // MXU_DIM)]
    per_tile = tn // MXU_DIM
    tiles_by_pattern = {}
    for tile in range(len(slab_kinds) // per_tile):
        pattern = tuple(slab_kinds[tile * per_tile:(tile + 1) * per_tile])
        tiles_by_pattern.setdefault(pattern, []).append(tile)

    def slabs(pattern, tiles):
        assert not (riders and 0 in tiles and len(tiles) > 1), "tile 0 needs its own branch"
        riders_active = tiles != [0]
        n_units = sum(r.n_units for r in riders) if riders_active else 0
        units, emitted = rider_units() if riders_active else iter(()), 0
        for s, kind in enumerate(pattern):
            cols = slice(s * MXU_DIM, (s + 1) * MXU_DIM)
            w = w_ref[:, cols]
            if wcopy_ref is not None:
                w = w.astype(BF16)
                wcopy_ref[:, cols] = w
            acc = jnp.dot(h_ref[...], w, preferred_element_type=F32)
            while emitted * len(pattern) < (s + 1) * n_units:
                next(units)
                emitted += 1
            o_ref[:, cols] = epilogues[kind](s, acc).astype(BF16)
        assert next(units, None) is None, "a rider emitted more units than it declared"

    for pattern, tiles in tiles_by_pattern.items():
        is_this_kind = functools.reduce(jnp.logical_or, [j == tile for tile in tiles])
        pl.when(is_this_kind)(functools.partial(slabs, pattern, tiles))


def _cast_rider(a, step_of, n_steps):
    rows = -(-a.shape[0] // n_steps)
    rows = -(-rows // BF16_SUBLANES) * BF16_SUBLANES
    assert a.shape[0] % rows == 0
    last = a.shape[0] // rows - 1
    spec = pl.BlockSpec((rows, a.shape[1]), lambda i, j: (jnp.minimum(step_of(i, j), last), 0))

    def body(ins, outs):
        outs[0][...] = ins[0][...].astype(BF16)
        yield

    return _Rider((a,), (spec,), (spec,), (jax.ShapeDtypeStruct(a.shape, BF16),), body, 1)


def _inproj(x2d, g_pre, w_in, cos, sin, *, tm, tn, make_riders=()):
    m, d = x2d.shape
    n = w_in.shape[1]
    pos_tiles = cos.shape[0] // tm
    emit_weights = w_in.dtype != BF16
    assert not emit_weights or m == tm, "each weight tile must be visited once to be copied out"
    n_col_tiles = n // tn
    rider_step = lambda i, j: i * (n_col_tiles - 1) + jnp.maximum(j - 1, 0)
    riders = tuple(make(rider_step, (m // tm) * (n_col_tiles - 1)) for make in make_riders)
    in_specs = [
        pl.BlockSpec((tm, d), lambda i, j: (i, 0)),
        pl.BlockSpec((1, d), lambda i, j: (0, 0)),
        pl.BlockSpec((d, tn), lambda i, j: (0, j)),
        pl.BlockSpec((tm, HEAD_DIM), lambda i, j: (i % pos_tiles, 0)),
        pl.BlockSpec((tm, HEAD_DIM), lambda i, j: (i % pos_tiles, 0)),
    ]
    out_specs = [pl.BlockSpec((tm, tn), lambda i, j: (i, j))]
    out_shape = [jax.ShapeDtypeStruct((m, n), BF16)]
    if emit_weights:
        out_specs.append(pl.BlockSpec((d, tn), lambda i, j: (0, j)))
        out_shape.append(jax.ShapeDtypeStruct((d, n), BF16))
    for r in riders:
        in_specs += r.in_specs
        out_specs += r.out_specs
        out_shape += r.out_shapes
    return pl.pallas_call(
        functools.partial(_inproj_kernel, slab=min(tm, 128), tn=tn, emit_weights=emit_weights,
                          riders=riders),
        grid=(m // tm, n_col_tiles),
        in_specs=in_specs,
        out_specs=out_specs,
        out_shape=out_shape,
        scratch_shapes=[pltpu.VMEM((tm, d), BF16)],
        compiler_params=_params("arbitrary", "arbitrary"),
        name="inproj",
    )(x2d, g_pre, w_in, cos, sin, *[a for r in riders for a in r.arrays])


def _group_norm(o):
    mu = jnp.mean(o, axis=-1, keepdims=True)
    d = o - mu
    var = jnp.mean(d * d, axis=-1, keepdims=True)
    return d * lax.rsqrt(var + EPS)


def _retention_tile(q_ref, k_ref, v_ref, z_ref, o_ref, s_ref):
    c = RET_CHUNK
    row = lax.broadcasted_iota(jnp.int32, (c, c), 0).astype(F32)
    col = lax.broadcasted_iota(jnp.int32, (c, c), 1).astype(F32)
    diff = row - col
    for h in range(RET_HEADS):
        lg = _LOG_GAMMA[h]
        mask = jnp.where(diff >= 0, jnp.exp(jnp.maximum(diff, 0.0) * lg), 0.0)
        q_decay = jnp.exp((row + 1.0) * lg)
        k_decay = jnp.exp((c - 1.0 - row) * lg)
        chunk_decay = math.exp(c * lg)
        cols = slice(h * HEAD_DIM, (h + 1) * HEAD_DIM)
        for ci in range(q_ref.shape[0] // c):
            rows = slice(ci * c, (ci + 1) * c)
            q = q_ref[rows, cols]
            k = k_ref[rows, cols]
            v = v_ref[rows, cols]
            s0 = s_ref[0, h]
            scores = lax.dot_general(q, k, (((1,), (1,)), ((), ())),
                                     preferred_element_type=F32) * mask
            inner = jnp.dot(scores.astype(BF16), v, preferred_element_type=F32)
            cross = jnp.dot(q, s0.astype(BF16), preferred_element_type=F32) * q_decay
            kd = (k.astype(F32) * k_decay).astype(BF16)
            s_ref[0, h] = chunk_decay * s0 + lax.dot_general(
                kd, v, (((0,), (0,)), ((), ())), preferred_element_type=F32)
            o = _group_norm(inner + cross)
            o_ref[rows, cols] = (o * jax.nn.silu(z_ref[rows, cols].astype(F32))).astype(BF16)
            yield


def _retention_step_rider(proj, state, step_of, n_steps):
    groups, bb, _ = proj.shape
    assert groups <= n_steps
    group = lambda i, j: jnp.minimum(step_of(i, j), groups - 1)
    col_spec = lambda cb: pl.BlockSpec((1, bb, RET_WIDTH), lambda i, j: (group(i, j), 0, cb))
    state_spec = pl.BlockSpec((bb, RET_HEADS, HEAD_DIM, HEAD_DIM), lambda i, j: (group(i, j), 0, 0, 0))

    def body(ins, outs):
        q_ref, k_ref, v_ref, z_ref, s_ref = ins
        o_ref, sn_ref = outs
        pad = jnp.zeros((HEAD_DIM - bb, HEAD_DIM), F32)
        for h in range(RET_HEADS):
            gamma = math.exp(_LOG_GAMMA[h])
            cols = slice(h * HEAD_DIM, (h + 1) * HEAD_DIM)
            q, k, v = q_ref[0, :, cols], k_ref[0, :, cols], v_ref[0, :, cols]
            qt = jnp.concatenate([q, pad], axis=0).T
            kt = jnp.concatenate([k, pad], axis=0).T
            o = []
            for b in range(bb):
                s_new = gamma * s_ref[b, h] + kt[:, b:b + 1] * v[b:b + 1, :]
                sn_ref[b, h] = s_new
                o.append(jnp.sum(qt[:, b:b + 1] * s_new, axis=0, keepdims=True))
            o_ref[0, :, cols] = _group_norm(jnp.concatenate(o, axis=0)) * jax.nn.silu(z_ref[0, :, cols])
            yield

    return _Rider(
        (proj, proj, proj, proj, state),
        (col_spec(0), col_spec(1), col_spec(2), col_spec(3), state_spec),
        (pl.BlockSpec((1, bb, RET_WIDTH), lambda i, j: (group(i, j), 0, 0)), state_spec),
        (jax.ShapeDtypeStruct((groups, bb, RET_WIDTH), F32), jax.ShapeDtypeStruct(state.shape, F32)),
        body, RET_HEADS)


def _s5_output_gate(y, z, wglu_ref, bglu_ref):
    y = jax.nn.gelu(y)
    g = jnp.dot(y.astype(BF16), wglu_ref[...], preferred_element_type=F32) + bglu_ref[...]
    return y * jax.nn.sigmoid(g) * jax.nn.silu(z.astype(F32))


def _s5_tile(u_ref, z_ref, wb_ref, cw_ref, ar_ref, ai_ref, d_ref, wglu_ref, bglu_ref,
             o_ref, sr_ref, si_ref, y_scr, x_scr, *, sub, filler=(), n_filler=0):
    n_sub = len(x_scr) // 2
    xr_scr, xi_scr = x_scr[:n_sub], x_scr[n_sub:]
    strided = lambda j: pl.ds(j, sub, stride=S5_PITCH)
    filler = iter(filler)
    progress = [0, 0]
    n_slots = n_sub * S5_PAIRS

    def fill():
        progress[0] += 1
        while progress[1] < n_filler and progress[1] * n_slots < progress[0] * n_filler:
            next(filler)
            progress[1] += 1

    for k in range(n_sub):
        rows = slice(k * sub, (k + 1) * sub)
        for blk in range(S5_BLOCKS):
            ub = u_ref[rows, blk * LANES:(blk + 1) * LANES]
            for j in range(blk * PAIRS_PER_BLOCK, (blk + 1) * PAIRS_PER_BLOCK):
                bu = jnp.dot(ub, wb_ref[j], preferred_element_type=F32)
                xr_scr[k][strided(j), :] = bu[:, :LANES]
                xi_scr[k][strided(j), :] = bu[:, LANES:]
                fill()
    ar = ar_ref[...]
    ai = ai_ref[...]
    xr, xi = sr_ref[0], si_ref[0]
    for k in range(n_sub):
        rows = slice(k * sub, (k + 1) * sub)
        for t in range(sub):
            step = slice(t * S5_PITCH, t * S5_PITCH + S5_PAIRS)
            xr, xi = (ar * xr - ai * xi + xr_scr[k][step, :],
                      ar * xi + ai * xr + xi_scr[k][step, :])
            xr_scr[k][step, :] = xr
            xi_scr[k][step, :] = xi
        for blk in range(S5_BLOCKS):
            cols = slice(blk * LANES, (blk + 1) * LANES)
            acc = d_ref[:, cols] * u_ref[rows, cols].astype(F32)
            for j in range(blk * PAIRS_PER_BLOCK, (blk + 1) * PAIRS_PER_BLOCK):
                x = jnp.concatenate([xr_scr[k][strided(j), :], xi_scr[k][strided(j), :]], axis=1)
                acc = acc + lax.dot_general(x.astype(BF16), cw_ref[j], (((1,), (1,)), ((), ())),
                                            preferred_element_type=F32)
            y_scr[rows, cols] = acc
        if k % 2 == 1 or k == n_sub - 1:
            rows = slice((k - k % 2) * sub, (k + 1) * sub)
            o_ref[rows, :] = _s5_output_gate(
                y_scr[rows, :], z_ref[rows, :], wglu_ref, bglu_ref).astype(BF16)
    sr_ref[0] = xr
    si_ref[0] = xi


def _mixers_kernel(q_ref, k_ref, v_ref, za_ref, u_ref, zb_ref, wb_ref, cw_ref, ar_ref, ai_ref, d_ref,
                   wglu_ref, bglu_ref, ya_ref, ret_ref, yb_ref, sr_ref, si_ref, y_scr, *x_scr, sub):
    @pl.when(pl.program_id(1) == 0)
    def _():
        ret_ref[...] = jnp.zeros_like(ret_ref)
        sr_ref[...] = jnp.zeros_like(sr_ref)
        si_ref[...] = jnp.zeros_like(si_ref)

    retention_units = RET_HEADS * (q_ref.shape[0] // RET_CHUNK)
    _s5_tile(u_ref, zb_ref, wb_ref, cw_ref, ar_ref, ai_ref, d_ref, wglu_ref, bglu_ref,
             yb_ref, sr_ref, si_ref, y_scr, x_scr, sub=sub,
             filler=_retention_tile(q_ref, k_ref, v_ref, za_ref, ya_ref, ret_ref),
             n_filler=retention_units)


def _mixers_prompt(proj, wb, cw, ar, ai, d, wglu, bglu, *, batch, seq, tm, sub):
    nt = seq // tm
    row_block = lambda b, i: b * nt + i
    col_spec = lambda cb: pl.BlockSpec((tm, RET_WIDTH), lambda b, i: (row_block(b, i), cb))
    s5_state_spec = pl.BlockSpec((1, S5_PAIRS, LANES), lambda b, i: (b, 0, 0))
    assert RET_WIDTH == S5_WIDTH
    return pl.pallas_call(
        functools.partial(_mixers_kernel, sub=sub),
        grid=(batch, nt),
        in_specs=[
            col_spec(0), col_spec(1), col_spec(2), col_spec(3), col_spec(4), col_spec(5),
            _resident(wb.shape), _resident(cw.shape), _resident(ar.shape), _resident(ai.shape),
            _resident(d.shape), _resident(wglu.shape), _resident(bglu.shape),
        ],
        out_specs=[
            pl.BlockSpec((tm, RET_WIDTH), lambda b, i: (row_block(b, i), 0)),
            pl.BlockSpec((1, RET_HEADS, HEAD_DIM, HEAD_DIM), lambda b, i: (b, 0, 0, 0)),
            pl.BlockSpec((tm, S5_WIDTH), lambda b, i: (row_block(b, i), 0)),
            s5_state_spec, s5_state_spec,
        ],
        out_shape=[
            jax.ShapeDtypeStruct((batch * seq, RET_WIDTH), BF16),
            jax.ShapeDtypeStruct((batch, RET_HEADS, HEAD_DIM, HEAD_DIM), F32),
            jax.ShapeDtypeStruct((batch * seq, S5_WIDTH), BF16),
            jax.ShapeDtypeStruct((batch, S5_PAIRS, LANES), F32),
            jax.ShapeDtypeStruct((batch, S5_PAIRS, LANES), F32),
        ],
        scratch_shapes=[pltpu.VMEM((tm, S5_WIDTH), F32)]
        + [pltpu.VMEM((sub * S5_PITCH, LANES), F32)] * (2 * (tm // sub)),
        compiler_params=_params("arbitrary", "arbitrary"),
        name="mixers_prompt",
    )(proj, proj, proj, proj, proj, proj, wb, cw, ar, ai, d, wglu, bglu)


def _s5_step_kernel(u_ref, z_ref, x0r_ref, x0i_ref, wb_ref, cw_ref, ar_ref, ai_ref, d_ref,
                    wglu_ref, bglu_ref, o_ref, sr_ref, si_ref, y_scr):
    for blk in range(S5_BLOCKS):
        cols = slice(blk * LANES, (blk + 1) * LANES)
        ub = u_ref[:, cols]
        acc = d_ref[:, cols] * ub.astype(F32)
        for j in range(blk * PAIRS_PER_BLOCK, (blk + 1) * PAIRS_PER_BLOCK):
            pc = slice(j * LANES, (j + 1) * LANES)
            bu = jnp.dot(ub, wb_ref[j], preferred_element_type=F32)
            ar = ar_ref[j:j + 1, :]
            ai = ai_ref[j:j + 1, :]
            x0r = x0r_ref[:, pc]
            x0i = x0i_ref[:, pc]
            nr = ar * x0r - ai * x0i + bu[:, :LANES]
            ni = ar * x0i + ai * x0r + bu[:, LANES:]
            sr_ref[:, pc] = nr
            si_ref[:, pc] = ni
            x = jnp.concatenate([nr, ni], axis=1).astype(BF16)
            acc = acc + lax.dot_general(x, cw_ref[j], (((1,), (1,)), ((), ())),
                                        preferred_element_type=F32)
        y_scr[:, cols] = acc
    o_ref[...] = _s5_output_gate(y_scr[...], z_ref[...], wglu_ref, bglu_ref).astype(BF16)


def _s5_sample(proj, x0r, x0i, wb, cw, ar, ai, d, wglu, bglu):
    nb = proj.shape[0]
    ub_col = (4 * RET_WIDTH) // S5_WIDTH
    full = lambda a: pl.BlockSpec(a.shape, lambda i: (0,) * a.ndim)
    return pl.pallas_call(
        _s5_step_kernel,
        grid=(1,),
        in_specs=[
            pl.BlockSpec((nb, S5_WIDTH), lambda i: (0, ub_col)),
            pl.BlockSpec((nb, S5_WIDTH), lambda i: (0, ub_col + 1)),
            full(x0r), full(x0i), full(wb), full(cw), full(ar), full(ai), full(d), full(wglu), full(bglu),
        ],
        out_specs=[
            pl.BlockSpec((nb, S5_WIDTH), lambda i: (0, 0)),
            full(x0r), full(x0i),
        ],
        out_shape=[
            jax.ShapeDtypeStruct((nb, S5_WIDTH), BF16),
            jax.ShapeDtypeStruct(x0r.shape, F32),
            jax.ShapeDtypeStruct(x0i.shape, F32),
        ],
        scratch_shapes=[pltpu.VMEM((nb, S5_WIDTH), F32)],
        compiler_params=_params("arbitrary"),
        name="s5_sample",
    )(proj, proj, x0r, x0i, wb, cw, ar, ai, d, wglu, bglu)


def _merge_kernel(ya_ref, yb_ref, ga_ref, gb_ref, x_ref, wpa_ref, wpb_ref, wout_ref, gpost_ref, o_ref,
                  merged_scr):
    slabs = [slice(s * MXU_DIM, (s + 1) * MXU_DIM) for s in range(D_MODEL // MXU_DIM)]
    ya_in = ya_ref[...].astype(BF16)
    yb_in = yb_ref[...].astype(BF16)
    for cols in slabs:
        ya = jnp.dot(ya_in, wpa_ref[:, cols], preferred_element_type=F32)
        yb = jnp.dot(yb_in, wpb_ref[:, cols], preferred_element_type=F32)
        merged_scr[:, cols] = (jax.nn.sigmoid(ga_ref[:, cols].astype(F32)) * ya
                               + jax.nn.sigmoid(gb_ref[:, cols].astype(F32)) * yb).astype(BF16)
    sq = jnp.zeros((o_ref.shape[0], 1), F32)
    for cols in slabs:
        out = jnp.dot(merged_scr[...], wout_ref[:, cols], preferred_element_type=F32)
        sq = sq + jnp.sum(out * out, axis=-1, keepdims=True)
        o_ref[:, cols] = out
    inv_rms = lax.rsqrt(sq * (1.0 / D_MODEL) + EPS)
    for cols in slabs:
        o_ref[:, cols] = x_ref[:, cols] + o_ref[:, cols] * inv_rms * gpost_ref[:, cols]


def _merge(ya, yb, proj, x2d, wpa, wpb, wout, gpost, *, tm):
    m = x2d.shape[0]
    ga_col = (4 * RET_WIDTH + 2 * S5_WIDTH) // D_MODEL
    return pl.pallas_call(
        _merge_kernel,
        grid=(m // tm,),
        in_specs=[
            pl.BlockSpec((tm, RET_WIDTH), lambda i: (i, 0)),
            pl.BlockSpec((tm, S5_WIDTH), lambda i: (i, 0)),
            pl.BlockSpec((tm, D_MODEL), lambda i: (i, ga_col)),
            pl.BlockSpec((tm, D_MODEL), lambda i: (i, ga_col + 1)),
            pl.BlockSpec((tm, D_MODEL), lambda i: (i, 0)),
            _resident(wpa.shape), _resident(wpb.shape), _resident(wout.shape), _resident(gpost.shape),
        ],
        out_specs=pl.BlockSpec((tm, D_MODEL), lambda i: (i, 0)),
        out_shape=jax.ShapeDtypeStruct((m, D_MODEL), F32),
        scratch_shapes=[pltpu.VMEM((tm, D_MODEL), BF16)],
        compiler_params=_params("arbitrary"),
        name="merge_out",
    )(ya, yb, proj, proj, x2d, wpa, wpb, wout, gpost)


def _rope_tables(pos):
    half = HEAD_DIM // 2
    inv = ROPE_BASE ** (-jnp.arange(half, dtype=F32) / half)
    ang = pos.astype(F32)[:, None] * inv[None, :]
    cos, sin = jnp.cos(ang), jnp.sin(ang)
    return jnp.concatenate([cos, cos], axis=-1), jnp.concatenate([-sin, sin], axis=-1)


def _s5_discretize(lam_re, lam_im, log_dt, b_re, b_im):
    dt = jnp.exp(log_dt)[:, None]
    mag = jnp.exp(lam_re * dt)
    abar_r = mag * jnp.cos(lam_im * dt)
    abar_i = mag * jnp.sin(lam_im * dt)
    nr, ni = abar_r - 1.0, abar_i
    den = lam_re * lam_re + lam_im * lam_im
    coef_r = (nr * lam_re + ni * lam_im) / den
    coef_i = (ni * lam_re - nr * lam_im) / den
    bbar_r = coef_r[:, :, None] * b_re - coef_i[:, :, None] * b_im
    bbar_i = coef_r[:, :, None] * b_im + coef_i[:, :, None] * b_re
    return abar_r, abar_i, bbar_r, bbar_i


def _s5_pair_weights(bbar_r, bbar_i, c_re, c_im):
    j = lax.broadcasted_iota(jnp.int32, (S5_PAIRS, LANES, 2 * LANES), 0)
    row = lax.broadcasted_iota(jnp.int32, (S5_PAIRS, LANES, 2 * LANES), 1)
    col = lax.broadcasted_iota(jnp.int32, (S5_PAIRS, LANES, 2 * LANES), 2)
    pair_lanes = 2 * S5_GROUP
    keep = ((row // pair_lanes == j % PAIRS_PER_BLOCK)
            & ((row // S5_GROUP) % 2 == (col // S5_P) % 2))

    def expand(re, im, perm):
        base = jnp.stack([re, im]).reshape((2, S5_PAIRS, 2) + re.shape[1:]).transpose(perm)
        base = base.reshape(S5_PAIRS, 1, S5_GROUP, 2 * LANES)
        tiled = jnp.broadcast_to(base, (S5_PAIRS, LANES // S5_GROUP, S5_GROUP, 2 * LANES))
        return jnp.where(keep, tiled.reshape(S5_PAIRS, LANES, 2 * LANES), 0.0).astype(BF16)

    wb = expand(bbar_r, bbar_i, (1, 4, 0, 2, 3))
    cw = expand(c_re, -c_im, (1, 3, 0, 2, 4))
    return wb, cw


def kernel(x_prompt, x_sample, state_ret, state_s5_re, state_s5_im, g_pre, w_in, w_pa, w_pb, w_out, g_post,
           s5_lam_re, s5_lam_im, s5_log_dt, s5_b_re, s5_b_im, s5_c_re, s5_c_im, s5_d, s5_w_glu, s5_b_glu):
    assert w_in.shape[0] == 1, "single trunk layer"
    bp, lp, _ = x_prompt.shape
    bs, ls, _ = x_sample.shape
    assert ls == 1 and lp % RET_CHUNK == 0

    b_glu = s5_b_glu[0].reshape(1, S5_WIDTH)
    gpre = g_pre[0].reshape(1, D_MODEL)
    gpost = g_post[0].reshape(1, D_MODEL)

    abar_r, abar_i, bbar_r, bbar_i = _s5_discretize(
        s5_lam_re[0], s5_lam_im[0], s5_log_dt[0], s5_b_re[0], s5_b_im[0])
    wb, cw = _s5_pair_weights(bbar_r, bbar_i, s5_c_re[0], s5_c_im[0])
    ar = abar_r.reshape(S5_PAIRS, LANES)
    ai = abar_i.reshape(S5_PAIRS, LANES)
    d_skip = s5_d[0].reshape(1, S5_WIDTH)

    xs = x_sample.reshape(bs, D_MODEL)
    cos_s, sin_s = _rope_tables(jnp.full((bs,), PAST_LEN, jnp.int32))
    proj_s, w_in_b = _inproj(xs, gpre, w_in[0], cos_s, sin_s, tm=bs, tn=INPROJ_COLS_SAMPLE)

    xp = x_prompt.reshape(bp * lp, D_MODEL)
    cos_p, sin_p = _rope_tables(jnp.arange(lp, dtype=jnp.int32))
    proj_s_grouped = proj_s[:, :4 * RET_WIDTH].astype(F32).reshape(
        bs // RET_SAMPLE_SEQS, RET_SAMPLE_SEQS, 4 * RET_WIDTH)
    proj_p, w_pa_b, w_pb_b, w_out_b, w_glu_b, ya_s, ret_s = _inproj(
        xp, gpre, w_in_b, cos_p, sin_p, tm=INPROJ_ROWS, tn=INPROJ_COLS_PROMPT,
        make_riders=[functools.partial(_cast_rider, w) for w in (w_pa[0], w_pb[0], w_out[0], s5_w_glu[0])]
        + [functools.partial(_retention_step_rider, proj_s_grouped, state_ret[0])])
    ya_s = ya_s.reshape(bs, RET_WIDTH)
    ya_p, ret_p, yb_p, s5r_p, s5i_p = _mixers_prompt(
        proj_p, wb, cw, ar, ai, d_skip, w_glu_b, b_glu, batch=bp, seq=lp, tm=MIXER_ROWS,
        sub=S5_SUBTILE_ROWS)
    y_p = _merge(ya_p, yb_p, proj_p, xp, w_pa_b, w_pb_b, w_out_b, gpost, tm=MERGE_ROWS)

    x0r = state_s5_re[0].reshape(bs, S5_GROUPS * S5_P)
    x0i = state_s5_im[0].reshape(bs, S5_GROUPS * S5_P)
    yb_s, s5r_s, s5i_s = _s5_sample(proj_s, x0r, x0i, wb, cw, ar, ai, d_skip, w_glu_b, b_glu)
    y_s = _merge(ya_s, yb_s, proj_s, xs, w_pa_b, w_pb_b, w_out_b, gpost, tm=bs)

    state_shape = (1, -1, S5_GROUPS, S5_P)
    return (y_p.reshape(bp, lp, D_MODEL), y_s.reshape(bs, 1, D_MODEL),
            ret_p[None], s5r_p.reshape(state_shape), s5i_p.reshape(state_shape),
            ret_s[None], s5r_s.reshape(state_shape), s5i_s.reshape(state_shape))
```
